```python
import math
import jax, jax.numpy as jnp
from jax import lax
import numpy as np

D_MODEL = 1024
BATCH = 32
SEQ = 256
DEPTH = 2
DEC_BATCH = 4
DEC_SEQ = 1024
PAST_LEN = 256

GRID_W = 64
W_MIX = D_MODEL
W_RET = 3 * D_MODEL // 8
H_RET = 6
DK_RET = W_RET // H_RET
DV_RET = W_RET // H_RET
W_SSD = 3 * D_MODEL // 8
P_SSD = 64
H_SSD = W_SSD // P_SSD
N_SSD = 128
G_SSD = 2
SSD_CONV = 4
CONV_CH = W_SSD + 2 * G_SSD * N_SSD
W_LRU = W_MIX - W_RET - W_SSD
LRU_BLOCKS = 4
LRU_BW = W_LRU // LRU_BLOCKS
LRU_CONV = 4
LRU_C = 8.0
D_FF = ((8 * D_MODEL // 3 + 127) // 128) * 128
FFN_CONV = 3
CHUNK = 64
ROPE_BASE = 10000.0
EPS = 1e-6
IN_DIM = 4 * W_RET + W_SSD + CONV_CH + H_SSD + 2 * W_LRU

kernel_name = "hybrid_ret_ssd_rglru_diffusion_step"

F32 = jnp.float32


def rmsnorm(x, g):
    xf = x.astype(F32)
    y = xf * lax.rsqrt(jnp.mean(xf * xf, axis=-1, keepdims=True) + EPS)
    return (y * g.astype(F32)).astype(x.dtype)


def dwconv(x, w, b):
    K, C = w.shape
    left = (K - 1) // 2
    right = K - 1 - left
    y = lax.conv_general_dilated(x, w.astype(x.dtype)[:, None, :], window_strides=(1,),
                                 padding=[(left, right)], dimension_numbers=('NWC', 'WIO', 'NWC'),
                                 feature_group_count=C)
    return y + b.astype(x.dtype)


def rope_1d(x, pos):
    half = x.shape[-1] // 2
    freqs = ROPE_BASE ** (-jnp.arange(half, dtype=F32) / half)
    ang = pos.astype(F32)[:, None] * freqs
    cos, sin = jnp.cos(ang), jnp.sin(ang)
    x1, x2 = x[..., :half].astype(F32), x[..., half:].astype(F32)
    return jnp.concatenate([x1 * cos - x2 * sin, x1 * sin + x2 * cos], axis=-1).astype(x.dtype)


def rope_2d(x, rows, cols):
    h = x.shape[-1] // 2
    return jnp.concatenate([rope_1d(x[..., :h], rows), rope_1d(x[..., h:], cols)], axis=-1)


def chunked_scan(q, k, v, log_a, s0):
    Bsz, H, T, K = q.shape
    V = v.shape[-1]
    n = T // CHUNK
    q = q.reshape(Bsz, H, n, CHUNK, K)
    k = k.reshape(Bsz, H, n, CHUNK, K)
    v = v.reshape(Bsz, H, n, CHUNK, V)
    cum = jnp.cumsum(log_a.astype(F32).reshape(Bsz, H, n, CHUNK), axis=-1)
    idx = jnp.arange(CHUNK)
    lower = idx[:, None] >= idx[None, :]
    decay = jnp.exp(jnp.where(lower, cum[..., :, None] - cum[..., None, :], -jnp.inf))
    scores = jnp.einsum('bhnik,bhnmk->bhnim', q, k) * decay
    o_intra = jnp.einsum('bhnim,bhnmv->bhniv', scores, v)
    to_end = jnp.exp(cum[..., -1:] - cum)
    chunk_states = jnp.einsum('bhnmk,bhnm,bhnmv->bhnkv', k, to_end, v)
    chunk_decay = jnp.exp(cum[..., -1])

    def step(s, inp):
        cs, cd = inp
        return cd[..., None, None] * s + cs, s

    s_final, s_prev = lax.scan(step, s0.astype(F32),
                               (jnp.moveaxis(chunk_states, 2, 0), jnp.moveaxis(chunk_decay, 2, 0)))
    s_prev = jnp.moveaxis(s_prev, 0, 2)
    o_inter = jnp.einsum('bhnik,bhni,bhnkv->bhniv', q, jnp.exp(cum), s_prev)
    return (o_intra + o_inter).reshape(Bsz, H, T, V), s_final


def bidir_scan(q, k, v_f, v_b, la_f, la_b, s0_f, s0_b):
    o_f, s_f = chunked_scan(q, k, v_f, la_f, s0_f)
    fl = lambda t: jnp.flip(t, axis=2)
    o_b, s_b = chunked_scan(fl(q), fl(k), fl(v_b), fl(la_b), s0_b)
    return o_f + fl(o_b), s_f, s_b


def rglru_scan(log_a, u, h0):
    a = jnp.exp(log_a)
    b = jnp.sqrt(-jnp.expm1(2.0 * log_a)) * u.astype(F32)

    def combine(l, r):
        al, bl = l
        ar, br = r
        return al * ar, ar * bl + br

    A, hs = lax.associative_scan(combine, (a, b), axis=1)
    hs = hs + A * h0.astype(F32)[:, None, :]
    return hs, hs[:, -1]


def mixer(h, states, p, pos):
    Bsz, T, _ = h.shape
    s_ret, s_ssd, s_lru = states
    proj = h @ p['w_in']
    sizes = (W_RET, W_RET, W_RET, W_RET, W_SSD, CONV_CH, H_SSD, W_LRU, W_LRU)
    offs = tuple(int(o) for o in np.cumsum(sizes)[:-1])
    q, k, v, g, z, xbc, dt_raw, xl, gl = jnp.split(proj, offs, axis=-1)
    heads = lambda t, nh: t.reshape(Bsz, T, nh, -1).transpose(0, 2, 1, 3)

    q, k, v = heads(q, H_RET), heads(k, H_RET), heads(v, H_RET)
    if pos is not None:
        q = rope_2d(q, pos[0], pos[1])
        k = rope_2d(k, pos[0], pos[1])
    q = q * (DK_RET ** -0.5)
    log_g = jax.nn.log_sigmoid(p['ret_decay'].astype(F32))
    la_f = jnp.broadcast_to(log_g[0][None, :, None], (Bsz, H_RET, T))
    la_b = jnp.broadcast_to(log_g[1][None, :, None], (Bsz, H_RET, T))
    o, rf, rb = bidir_scan(q, k, v, v, la_f, la_b, s_ret[:, 0], s_ret[:, 1])
    o = o.transpose(0, 2, 1, 3)
    mu = jnp.mean(o, axis=-1, keepdims=True)
    var = jnp.mean(jnp.square(o - mu), axis=-1, keepdims=True)
    o = ((o - mu) * lax.rsqrt(var + EPS)).reshape(Bsz, T, W_RET)
    y_ret = (o * p['ret_norm_g'].astype(F32) * jax.nn.silu(g.astype(F32))).astype(h.dtype)

    xbc = jax.nn.silu(dwconv(xbc, p['ssd_conv_w'], p['ssd_conv_b']))
    xs, bm, cm = jnp.split(xbc, (W_SSD, W_SSD + G_SSD * N_SSD), axis=-1)
    xs_h = heads(xs, H_SSD)
    rep = H_SSD // G_SSD
    bm = jnp.repeat(heads(bm, G_SSD), rep, axis=1)
    cm = jnp.repeat(heads(cm, G_SSD), rep, axis=1)
    dt_raw = dt_raw.astype(F32)
    dt_f = jax.nn.softplus(dt_raw + p['ssd_dt_bias'][0].astype(F32)).transpose(0, 2, 1)
    dt_b = jax.nn.softplus(dt_raw + p['ssd_dt_bias'][1].astype(F32)).transpose(0, 2, 1)
    A = -jnp.exp(p['ssd_a_log'].astype(F32))
    o, sf, sb = bidir_scan(cm, bm, xs_h * dt_f[..., None], xs_h * dt_b[..., None],
                           dt_f * A[0][None, :, None], dt_b * A[1][None, :, None],
                           s_ssd[:, 0], s_ssd[:, 1])
    y = o + p['ssd_d'].astype(F32)[None, :, None, None] * xs_h
    y = y.transpose(0, 2, 1, 3).reshape(Bsz, T, W_SSD)
    y_ssd = rmsnorm(y * jax.nn.silu(z.astype(F32)), p['ssd_norm_g']).astype(h.dtype)

    xl = dwconv(xl, p['lru_conv_w'], p['lru_conv_b'])

    def lru_dir(xd, d, h0):
        xb = xd.reshape(Bsz, T, LRU_BLOCKS, LRU_BW)
        r = jax.nn.sigmoid(jnp.einsum('btnc,ncd->btnd', xb, p['lru_w_a'][d]).reshape(Bsz, T, W_LRU)
                           + p['lru_b_a'][d]).astype(F32)
        i = jax.nn.sigmoid(jnp.einsum('btnc,ncd->btnd', xb, p['lru_w_x'][d]).reshape(Bsz, T, W_LRU)
                           + p['lru_b_x'][d]).astype(F32)
        log_a = -LRU_C * r * jax.nn.softplus(-p['lru_lambda'][d].astype(F32))
        return rglru_scan(log_a, i * xd.astype(F32), h0)

    hf, lf = lru_dir(xl, 0, s_lru[:, 0])
    hb, lb = lru_dir(jnp.flip(xl, axis=1), 1, s_lru[:, 1])
    y_lru = ((hf + jnp.flip(hb, axis=1)) * jax.nn.gelu(gl.astype(F32))).astype(h.dtype)

    out = jnp.concatenate([y_ret, y_ssd, y_lru], axis=-1) @ p['w_out']
    new_states = (jnp.stack([rf, rb], axis=1), jnp.stack([sf, sb], axis=1), jnp.stack([lf, lb], axis=1))
    return out, new_states


def conv_ffn(h, p):
    u = dwconv(h @ p['ffn_w_up'], p['ffn_conv_w'], p['ffn_conv_b'])
    val, gate = jnp.split(u, 2, axis=-1)
    return (jax.nn.silu(gate) * val) @ p['ffn_w_down']


def trunk_layer(x, mod, states, p, pos):
    sh1, sc1, g1, sh2, sc2, g2 = jnp.split(mod[:, None, :].astype(x.dtype), 6, axis=-1)
    h = rmsnorm(x, p['norm1_g']) * (1.0 + sc1) + sh1
    mix, new_states = mixer(h, states, p, pos)
    x = x + g1 * mix
    h = rmsnorm(x, p['norm2_g']) * (1.0 + sc2) + sh2
    x = x + g2 * conv_ffn(h, p)
    return x, new_states


def setup_inputs(seed: int = 0) -> dict:
    key = jax.random.key(seed)
    ks = jax.random.split(key, 40)
    nrm = lambda k, shape, s: jax.random.normal(k, shape, F32) * s
    gain = lambda k, shape: 1.0 + nrm(k, shape, 0.02)
    g0 = 1.0 - 2.0 ** (-5.0 - jnp.arange(H_RET, dtype=F32))
    ret_base = jnp.log(g0) - jnp.log1p(-g0)
    dt0 = jnp.exp(jax.random.uniform(ks[16], (DEPTH, 2, H_SSD), F32, math.log(1e-3), math.log(1e-1)))
    u0 = jax.random.uniform(ks[26], (DEPTH, 2, W_LRU), F32, 0.9, 0.999)
    a0 = u0 ** (1.0 / LRU_C)
    return {
        "x_prompt": nrm(ks[0], (BATCH, SEQ, D_MODEL), 1.0),
        "x_sample": nrm(ks[1], (DEC_BATCH, DEC_SEQ, D_MODEL), 1.0),
        "c": nrm(ks[2], (DEC_BATCH, D_MODEL), 1.0),
        "c_ctx": nrm(ks[3], (D_MODEL,), 1.0),
        "state_ret": nrm(ks[4], (DEC_BATCH, DEPTH, 2, H_RET, DK_RET, DV_RET), 1.0),
        "state_ssd": nrm(ks[5], (DEC_BATCH, DEPTH, 2, H_SSD, N_SSD, P_SSD), 0.1),
        "state_lru": nrm(ks[6], (DEC_BATCH, DEPTH, 2, W_LRU), 0.5),
        "w_ada": nrm(ks[7], (DEPTH, D_MODEL, 6 * D_MODEL), 0.5 * D_MODEL ** -0.5),
        "b_ada": nrm(ks[8], (DEPTH, 6 * D_MODEL), 0.01),
        "norm1_g": gain(ks[9], (DEPTH, D_MODEL)),
        "norm2_g": gain(ks[10], (DEPTH, D_MODEL)),
        "w_in": nrm(ks[11], (DEPTH, D_MODEL, IN_DIM), D_MODEL ** -0.5),
        "ret_decay": ret_base + nrm(ks[12], (DEPTH, 2, H_RET), 0.1),
        "ret_norm_g": gain(ks[13], (DEPTH, W_RET)),
        "ssd_conv_w": nrm(ks[14], (DEPTH, SSD_CONV, CONV_CH), SSD_CONV ** -0.5),
        "ssd_conv_b": nrm(ks[15], (DEPTH, CONV_CH), 0.01),
        "ssd_dt_bias": dt0 + jnp.log(-jnp.expm1(-dt0)),
        "ssd_a_log": jnp.log(jax.random.uniform(ks[17], (DEPTH, 2, H_SSD), F32, 1.0, 16.0)),
        "ssd_d": gain(ks[18], (DEPTH, H_SSD)),
        "ssd_norm_g": gain(ks[19], (DEPTH, W_SSD)),
        "lru_conv_w": nrm(ks[20], (DEPTH, LRU_CONV, W_LRU), LRU_CONV ** -0.5),
        "lru_conv_b": nrm(ks[21], (DEPTH, W_LRU), 0.01),
        "lru_w_a": nrm(ks[22], (DEPTH, 2, LRU_BLOCKS, LRU_BW, LRU_BW), LRU_BW ** -0.5),
        "lru_b_a": nrm(ks[23], (DEPTH, 2, W_LRU), 0.01),
        "lru_w_x": nrm(ks[24], (DEPTH, 2, LRU_BLOCKS, LRU_BW, LRU_BW), LRU_BW ** -0.5),
        "lru_b_x": nrm(ks[25], (DEPTH, 2, W_LRU), 0.01),
        "lru_lambda": jnp.log(a0) - jnp.log1p(-a0),
        "w_out": nrm(ks[27], (DEPTH, W_MIX, D_MODEL), W_MIX ** -0.5),
        "ffn_w_up": nrm(ks[28], (DEPTH, D_MODEL, 2 * D_FF), D_MODEL ** -0.5),
        "ffn_conv_w": nrm(ks[29], (DEPTH, FFN_CONV, 2 * D_FF), FFN_CONV ** -0.5),
        "ffn_conv_b": nrm(ks[30], (DEPTH, 2 * D_FF), 0.01),
        "ffn_w_down": nrm(ks[31], (DEPTH, D_FF, D_MODEL), D_FF ** -0.5),
        "final_norm_g": gain(ks[32], (D_MODEL,)),
    }


def reference(x_prompt, x_sample, c, c_ctx, state_ret, state_ssd, state_lru, w_ada, b_ada,
              norm1_g, norm2_g, w_in, ret_decay, ret_norm_g, ssd_conv_w, ssd_conv_b, ssd_dt_bias,
              ssd_a_log, ssd_d, ssd_norm_g, lru_conv_w, lru_conv_b, lru_w_a, lru_b_a, lru_w_x,
              lru_b_x, lru_lambda, w_out, ffn_w_up, ffn_conv_w, ffn_conv_b, ffn_w_down, final_norm_g):
    Bp = x_prompt.shape[0]
    T_lat = x_sample.shape[1]
    ROWS = T_lat // GRID_W
    rows = jnp.repeat(jnp.arange(ROWS), GRID_W)
    cols = jnp.tile(jnp.arange(GRID_W), ROWS)
    zero_states = (jnp.zeros((Bp, 2, H_RET, DK_RET, DV_RET), F32),
                   jnp.zeros((Bp, 2, H_SSD, N_SSD, P_SSD), F32),
                   jnp.zeros((Bp, 2, W_LRU), F32))
    xp, xs = x_prompt, x_sample
    new_ret, new_ssd, new_lru = [], [], []
    for l in range(DEPTH):
        p = {
            'norm1_g': norm1_g[l], 'norm2_g': norm2_g[l], 'w_in': w_in[l],
            'ret_decay': ret_decay[l], 'ret_norm_g': ret_norm_g[l],
            'ssd_conv_w': ssd_conv_w[l], 'ssd_conv_b': ssd_conv_b[l], 'ssd_dt_bias': ssd_dt_bias[l],
            'ssd_a_log': ssd_a_log[l], 'ssd_d': ssd_d[l], 'ssd_norm_g': ssd_norm_g[l],
            'lru_conv_w': lru_conv_w[l], 'lru_conv_b': lru_conv_b[l], 'lru_w_a': lru_w_a[l],
            'lru_b_a': lru_b_a[l], 'lru_w_x': lru_w_x[l], 'lru_b_x': lru_b_x[l],
            'lru_lambda': lru_lambda[l], 'w_out': w_out[l], 'ffn_w_up': ffn_w_up[l],
            'ffn_conv_w': ffn_conv_w[l], 'ffn_conv_b': ffn_conv_b[l], 'ffn_w_down': ffn_w_down[l],
        }
        mod_ctx = (jax.nn.silu(c_ctx) @ w_ada[l] + b_ada[l])[None, :]
        mod_lat = jax.nn.silu(c) @ w_ada[l] + b_ada[l]
        xp, (sr, ss, sl) = trunk_layer(xp, mod_ctx, zero_states, p, None)
        new_ret.append(sr)
        new_ssd.append(ss)
        new_lru.append(sl)
        xs, _ = trunk_layer(xs, mod_lat, (state_ret[:, l], state_ssd[:, l], state_lru[:, l]), p, (rows, cols))
    y_prompt = rmsnorm(xp, final_norm_g)
    y_sample = rmsnorm(xs, final_norm_g)
    new_state_ret = jnp.stack(new_ret, axis=1).astype(x_prompt.dtype)
    new_state_ssd = jnp.stack(new_ssd, axis=1).astype(x_prompt.dtype)
    new_state_lru = jnp.stack(new_lru, axis=1).astype(x_prompt.dtype)
    return (y_prompt, y_sample, new_state_ret, new_state_ssd, new_state_lru)
```

```python
import functools
import math

import jax
import jax.numpy as jnp
from jax import lax
from jax.experimental import pallas as pl
from jax.experimental.pallas import tpu as pltpu

F32 = jnp.float32
BF16 = jnp.bfloat16

D_MODEL = 1024
DEPTH = 2
GRID_W = 64
W_RET = 384
H_RET = 6
DK_RET = 64
W_SSD = 384
P_SSD = 64
H_SSD = 6
N_SSD = 128
G_SSD = 2
CONV_CH = W_SSD + 2 * G_SSD * N_SSD
W_LRU = 256
LRU_BLOCKS = 4
LRU_BW = 64
LRU_C = 8.0
D_FF = 2816
ROPE_BASE = 10000.0
EPS = 1e-6

LANES = 128
HEAD_PAIR = LANES
SCAN_CHUNK = 256
FF_BLOCK = 256
N_FF_BLOCKS = D_FF // FF_BLOCK
ROW_TILE = 512
FFN_ROW_TILE = 1024
PROJ_COL_BLOCK = 512
VMEM_LIMIT = 56 * 1024 * 1024

W_GRP_R = 4 * W_RET
W_GRP_S = W_SSD + CONV_CH + LANES
W_GRP_L = 2 * W_LRU
DT_BWD_LANE = 8


def _dot(a, b):
    return jnp.dot(a, b, preferred_element_type=F32)


def _dot_nt(a, b):
    return lax.dot_general(a, b, (((1,), (1,)), ((), ())), preferred_element_type=F32)


def _sigmoid(x):
    return 1.0 / (1.0 + jnp.exp(-x))


def _silu(x):
    return x * _sigmoid(x)


def _softplus(x):
    return jnp.maximum(x, 0.0) + jnp.log1p(jnp.exp(-jnp.abs(x)))


def _log_sigmoid(x):
    return -_softplus(-x)


def _gelu_tanh(x):
    c = math.sqrt(2.0 / math.pi)
    return 0.5 * x * (1.0 + jnp.tanh(c * (x + 0.044715 * (x * x * x))))


def _rms_scale(x):
    return x * lax.rsqrt(jnp.mean(x * x, axis=-1, keepdims=True) + EPS)


def _const_spec(shape):
    zeros = (0,) * len(shape)
    return pl.BlockSpec(shape, lambda *_: zeros, pipeline_mode=pl.Buffered(1))


def _params(n_axes):
    return pltpu.CompilerParams(dimension_semantics=("arbitrary",) * n_axes,
                                vmem_limit_bytes=VMEM_LIMIT)


def _mod_body(c_ref, w_ref, b_ref, o_ref):
    s = _silu(c_ref[...]).astype(BF16)
    o_ref[0] = _dot(s, w_ref[0].astype(BF16)) + b_ref[0]


def _modulation(cvec, w_ada, b_ada):
    rows = cvec.shape[0]
    nblk = (6 * D_MODEL) // D_MODEL
    return pl.pallas_call(
        _mod_body,
        grid=(DEPTH, nblk),
        in_specs=[pl.BlockSpec((rows, D_MODEL), lambda l, j: (0, 0)),
                  pl.BlockSpec((1, D_MODEL, D_MODEL), lambda l, j: (l, 0, j)),
                  pl.BlockSpec((1, 1, D_MODEL), lambda l, j: (l, 0, j))],
        out_specs=pl.BlockSpec((1, rows, D_MODEL), lambda l, j: (l, 0, j)),
        out_shape=jax.ShapeDtypeStruct((DEPTH, rows, 6 * D_MODEL), F32),
        compiler_params=_params(2),
        name="adaln_mod",
    )(cvec, w_ada, b_ada.reshape(DEPTH, 1, 6 * D_MODEL))


def _inproj_body(x_ref, mod_ref, g_ref, wr_ref, ws_ref, wl_ref, or_ref, os_ref, ol_ref):
    y = _rms_scale(x_ref[...]) * g_ref[...]
    h = (y * (1.0 + mod_ref[0, 1:2, :]) + mod_ref[0, 0:1, :]).astype(BF16)
    for w_ref, o_ref in ((wr_ref, or_ref), (ws_ref, os_ref), (wl_ref, ol_ref)):
        n = w_ref.shape[1]
        for j in range(0, n, PROJ_COL_BLOCK):
            jb = min(PROJ_COL_BLOCK, n - j)
            o_ref[:, j:j + jb] = _dot(h, w_ref[:, j:j + jb])


def _in_proj(x2d, mod, gain, wr, ws, wl):
    n = x2d.shape[0]
    rows_per_mod = n // mod.shape[0]
    tm = ROW_TILE
    return pl.pallas_call(
        _inproj_body,
        grid=(n // tm,),
        in_specs=[pl.BlockSpec((tm, D_MODEL), lambda i: (i, 0)),
                  pl.BlockSpec((1, 6, D_MODEL), lambda i: (i * tm // rows_per_mod, 0, 0)),
                  _const_spec((1, D_MODEL)),
                  _const_spec(wr.shape), _const_spec(ws.shape), _const_spec(wl.shape)],
        out_specs=[pl.BlockSpec((tm, W_GRP_R), lambda i: (i, 0)),
                   pl.BlockSpec((tm, W_GRP_S), lambda i: (i, 0)),
                   pl.BlockSpec((tm, W_GRP_L), lambda i: (i, 0))],
        out_shape=[jax.ShapeDtypeStruct((n, W_GRP_R), F32),
                   jax.ShapeDtypeStruct((n, W_GRP_S), F32),
                   jax.ShapeDtypeStruct((n, W_GRP_L), F32)],
        compiler_params=_params(1),
        name="in_proj",
    )(x2d, mod, gain, wr, ws, wl)


def _shift_rows(x, k, pos, seq_len):
    n = x.shape[0]
    if k == 0:
        return x
    rolled = pltpu.roll(x, (-k) % n, 0)
    ok = (pos >= -k) if k < 0 else (pos <= seq_len - 1 - k)
    return jnp.where(ok, rolled, 0.0)


def _dwconv4(x, w_ref, b_ref, col0, pos, seq_len):
    width = x.shape[1]
    cols = slice(col0, col0 + width)
    acc = x * w_ref[1:2, cols] + b_ref[0:1, cols]
    acc = acc + _shift_rows(x, -1, pos, seq_len) * w_ref[0:1, cols]
    acc = acc + _shift_rows(x, 1, pos, seq_len) * w_ref[2:3, cols]
    acc = acc + _shift_rows(x, 2, pos, seq_len) * w_ref[3:4, cols]
    return acc


def _head_lane_expand(x, lane0):
    rows = x.shape[0]
    low = lax.broadcasted_iota(jnp.int32, (rows, HEAD_PAIR), 1) < DK_RET
    blocks = []
    for p in range(H_RET // 2):
        a = jnp.broadcast_to(x[:, lane0 + 2 * p:lane0 + 2 * p + 1], (rows, HEAD_PAIR))
        b = jnp.broadcast_to(x[:, lane0 + 2 * p + 1:lane0 + 2 * p + 2], (rows, HEAD_PAIR))
        blocks.append(jnp.where(low, a, b))
    return jnp.concatenate(blocks, axis=1)


def _ret_body(*refs, seq_len, rope, has_s0, emit_state):
    it = iter(refs)
    p_ref = next(it)
    dec_lane_ref = next(it)
    dec_wide_ref = next(it)
    gn_ref = next(it)
    cos_ref = sin_ref = s0_ref = st_ref = qk_ref = None
    if rope:
        cos_ref = next(it)
        sin_ref = next(it)
    if has_s0:
        s0_ref = next(it)
    y_ref = next(it)
    if emit_state:
        st_ref = next(it)
    if rope:
        qk_ref = next(it)

    C = SCAN_CHUNK
    nch = seq_len // C
    npair = H_RET // 2
    lgl = _log_sigmoid(dec_lane_ref[...])
    lgw = _log_sigmoid(dec_wide_ref[...])

    ii = lax.broadcasted_iota(jnp.int32, (C, C), 0)
    mm = lax.broadcasted_iota(jnp.int32, (C, C), 1)
    dif = (ii - mm).astype(F32)
    lower = ii >= mm
    upper = mm >= ii
    neg_inf = -jnp.inf
    dmats = []
    for h in range(H_RET):
        df = jnp.exp(jnp.where(lower, dif * lgw[h:h + 1, :], neg_inf))
        db = jnp.exp(jnp.where(upper, (-dif) * lgw[H_RET + h:H_RET + h + 1, :], neg_inf))
        dmats.append(df + db)

    lane = lax.broadcasted_iota(jnp.int32, (C, HEAD_PAIR), 1)
    low = lane < DK_RET
    r128 = lax.broadcasted_iota(jnp.int32, (HEAD_PAIR, HEAD_PAIR), 0) >= DK_RET
    c128 = lax.broadcasted_iota(jnp.int32, (HEAD_PAIR, HEAD_PAIR), 1) >= DK_RET
    same_head = r128 == c128
    avg = jnp.where(same_head, 1.0 / DK_RET, 0.0).astype(BF16)

    ri = lax.broadcasted_iota(jnp.int32, (C, W_RET), 0).astype(F32)
    need_states = emit_state or (has_s0 and nch > 1)
    if need_states:
        w_f = jnp.exp((C - 1.0 - ri) * lgl[0:1, :])
        w_b = jnp.exp(ri * lgl[1:2, :])
    if has_s0:
        e_f = jnp.exp((ri + 1.0) * lgl[0:1, :])
        e_b = jnp.exp((C - ri) * lgl[1:2, :])
        dec_f = jnp.exp(float(C) * lgl[0:1, :])
        dec_b = jnp.exp(float(C) * lgl[1:2, :])

    if rope:
        swap_low = (lax.broadcasted_iota(jnp.int32, (seq_len, HEAD_PAIR), 1) % 32) < 16
        cos = cos_ref[...]
        sin = sin_ref[...]
        for blk in range(2 * npair):
            cols = slice(blk * HEAD_PAIR, (blk + 1) * HEAD_PAIR)
            x = p_ref[:, cols]
            swapped = jnp.where(swap_low, pltpu.roll(x, HEAD_PAIR - 16, 1), pltpu.roll(x, 16, 1))
            qk_ref[:, cols] = x * cos + swapped * sin

    def rows(c):
        return slice(c * C, (c + 1) * C)

    def cols(p):
        return slice(p * HEAD_PAIR, (p + 1) * HEAD_PAIR)

    def get_q(c, p):
        src = qk_ref if rope else p_ref
        return src[rows(c), p * HEAD_PAIR:(p + 1) * HEAD_PAIR]

    def get_k(c, p):
        src = qk_ref if rope else p_ref
        return src[rows(c), W_RET + p * HEAD_PAIR:W_RET + (p + 1) * HEAD_PAIR]

    def get_v(c, p):
        return p_ref[rows(c), 2 * W_RET + p * HEAD_PAIR:2 * W_RET + (p + 1) * HEAD_PAIR]

    ds_f = [[None] * npair for _ in range(nch)]
    ds_b = [[None] * npair for _ in range(nch)]
    if need_states:
        for c in range(nch):
            for p in range(npair):
                k = get_k(c, p)
                vb = get_v(c, p).astype(BF16)
                kf = jnp.transpose(k * w_f[:, cols(p)]).astype(BF16)
                kb = jnp.transpose(k * w_b[:, cols(p)]).astype(BF16)
                ds_f[c][p] = jnp.where(same_head, _dot(kf, vb), 0.0)
                ds_b[c][p] = jnp.where(same_head, _dot(kb, vb), 0.0)

    if emit_state:
        for p in range(npair):
            st_ref[0, 0, p] = ds_f[0][p]
            st_ref[0, 1, p] = ds_b[0][p]

    sf_in = [[None] * npair for _ in range(nch)]
    sb_in = [[None] * npair for _ in range(nch)]
    if has_s0:
        for p in range(npair):
            s = s0_ref[0, 0, p]
            for c in range(nch):
                sf_in[c][p] = s
                if c + 1 < nch:
                    s = s * dec_f[:, cols(p)] + ds_f[c][p]
            s = s0_ref[0, 1, p]
            for c in reversed(range(nch)):
                sb_in[c][p] = s
                if c > 0:
                    s = s * dec_b[:, cols(p)] + ds_b[c][p]

    for c in range(nch):
        for p in range(npair):
            q = get_q(c, p)
            kb = get_k(c, p).astype(BF16)
            vb = get_v(c, p).astype(BF16)
            q0 = jnp.where(low, q, 0.0).astype(BF16)
            q1 = jnp.where(low, 0.0, q).astype(BF16)
            o0 = _dot((_dot_nt(q0, kb) * dmats[2 * p]).astype(BF16), vb)
            o1 = _dot((_dot_nt(q1, kb) * dmats[2 * p + 1]).astype(BF16), vb)
            o = jnp.where(low, o0, o1)
            if has_s0:
                o = o + _dot((q * e_f[:, cols(p)]).astype(BF16), sf_in[c][p].astype(BF16))
                o = o + _dot((q * e_b[:, cols(p)]).astype(BF16), sb_in[c][p].astype(BF16))
            mu = _dot(o.astype(BF16), avg)
            d = o - mu
            var = _dot((d * d).astype(BF16), avg)
            on = d * lax.rsqrt(var + EPS)
            g = p_ref[rows(c), 3 * W_RET + p * HEAD_PAIR:3 * W_RET + (p + 1) * HEAD_PAIR]
            y_ref[rows(c), cols(p)] = (on * gn_ref[0:1, cols(p)] * _silu(g)).astype(BF16)


def _retention(proj_r, dec_lane, dec_wide, gn, seq_len, rope_tabs=None, s0=None):
    n = proj_r.shape[0]
    nseq = n // seq_len
    rope = rope_tabs is not None
    has_s0 = s0 is not None
    emit_state = not has_s0
    npair = H_RET // 2
    in_specs = [pl.BlockSpec((seq_len, W_GRP_R), lambda i: (i, 0)),
                _const_spec(dec_lane.shape), _const_spec(dec_wide.shape), _const_spec(gn.shape)]
    args = [proj_r, dec_lane, dec_wide, gn]
    if rope:
        in_specs += [_const_spec(rope_tabs[0].shape), _const_spec(rope_tabs[1].shape)]
        args += list(rope_tabs)
    if has_s0:
        in_specs.append(pl.BlockSpec((1, 2, npair, HEAD_PAIR, HEAD_PAIR), lambda i: (i, 0, 0, 0, 0)))
        args.append(s0)
    out_specs = [pl.BlockSpec((seq_len, W_RET), lambda i: (i, 0))]
    out_shape = [jax.ShapeDtypeStruct((n, W_RET), BF16)]
    if emit_state:
        out_specs.append(pl.BlockSpec((1, 2, npair, HEAD_PAIR, HEAD_PAIR), lambda i: (i, 0, 0, 0, 0)))
        out_shape.append(jax.ShapeDtypeStruct((nseq, 2, npair, HEAD_PAIR, HEAD_PAIR), F32))
    scratch = [pltpu.VMEM((seq_len, 2 * W_RET), F32)] if rope else []
    return pl.pallas_call(
        functools.partial(_ret_body, seq_len=seq_len, rope=rope, has_s0=has_s0, emit_state=emit_state),
        grid=(nseq,), in_specs=in_specs, out_specs=out_specs, out_shape=out_shape,
        scratch_shapes=scratch, compiler_params=_params(1),
        name="retention_lat" if rope else "retention_ctx",
    )(*args)


def _ssd_body(*refs, seq_len, has_s0, emit_state):
    it = iter(refs)
    p_ref = next(it)
    cw_ref = next(it)
    cb_ref = next(it)
    dtb_ref = next(it)
    alog_ref = next(it)
    dskip_ref = next(it)
    gn_ref = next(it)
    s0_ref = st_ref = None
    if has_s0:
        s0_ref = next(it)
    y_ref = next(it)
    if emit_state:
        st_ref = next(it)
    xs_ref = next(it)
    bc_ref = next(it)

    T = seq_len
    C = SCAN_CHUNK
    nch = T // C
    pos = lax.broadcasted_iota(jnp.int32, (T, LANES), 0)

    for blk in range(CONV_CH // LANES):
        x = p_ref[:, W_SSD + blk * LANES:W_SSD + (blk + 1) * LANES]
        a = _silu(_dwconv4(x, cw_ref, cb_ref, blk * LANES, pos, T))
        if blk < W_SSD // LANES:
            xs_ref[:, blk * LANES:(blk + 1) * LANES] = a
        else:
            o = blk * LANES - W_SSD
            bc_ref[:, o:o + LANES] = a

    a_neg = -jnp.exp(alog_ref[...])
    ii = lax.broadcasted_iota(jnp.int32, (C, C), 0)
    mm = lax.broadcasted_iota(jnp.int32, (C, C), 1)
    lower = ii >= mm
    upper = mm >= ii
    tri_l = jnp.where(lower, 1.0, 0.0).astype(BF16)
    tri_u = jnp.where(upper, 1.0, 0.0).astype(BF16)
    neg_inf = -jnp.inf
    lane384 = lax.broadcasted_iota(jnp.int32, (C, W_SSD), 1)
    grp0 = lane384 < (W_SSD // G_SSD)
    lane_s = lax.broadcasted_iota(jnp.int32, (N_SSD, W_SSD), 1)
    grp0_s = lane_s < (W_SSD // G_SSD)
    low = lax.broadcasted_iota(jnp.int32, (C, HEAD_PAIR), 1) < P_SSD
    need_states = emit_state or (has_s0 and nch > 1)

    def rows(c):
        return slice(c * C, (c + 1) * C)

    def chunk_scalars(c):
        dt = _softplus(p_ref[rows(c), W_SSD + CONV_CH:W_SSD + CONV_CH + LANES] + dtb_ref[...])
        la = dt * a_neg
        a1 = la.astype(BF16)
        r1 = la - a1.astype(F32)
        a2 = r1.astype(BF16)
        a3 = (r1 - a2.astype(F32)).astype(BF16)
        pre = _dot(tri_l, a1) + _dot(tri_l, a2) + _dot(tri_l, a3)
        suf = _dot(tri_u, a1) + _dot(tri_u, a2) + _dot(tri_u, a3)
        return dt, pre, suf

    def group_select(x0, x1, mask):
        return jnp.where(mask, x0, x1)

    ds_f = [None] * nch
    ds_b = [None] * nch
    dec_f = [None] * nch
    dec_b = [None] * nch
    if need_states:
        for c in range(nch):
            dt, pre, suf = chunk_scalars(c)
            wf = jnp.exp(pre[C - 1:C, :] - pre) * dt
            wb = jnp.exp(suf[0:1, :] - suf) * dt
            xs = xs_ref[rows(c), :]
            xf = (xs * _head_lane_expand(wf, 0)).astype(BF16)
            xb = (xs * _head_lane_expand(wb, DT_BWD_LANE)).astype(BF16)
            bt0 = jnp.transpose(bc_ref[rows(c), 0:N_SSD]).astype(BF16)
            bt1 = jnp.transpose(bc_ref[rows(c), N_SSD:2 * N_SSD]).astype(BF16)
            ds_f[c] = group_select(_dot(bt0, xf), _dot(bt1, xf), grp0_s)
            ds_b[c] = group_select(_dot(bt0, xb), _dot(bt1, xb), grp0_s)
            dec_f[c] = _head_lane_expand(jnp.exp(pre[C - 1:C, :]), 0)
            dec_b[c] = _head_lane_expand(jnp.exp(suf[0:1, :]), DT_BWD_LANE)

    if emit_state:
        st_ref[0, 0] = ds_f[0]
        st_ref[0, 1] = ds_b[0]

    sf_in = [None] * nch
    sb_in = [None] * nch
    if has_s0:
        s = s0_ref[0, 0]
        for c in range(nch):
            sf_in[c] = s
            if c + 1 < nch:
                s = s * dec_f[c] + ds_f[c]
        s = s0_ref[0, 1]
        for c in reversed(range(nch)):
            sb_in[c] = s
            if c > 0:
                s = s * dec_b[c] + ds_b[c]

    for c in range(nch):
        dt, pre, suf = chunk_scalars(c)
        pre_t = jnp.transpose(pre)
        suf_t = jnp.transpose(suf)
        dt_t = jnp.transpose(dt)
        xs = xs_ref[rows(c), :]
        xs_b = xs.astype(BF16)
        cm = [bc_ref[rows(c), 2 * N_SSD + g * N_SSD:2 * N_SSD + (g + 1) * N_SSD].astype(BF16)
              for g in range(G_SSD)]
        bm = [bc_ref[rows(c), g * N_SSD:(g + 1) * N_SSD].astype(BF16) for g in range(G_SSD)]
        gram = [_dot_nt(cm[g], bm[g]) for g in range(G_SSD)]
        heads = []
        for h in range(H_SSD):
            g = h // (H_SSD // G_SSD)
            hb = DT_BWD_LANE + h
            df = jnp.exp(jnp.where(lower, pre[:, h:h + 1] - pre_t[h:h + 1, :], neg_inf))
            db = jnp.exp(jnp.where(upper, suf[:, hb:hb + 1] - suf_t[hb:hb + 1, :], neg_inf))
            w = gram[g] * (df * dt_t[h:h + 1, :] + db * dt_t[hb:hb + 1, :])
            p = h // 2
            heads.append(_dot(w.astype(BF16), xs_b[:, p * HEAD_PAIR:(p + 1) * HEAD_PAIR]))
        o = jnp.concatenate([jnp.where(low, heads[2 * p], heads[2 * p + 1]) for p in range(H_SSD // 2)],
                            axis=1)
        if has_s0:
            sf = sf_in[c].astype(BF16)
            sb = sb_in[c].astype(BF16)
            o = o + _head_lane_expand(jnp.exp(pre), 0) * group_select(_dot(cm[0], sf), _dot(cm[1], sf), grp0)
            o = o + (_head_lane_expand(jnp.exp(suf), DT_BWD_LANE)
                     * group_select(_dot(cm[0], sb), _dot(cm[1], sb), grp0))
        y = o + dskip_ref[...] * xs
        yz = y * _silu(p_ref[rows(c), 0:W_SSD])
        y_ref[rows(c), :] = (_rms_scale(yz) * gn_ref[...]).astype(BF16)


def _ssd(proj_s, cw, cb, dtb, alog, dskip, gn, seq_len, s0=None):
    n = proj_s.shape[0]
    nseq = n // seq_len
    has_s0 = s0 is not None
    emit_state = not has_s0
    in_specs = [pl.BlockSpec((seq_len, W_GRP_S), lambda i: (i, 0))]
    in_specs += [_const_spec(a.shape) for a in (cw, cb, dtb, alog, dskip, gn)]
    args = [proj_s, cw, cb, dtb, alog, dskip, gn]
    if has_s0:
        in_specs.append(pl.BlockSpec((1, 2, N_SSD, W_SSD), lambda i: (i, 0, 0, 0)))
        args.append(s0)
    out_specs = [pl.BlockSpec((seq_len, W_SSD), lambda i: (i, 0))]
    out_shape = [jax.ShapeDtypeStruct((n, W_SSD), BF16)]
    if emit_state:
        out_specs.append(pl.BlockSpec((1, 2, N_SSD, W_SSD), lambda i: (i, 0, 0, 0)))
        out_shape.append(jax.ShapeDtypeStruct((nseq, 2, N_SSD, W_SSD), F32))
    scratch = [pltpu.VMEM((seq_len, W_SSD), F32), pltpu.VMEM((seq_len, 2 * G_SSD * N_SSD), F32)]
    return pl.pallas_call(
        functools.partial(_ssd_body, seq_len=seq_len, has_s0=has_s0, emit_state=emit_state),
        grid=(nseq,), in_specs=in_specs, out_specs=out_specs, out_shape=out_shape,
        scratch_shapes=scratch, compiler_params=_params(1),
        name="ssd_lat" if has_s0 else "ssd_ctx",
    )(*args)


def _lru_body(*refs, seq_len, has_h0, emit_state):
    it = iter(refs)
    p_ref = next(it)
    cw_ref = next(it)
    cb_ref = next(it)
    wg_ref = next(it)
    bg_ref = next(it)
    lam_ref = next(it)
    h0_ref = st_ref = None
    if has_h0:
        h0_ref = next(it)
    y_ref = next(it)
    if emit_state:
        st_ref = next(it)

    T = seq_len
    pos = lax.broadcasted_iota(jnp.int32, (T, W_LRU), 0)
    xc = _dwconv4(p_ref[:, 0:W_LRU], cw_ref, cb_ref, 0, pos, T)
    gates = _sigmoid(_dot(xc.astype(BF16), wg_ref[...]) + bg_ref[...])
    decay_rate = _softplus(-lam_ref[...])
    total = None
    for d in range(2):
        r = gates[:, 2 * d * W_LRU:(2 * d + 1) * W_LRU]
        i = gates[:, (2 * d + 1) * W_LRU:(2 * d + 2) * W_LRU]
        log_a = (-LRU_C * r) * decay_rate[d:d + 1, :]
        a = jnp.exp(log_a)
        th = jnp.tanh(log_a)
        b = jnp.sqrt(-2.0 * th / (1.0 - th)) * (i * xc)
        step = 1
        while step < T:
            if d == 0:
                ok = pos >= step
                shift = step
            else:
                ok = pos <= T - 1 - step
                shift = T - step
            a_prev = jnp.where(ok, pltpu.roll(a, shift, 0), 1.0)
            b_prev = jnp.where(ok, pltpu.roll(b, shift, 0), 0.0)
            b = a * b_prev + b
            a = a * a_prev
            step *= 2
        hs = b
        if has_h0:
            hs = hs + a * h0_ref[0, d:d + 1, :]
        if emit_state:
            last = T - 1 if d == 0 else 0
            st_ref[0, d:d + 1, :] = hs[last:last + 1, :]
        total = hs if total is None else total + hs
    y_ref[...] = (total * _gelu_tanh(p_ref[:, W_LRU:2 * W_LRU])).astype(BF16)


def _lru(proj_l, cw, cb, wg, bg, lam, seq_len, h0=None):
    n = proj_l.shape[0]
    nseq = n // seq_len
    has_h0 = h0 is not None
    emit_state = not has_h0
    in_specs = [pl.BlockSpec((seq_len, W_GRP_L), lambda i: (i, 0))]
    in_specs += [_const_spec(a.shape) for a in (cw, cb, wg, bg, lam)]
    args = [proj_l, cw, cb, wg, bg, lam]
    if has_h0:
        in_specs.append(pl.BlockSpec((1, 2, W_LRU), lambda i: (i, 0, 0)))
        args.append(h0)
    out_specs = [pl.BlockSpec((seq_len, W_LRU), lambda i: (i, 0))]
    out_shape = [jax.ShapeDtypeStruct((n, W_LRU), BF16)]
    if emit_state:
        out_specs.append(pl.BlockSpec((1, 2, W_LRU), lambda i: (i, 0, 0)))
        out_shape.append(jax.ShapeDtypeStruct((nseq, 2, W_LRU), F32))
    return pl.pallas_call(
        functools.partial(_lru_body, seq_len=seq_len, has_h0=has_h0, emit_state=emit_state),
        grid=(nseq,), in_specs=in_specs, out_specs=out_specs, out_shape=out_shape,
        compiler_params=_params(1),
        name="lru_lat" if has_h0 else "lru_ctx",
    )(*args)


def _outproj_body(x_ref, yr_ref, ys_ref, yl_ref, mod_ref, g_ref, wo_ref, x1_ref, h2_ref):
    mix = _dot(yr_ref[...], wo_ref[0:W_RET, :])
    mix = mix + _dot(ys_ref[...], wo_ref[W_RET:W_RET + W_SSD, :])
    mix = mix + _dot(yl_ref[...], wo_ref[W_RET + W_SSD:D_MODEL, :])
    x1 = x_ref[...] + mod_ref[0, 2:3, :] * mix
    x1_ref[...] = x1
    y = _rms_scale(x1) * g_ref[...]
    h2_ref[...] = (y * (1.0 + mod_ref[0, 4:5, :]) + mod_ref[0, 3:4, :]).astype(BF16)


def _out_proj(x2d, y_ret, y_ssd, y_lru, mod, gain, wo):
    n = x2d.shape[0]
    rows_per_mod = n // mod.shape[0]
    tm = ROW_TILE
    row_spec = lambda w: pl.BlockSpec((tm, w), lambda i: (i, 0))
    return pl.pallas_call(
        _outproj_body,
        grid=(n // tm,),
        in_specs=[row_spec(D_MODEL), row_spec(W_RET), row_spec(W_SSD), row_spec(W_LRU),
                  pl.BlockSpec((1, 6, D_MODEL), lambda i: (i * tm // rows_per_mod, 0, 0)),
                  _const_spec((1, D_MODEL)), _const_spec(wo.shape)],
        out_specs=[row_spec(D_MODEL), row_spec(D_MODEL)],
        out_shape=[jax.ShapeDtypeStruct((n, D_MODEL), F32), jax.ShapeDtypeStruct((n, D_MODEL), BF16)],
        compiler_params=_params(1),
        name="out_proj",
    )(x2d, y_ret, y_ssd, y_lru, mod, gain, wo)


def _ffn_body(h_ref, x1_ref, mod_ref, wv_ref, wg_ref, cwv_ref, cwg_ref, cbv_ref, cbg_ref, wd_ref, fg_ref,
              o_ref, acc_ref, *, seq_len, final_norm):
    tm = h_ref.shape[0]
    assert seq_len & (seq_len - 1) == 0
    pos = lax.broadcasted_iota(jnp.int32, (tm, FF_BLOCK), 0) & (seq_len - 1)

    def conv3(u, cw, cb):
        acc = u * cw[1:2, :] + cb
        acc = acc + _shift_rows(u, -1, pos, seq_len) * cw[0:1, :]
        acc = acc + _shift_rows(u, 1, pos, seq_len) * cw[2:3, :]
        return acc

    acc_ref[...] = jnp.zeros_like(acc_ref)

    def step(j, carry):
        h = h_ref[...]
        val = conv3(_dot(h, wv_ref[j]), cwv_ref[j], cbv_ref[j])
        gate = conv3(_dot(h, wg_ref[j]), cwg_ref[j], cbg_ref[j])
        act = (_silu(gate) * val).astype(BF16)
        acc_ref[...] += _dot(act, wd_ref[j])
        return carry

    lax.fori_loop(0, N_FF_BLOCKS, step, 0)
    out = x1_ref[...] + mod_ref[0, 5:6, :] * acc_ref[...]
    if final_norm:
        out = _rms_scale(out) * fg_ref[...]
    o_ref[...] = out


def _ffn(h2, x1, mod, wv, wg, cwv, cwg, cbv, cbg, wd, fgain, seq_len, final_norm):
    n = h2.shape[0]
    rows_per_mod = n // mod.shape[0]
    tm = FFN_ROW_TILE
    row_spec = pl.BlockSpec((tm, D_MODEL), lambda i: (i, 0))
    consts = (wv, wg, cwv, cwg, cbv, cbg, wd, fgain)
    return pl.pallas_call(
        functools.partial(_ffn_body, seq_len=seq_len, final_norm=final_norm),
        grid=(n // tm,),
        in_specs=[row_spec, row_spec,
                  pl.BlockSpec((1, 6, D_MODEL), lambda i: (i * tm // rows_per_mod, 0, 0))]
                 + [_const_spec(a.shape) for a in consts],
        out_specs=row_spec,
        out_shape=jax.ShapeDtypeStruct((n, D_MODEL), F32),
        scratch_shapes=[pltpu.VMEM((tm, D_MODEL), F32)],
        compiler_params=_params(1),
        name="ffn",
    )(h2, x1, mod, *consts)


def _rope_tables(t_len):
    tok = jnp.arange(t_len)
    row_pos = (tok // GRID_W).astype(F32)
    col_pos = (tok % GRID_W).astype(F32)
    lane = jnp.arange(HEAD_PAIR)
    d = lane % DK_RET
    use_row = d < (DK_RET // 2)
    half = DK_RET // 4
    freqs = ROPE_BASE ** (-(d % half).astype(F32) / half)
    ang = jnp.where(use_row[None, :], row_pos[:, None], col_pos[:, None]) * freqs[None, :]
    first = (d % (2 * half)) < half
    return jnp.cos(ang), jnp.where(first[None, :], -jnp.sin(ang), jnp.sin(ang))


def _prep_layer(l, w_in, ret_decay, ret_norm_g, ssd_conv_w, ssd_conv_b, ssd_dt_bias, ssd_a_log, ssd_d,
                ssd_norm_g, lru_conv_w, lru_conv_b, lru_w_a, lru_b_a, lru_w_x, lru_b_x, lru_lambda,
                w_out, ffn_w_up, ffn_conv_w, ffn_conv_b, ffn_w_down):
    w = w_in[l]
    o_q, o_z, o_xbc, o_dt, o_xl = 0, 4 * W_RET, 4 * W_RET + W_SSD, 4 * W_RET + W_SSD + CONV_CH, \
        4 * W_RET + W_SSD + CONV_CH + H_SSD
    wr = jnp.concatenate([w[:, o_q:o_q + W_RET] * (DK_RET ** -0.5), w[:, W_RET:o_z]], axis=1).astype(BF16)
    dt_cols = w[:, o_dt:o_dt + H_SSD]
    dt_blk = jnp.zeros((D_MODEL, LANES), F32)
    dt_blk = dt_blk.at[:, 0:H_SSD].set(dt_cols).at[:, DT_BWD_LANE:DT_BWD_LANE + H_SSD].set(dt_cols)
    ws = jnp.concatenate([w[:, o_z:o_dt], dt_blk], axis=1).astype(BF16)
    wl = w[:, o_xl:o_xl + 2 * W_LRU].astype(BF16)

    def lane_place(pair):
        v = jnp.zeros((1, LANES), F32)
        return v.at[0, 0:H_SSD].set(pair[0]).at[0, DT_BWD_LANE:DT_BWD_LANE + H_SSD].set(pair[1])

    def block_diag(blocks):
        out = jnp.zeros((W_LRU, W_LRU), F32)
        for b in range(LRU_BLOCKS):
            out = out.at[b * LRU_BW:(b + 1) * LRU_BW, b * LRU_BW:(b + 1) * LRU_BW].set(blocks[b])
        return out

    wgate = jnp.concatenate([block_diag(lru_w_a[l, 0]), block_diag(lru_w_x[l, 0]),
                             block_diag(lru_w_a[l, 1]), block_diag(lru_w_x[l, 1])], axis=1).astype(BF16)
    bgate = jnp.concatenate([lru_b_a[l, 0], lru_b_x[l, 0], lru_b_a[l, 1], lru_b_x[l, 1]])[None, :]
    up = ffn_w_up[l]
    blocked_cols = lambda a: a.reshape(a.shape[0], N_FF_BLOCKS, FF_BLOCK).transpose(1, 0, 2)
    return dict(
        wr=wr, ws=ws, wl=wl,
        dec_lane=jnp.repeat(ret_decay[l], DK_RET, axis=1),
        dec_wide=jnp.broadcast_to(ret_decay[l].reshape(2 * H_RET, 1), (2 * H_RET, SCAN_CHUNK)),
        ret_gn=ret_norm_g[l][None, :],
        ssd_cw=ssd_conv_w[l], ssd_cb=ssd_conv_b[l][None, :],
        ssd_dtb=lane_place(ssd_dt_bias[l]), ssd_alog=lane_place(ssd_a_log[l]),
        ssd_dskip=jnp.repeat(ssd_d[l], P_SSD)[None, :], ssd_gn=ssd_norm_g[l][None, :],
        lru_cw=lru_conv_w[l], lru_cb=lru_conv_b[l][None, :], lru_wg=wgate, lru_bg=bgate,
        lru_lam=lru_lambda[l],
        wo=w_out[l].astype(BF16),
        wv=blocked_cols(up[:, :D_FF]).astype(BF16), wg=blocked_cols(up[:, D_FF:]).astype(BF16),
        cwv=blocked_cols(ffn_conv_w[l][:, :D_FF]), cwg=blocked_cols(ffn_conv_w[l][:, D_FF:]),
        cbv=blocked_cols(ffn_conv_b[l][None, :D_FF]), cbg=blocked_cols(ffn_conv_b[l][None, D_FF:]),
        wd=ffn_w_down[l].reshape(N_FF_BLOCKS, FF_BLOCK, D_MODEL).astype(BF16),
    )


def _trunk_layer(x2d, mod, seq_len, lp, norm1, norm2, fgain, final_norm, rope_tabs=None, states=None):
    pr, ps, plru = _in_proj(x2d, mod, norm1, lp["wr"], lp["ws"], lp["wl"])
    s_ret = s_ssd = s_lru = None
    if states is not None:
        s_ret, s_ssd, s_lru = states
    ret = _retention(pr, lp["dec_lane"], lp["dec_wide"], lp["ret_gn"], seq_len, rope_tabs, s_ret)
    ssd = _ssd(ps, lp["ssd_cw"], lp["ssd_cb"], lp["ssd_dtb"], lp["ssd_alog"], lp["ssd_dskip"],
               lp["ssd_gn"], seq_len, s_ssd)
    lru = _lru(plru, lp["lru_cw"], lp["lru_cb"], lp["lru_wg"], lp["lru_bg"], lp["lru_lam"], seq_len, s_lru)
    x1, h2 = _out_proj(x2d, ret[0], ssd[0], lru[0], mod, norm2, lp["wo"])
    out = _ffn(h2, x1, mod, lp["wv"], lp["wg"], lp["cwv"], lp["cwg"], lp["cbv"], lp["cbg"], lp["wd"],
               fgain, seq_len, final_norm)
    new_states = None
    if states is None:
        new_states = (ret[1], ssd[1], lru[1])
    return out, new_states


def kernel(x_prompt, x_sample, c, c_ctx, state_ret, state_ssd, state_lru, w_ada, b_ada, norm1_g, norm2_g, w_in, ret_decay, ret_norm_g, ssd_conv_w, ssd_conv_b, ssd_dt_bias, ssd_a_log, ssd_d, ssd_norm_g, lru_conv_w, lru_conv_b, lru_w_a, lru_b_a, lru_w_x, lru_b_x, lru_lambda, w_out, ffn_w_up, ffn_conv_w, ffn_conv_b, ffn_w_down, final_norm_g):
    bp, tp, _ = x_prompt.shape
    bs, ts, _ = x_sample.shape
    assert tp == SCAN_CHUNK and ts % SCAN_CHUNK == 0 and ts % GRID_W == 0

    pad_rows = 8 - (1 + bs)
    cvec = jnp.concatenate([c_ctx[None, :], c, jnp.zeros((pad_rows, D_MODEL), F32)], axis=0)
    mods = _modulation(cvec, w_ada, b_ada).reshape(DEPTH, 8, 6, D_MODEL)

    rope_tabs = _rope_tables(ts)
    fgain = final_norm_g[None, :]
    xp = x_prompt.reshape(bp * tp, D_MODEL)
    xs = x_sample.reshape(bs * ts, D_MODEL)
    npair = H_RET // 2
    new_ret, new_ssd, new_lru = [], [], []
    for l in range(DEPTH):
        lp = _prep_layer(l, w_in, ret_decay, ret_norm_g, ssd_conv_w, ssd_conv_b, ssd_dt_bias, ssd_a_log,
                         ssd_d, ssd_norm_g, lru_conv_w, lru_conv_b, lru_w_a, lru_b_a, lru_w_x, lru_b_x,
                         lru_lambda, w_out, ffn_w_up, ffn_conv_w, ffn_conv_b, ffn_w_down)
        final = l == DEPTH - 1
        n1, n2 = norm1_g[l][None, :], norm2_g[l][None, :]
        xp, (sr, ss, sl) = _trunk_layer(xp, mods[l, 0:1], tp, lp, n1, n2, fgain, final)
        sr = sr.reshape(bp, 2, npair, 2, DK_RET, 2, DK_RET)
        sr = jnp.stack([sr[:, :, :, 0, :, 0, :], sr[:, :, :, 1, :, 1, :]], axis=3)
        new_ret.append(sr.reshape(bp, 2, H_RET, DK_RET, DK_RET))
        new_ssd.append(ss.reshape(bp, 2, N_SSD, H_SSD, P_SSD).transpose(0, 1, 3, 2, 4))
        new_lru.append(sl)

        s0r = state_ret[:, l].reshape(bs, 2, npair, 2, DK_RET, DK_RET)
        zero = jnp.zeros_like(s0r[:, :, :, 0])
        s0r = jnp.concatenate([jnp.concatenate([s0r[:, :, :, 0], zero], axis=-1),
                               jnp.concatenate([zero, s0r[:, :, :, 1]], axis=-1)], axis=-2)
        s0s = state_ssd[:, l].transpose(0, 1, 3, 2, 4).reshape(bs, 2, N_SSD, W_SSD)
        xs, _ = _trunk_layer(xs, mods[l, 1:1 + bs], ts, lp, n1, n2, fgain, final, rope_tabs,
                             (s0r, s0s, state_lru[:, l]))

    y_prompt = xp.reshape(bp, tp, D_MODEL)
    y_sample = xs.reshape(bs, ts, D_MODEL)
    return (y_prompt, y_sample, jnp.stack(new_ret, axis=1), jnp.stack(new_ssd, axis=1),
            jnp.stack(new_lru, axis=1))
```

```python
import functools
import math

import jax
import jax.numpy as jnp
from jax import lax
from jax.experimental import pallas as pl
from jax.experimental.pallas import tpu as pltpu

F32 = jnp.float32
BF16 = jnp.bfloat16

D_MODEL = 1024
DEPTH = 2
GRID_W = 64
W_RET = 384
H_RET = 6
DK_RET = 64
W_SSD = 384
P_SSD = 64
H_SSD = 6
N_SSD = 128
G_SSD = 2
CONV_CH = W_SSD + 2 * G_SSD * N_SSD
W_LRU = 256
LRU_BLOCKS = 4
LRU_BW = 64
LRU_C = 8.0
D_FF = 2816
ROPE_BASE = 10000.0
EPS = 1e-6

LANES = 128
SUBLANES = 8
HEAD_PAIR = LANES
SCAN_CHUNK = 256
FF_BLOCK = 256
N_FF_BLOCKS = D_FF // FF_BLOCK
ROW_TILE = 512
FFN_ROW_TILE = 1024
PROJ_COL_BLOCK = 512
VMEM_LIMIT = 56 * 1024 * 1024

W_GRP_R = 4 * W_RET
W_GRP_S = W_SSD + CONV_CH + LANES
W_GRP_L = 2 * W_LRU
W_IN_PAD = W_GRP_R + W_GRP_S + W_GRP_L
DT_BWD_LANE = 8


def _dot(a, b):
    return jnp.dot(a, b, preferred_element_type=F32)


def _dot_nt(a, b):
    return lax.dot_general(a, b, (((1,), (1,)), ((), ())), preferred_element_type=F32)


def _sigmoid(x):
    return 1.0 / (1.0 + jnp.exp(-x))


def _silu(x):
    return x * _sigmoid(x)


def _softplus(x):
    return jnp.maximum(x, 0.0) + jnp.log1p(jnp.exp(-jnp.abs(x)))


def _log_sigmoid(x):
    return -_softplus(-x)


def _gelu_tanh(x):
    c = math.sqrt(2.0 / math.pi)
    return 0.5 * x * (1.0 + jnp.tanh(c * (x + 0.044715 * (x * x * x))))


def _rms_scale(x):
    return x * lax.rsqrt(jnp.mean(x * x, axis=-1, keepdims=True) + EPS)


def _const_spec(shape):
    zeros = (0,) * len(shape)
    return pl.BlockSpec(shape, lambda *_: zeros, pipeline_mode=pl.Buffered(1))


def _layer_spec(arr, l):
    rest = (0,) * (arr.ndim - 1)
    return pl.BlockSpec((1,) + arr.shape[1:], lambda *_: (l,) + rest, pipeline_mode=pl.Buffered(1))


def _mod_spec(row0, rows_per_mod, tm, l):
    return pl.BlockSpec((1, 1, 6, D_MODEL), lambda i: (l, row0 + i * tm // rows_per_mod, 0, 0))


def _params(n_axes):
    return pltpu.CompilerParams(dimension_semantics=("arbitrary",) * n_axes,
                                vmem_limit_bytes=VMEM_LIMIT)


def _mod_body(c_ref, w_ref, b_ref, o_ref):
    s = _silu(c_ref[...]).astype(BF16)
    o_ref[0] = _dot(s, w_ref[0].astype(BF16)) + b_ref[0]


def _modulation(cvec, w_ada, b_ada):
    rows = cvec.shape[0]
    nblk = (6 * D_MODEL) // D_MODEL
    return pl.pallas_call(
        _mod_body,
        grid=(DEPTH, nblk),
        in_specs=[pl.BlockSpec((rows, D_MODEL), lambda l, j: (0, 0)),
                  pl.BlockSpec((1, D_MODEL, D_MODEL), lambda l, j: (l, 0, j)),
                  pl.BlockSpec((1, 1, D_MODEL), lambda l, j: (l, 0, j))],
        out_specs=pl.BlockSpec((1, rows, D_MODEL), lambda l, j: (l, 0, j)),
        out_shape=jax.ShapeDtypeStruct((DEPTH, rows, 6 * D_MODEL), F32),
        compiler_params=_params(2),
        name="adaln_mod",
    )(cvec, w_ada, b_ada.reshape(DEPTH, 1, 6 * D_MODEL))


def _inproj_body(x_ref, mod_ref, g_ref, w_ref, or_ref, os_ref, ol_ref):
    y = _rms_scale(x_ref[...]) * g_ref[0]
    h = (y * (1.0 + mod_ref[0, 0, 1:2, :]) + mod_ref[0, 0, 0:1, :]).astype(BF16)
    col0 = 0
    for o_ref in (or_ref, os_ref, ol_ref):
        n = o_ref.shape[1]
        for j in range(0, n, PROJ_COL_BLOCK):
            jb = min(PROJ_COL_BLOCK, n - j)
            o_ref[:, j:j + jb] = _dot(h, w_ref[0, :, col0 + j:col0 + j + jb])
        col0 += n


def _in_proj(x2d, mods, row0, rows_per_mod, gain, w_all, l):
    n = x2d.shape[0]
    tm = ROW_TILE
    return pl.pallas_call(
        _inproj_body,
        grid=(n // tm,),
        in_specs=[pl.BlockSpec((tm, D_MODEL), lambda i: (i, 0)),
                  _mod_spec(row0, rows_per_mod, tm, l),
                  _layer_spec(gain, l), _layer_spec(w_all, l)],
        out_specs=[pl.BlockSpec((tm, W_GRP_R), lambda i: (i, 0)),
                   pl.BlockSpec((tm, W_GRP_S), lambda i: (i, 0)),
                   pl.BlockSpec((tm, W_GRP_L), lambda i: (i, 0))],
        out_shape=[jax.ShapeDtypeStruct((n, W_GRP_R), F32),
                   jax.ShapeDtypeStruct((n, W_GRP_S), F32),
                   jax.ShapeDtypeStruct((n, W_GRP_L), F32)],
        compiler_params=_params(1),
        name="in_proj",
    )(x2d, mods, gain, w_all)


def _shift_rows(x, k, pos, seq_len):
    n = x.shape[0]
    if k == 0:
        return x
    rolled = pltpu.roll(x, (-k) % n, 0)
    ok = (pos >= -k) if k < 0 else (pos <= seq_len - 1 - k)
    return jnp.where(ok, rolled, 0.0)


def _dwconv4(x, w_ref, b_ref, col0, pos, seq_len):
    width = x.shape[1]
    cols = slice(col0, col0 + width)
    acc = x * w_ref[0, 1:2, cols] + b_ref[0, 0:1, cols]
    acc = acc + _shift_rows(x, -1, pos, seq_len) * w_ref[0, 0:1, cols]
    acc = acc + _shift_rows(x, 1, pos, seq_len) * w_ref[0, 2:3, cols]
    acc = acc + _shift_rows(x, 2, pos, seq_len) * w_ref[0, 3:4, cols]
    return acc


def _head_lane_expand(x, lane0):
    rows = x.shape[0]
    low = lax.broadcasted_iota(jnp.int32, (rows, HEAD_PAIR), 1) < DK_RET
    blocks = []
    for p in range(H_RET // 2):
        a = jnp.broadcast_to(x[:, lane0 + 2 * p:lane0 + 2 * p + 1], (rows, HEAD_PAIR))
        b = jnp.broadcast_to(x[:, lane0 + 2 * p + 1:lane0 + 2 * p + 2], (rows, HEAD_PAIR))
        blocks.append(jnp.where(low, a, b))
    return jnp.concatenate(blocks, axis=1)


def _ret_body(*refs, seq_len, rope, has_s0, emit_state):
    it = iter(refs)
    p_ref = next(it)
    dec_lane_ref = next(it)
    dec_wide_ref = next(it)
    gn_ref = next(it)
    cos_ref = sin_ref = s0_ref = st_ref = qk_ref = None
    if rope:
        cos_ref = next(it)
        sin_ref = next(it)
    if has_s0:
        s0_ref = next(it)
    y_ref = next(it)
    if emit_state:
        st_ref = next(it)
    if rope:
        qk_ref = next(it)

    C = SCAN_CHUNK
    nch = seq_len // C
    npair = H_RET // 2
    lgl = _log_sigmoid(dec_lane_ref[0])
    lgw = _log_sigmoid(dec_wide_ref[0])

    ii = lax.broadcasted_iota(jnp.int32, (C, C), 0)
    mm = lax.broadcasted_iota(jnp.int32, (C, C), 1)
    dif = (ii - mm).astype(F32)
    lower = ii >= mm
    upper = mm >= ii
    neg_inf = -jnp.inf
    dmats = []
    for h in range(H_RET):
        df = jnp.exp(jnp.where(lower, dif * lgw[h:h + 1, :], neg_inf))
        db = jnp.exp(jnp.where(upper, (-dif) * lgw[H_RET + h:H_RET + h + 1, :], neg_inf))
        dmats.append(df + db)

    lane = lax.broadcasted_iota(jnp.int32, (C, HEAD_PAIR), 1)
    low = lane < DK_RET
    r128 = lax.broadcasted_iota(jnp.int32, (HEAD_PAIR, HEAD_PAIR), 0) >= DK_RET
    c128 = lax.broadcasted_iota(jnp.int32, (HEAD_PAIR, HEAD_PAIR), 1) >= DK_RET
    same_head = r128 == c128
    avg = jnp.where(same_head, 1.0 / DK_RET, 0.0).astype(BF16)

    ri = lax.broadcasted_iota(jnp.int32, (C, W_RET), 0).astype(F32)
    need_states = emit_state or (has_s0 and nch > 1)
    if need_states:
        w_f = jnp.exp((C - 1.0 - ri) * lgl[0:1, :])
        w_b = jnp.exp(ri * lgl[1:2, :])
    if has_s0:
        e_f = jnp.exp((ri + 1.0) * lgl[0:1, :])
        e_b = jnp.exp((C - ri) * lgl[1:2, :])
        dec_f = jnp.exp(float(C) * lgl[0:1, :])
        dec_b = jnp.exp(float(C) * lgl[1:2, :])

    if rope:
        swap_low = (lax.broadcasted_iota(jnp.int32, (seq_len, HEAD_PAIR), 1) % 32) < 16
        cos = cos_ref[...]
        sin = sin_ref[...]
        for blk in range(2 * npair):
            cols_b = slice(blk * HEAD_PAIR, (blk + 1) * HEAD_PAIR)
            x = p_ref[:, cols_b]
            swapped = jnp.where(swap_low, pltpu.roll(x, HEAD_PAIR - 16, 1), pltpu.roll(x, 16, 1))
            qk_ref[:, cols_b] = x * cos + swapped * sin

    def rows(c):
        return slice(c * C, (c + 1) * C)

    def cols(p):
        return slice(p * HEAD_PAIR, (p + 1) * HEAD_PAIR)

    def get_q(c, p):
        src = qk_ref if rope else p_ref
        return src[rows(c), p * HEAD_PAIR:(p + 1) * HEAD_PAIR]

    def get_k(c, p):
        src = qk_ref if rope else p_ref
        return src[rows(c), W_RET + p * HEAD_PAIR:W_RET + (p + 1) * HEAD_PAIR]

    def get_v(c, p):
        return p_ref[rows(c), 2 * W_RET + p * HEAD_PAIR:2 * W_RET + (p + 1) * HEAD_PAIR]

    ds_f = [[None] * npair for _ in range(nch)]
    ds_b = [[None] * npair for _ in range(nch)]
    if need_states:
        for c in range(nch):
            for p in range(npair):
                k = get_k(c, p)
                vb = get_v(c, p).astype(BF16)
                kf = jnp.transpose(k * w_f[:, cols(p)]).astype(BF16)
                kb = jnp.transpose(k * w_b[:, cols(p)]).astype(BF16)
                ds_f[c][p] = _dot(kf, vb)
                ds_b[c][p] = _dot(kb, vb)

    if emit_state:
        for p in range(npair):
            for a in range(2):
                blk = slice(a * DK_RET, (a + 1) * DK_RET)
                st_ref[0, 0, 2 * p + a] = ds_f[0][p][blk, blk]
                st_ref[0, 1, 2 * p + a] = ds_b[0][p][blk, blk]

    sf_in = [[None] * npair for _ in range(nch)]
    sb_in = [[None] * npair for _ in range(nch)]
    if has_s0:
        for p in range(npair):
            s = s0_ref[0, 0, 0, p]
            for c in range(nch):
                sf_in[c][p] = s
                if c + 1 < nch:
                    s = s * dec_f[:, cols(p)] + jnp.where(same_head, ds_f[c][p], 0.0)
            s = s0_ref[0, 0, 1, p]
            for c in reversed(range(nch)):
                sb_in[c][p] = s
                if c > 0:
                    s = s * dec_b[:, cols(p)] + jnp.where(same_head, ds_b[c][p], 0.0)

    for c in range(nch):
        for p in range(npair):
            q = get_q(c, p)
            kb = get_k(c, p).astype(BF16)
            vb = get_v(c, p).astype(BF16)
            q0 = jnp.where(low, q, 0.0).astype(BF16)
            q1 = jnp.where(low, 0.0, q).astype(BF16)
            o0 = _dot((_dot_nt(q0, kb) * dmats[2 * p]).astype(BF16), vb)
            o1 = _dot((_dot_nt(q1, kb) * dmats[2 * p + 1]).astype(BF16), vb)
            o = jnp.where(low, o0, o1)
            if has_s0:
                o = o + _dot((q * e_f[:, cols(p)]).astype(BF16), sf_in[c][p].astype(BF16))
                o = o + _dot((q * e_b[:, cols(p)]).astype(BF16), sb_in[c][p].astype(BF16))
            mu = _dot(o.astype(BF16), avg)
            d = o - mu
            var = _dot((d * d).astype(BF16), avg)
            on = d * lax.rsqrt(var + EPS)
            g = p_ref[rows(c), 3 * W_RET + p * HEAD_PAIR:3 * W_RET + (p + 1) * HEAD_PAIR]
            y_ref[rows(c), cols(p)] = (on * gn_ref[0, 0:1, cols(p)] * _silu(g)).astype(BF16)


def _retention(proj_r, dec_lane, dec_wide, gn, seq_len, l, rope_tabs=None, s0=None):
    n = proj_r.shape[0]
    nseq = n // seq_len
    rope = rope_tabs is not None
    has_s0 = s0 is not None
    emit_state = not has_s0
    npair = H_RET // 2
    in_specs = [pl.BlockSpec((seq_len, W_GRP_R), lambda i: (i, 0)),
                _layer_spec(dec_lane, l), _layer_spec(dec_wide, l), _layer_spec(gn, l)]
    args = [proj_r, dec_lane, dec_wide, gn]
    if rope:
        in_specs += [_const_spec(rope_tabs[0].shape), _const_spec(rope_tabs[1].shape)]
        args += list(rope_tabs)
    if has_s0:
        in_specs.append(pl.BlockSpec((1, 1, 2, npair, HEAD_PAIR, HEAD_PAIR), lambda i: (i, l, 0, 0, 0, 0)))
        args.append(s0)
    out_specs = [pl.BlockSpec((seq_len, W_RET), lambda i: (i, 0))]
    out_shape = [jax.ShapeDtypeStruct((n, W_RET), BF16)]
    if emit_state:
        out_specs.append(pl.BlockSpec((1, 2, H_RET, DK_RET, DK_RET), lambda i: (i, 0, 0, 0, 0)))
        out_shape.append(jax.ShapeDtypeStruct((nseq, 2, H_RET, DK_RET, DK_RET), F32))
    scratch = [pltpu.VMEM((seq_len, 2 * W_RET), F32)] if rope else []
    return pl.pallas_call(
        functools.partial(_ret_body, seq_len=seq_len, rope=rope, has_s0=has_s0, emit_state=emit_state),
        grid=(nseq,), in_specs=in_specs, out_specs=out_specs, out_shape=out_shape,
        scratch_shapes=scratch, compiler_params=_params(1),
        name="retention_lat" if rope else "retention_ctx",
    )(*args)


def _ssd_body(*refs, seq_len, has_s0, emit_state):
    it = iter(refs)
    p_ref = next(it)
    cw_ref = next(it)
    cb_ref = next(it)
    dtb_ref = next(it)
    alog_ref = next(it)
    dskip_ref = next(it)
    gn_ref = next(it)
    s0_ref = st_ref = None
    if has_s0:
        s0_ref = next(it)
    y_ref = next(it)
    if emit_state:
        st_ref = next(it)
    xs_ref = next(it)
    bc_ref = next(it)

    T = seq_len
    C = SCAN_CHUNK
    nch = T // C
    pos = lax.broadcasted_iota(jnp.int32, (T, LANES), 0)

    for blk in range(CONV_CH // LANES):
        x = p_ref[:, W_SSD + blk * LANES:W_SSD + (blk + 1) * LANES]
        a = _silu(_dwconv4(x, cw_ref, cb_ref, blk * LANES, pos, T))
        if blk < W_SSD // LANES:
            xs_ref[:, blk * LANES:(blk + 1) * LANES] = a
        else:
            o = blk * LANES - W_SSD
            bc_ref[:, o:o + LANES] = a

    a_neg = -jnp.exp(alog_ref[0])
    ii = lax.broadcasted_iota(jnp.int32, (C, C), 0)
    mm = lax.broadcasted_iota(jnp.int32, (C, C), 1)
    lower = ii >= mm
    upper = mm >= ii
    tri_l = jnp.where(lower, 1.0, 0.0).astype(BF16)
    tri_u = jnp.where(upper, 1.0, 0.0).astype(BF16)
    neg_inf = -jnp.inf
    lane384 = lax.broadcasted_iota(jnp.int32, (C, W_SSD), 1)
    grp0 = lane384 < (W_SSD // G_SSD)
    lane_s = lax.broadcasted_iota(jnp.int32, (N_SSD, W_SSD), 1)
    grp0_s = lane_s < (W_SSD // G_SSD)
    low = lax.broadcasted_iota(jnp.int32, (C, HEAD_PAIR), 1) < P_SSD
    need_states = emit_state or (has_s0 and nch > 1)

    def rows(c):
        return slice(c * C, (c + 1) * C)

    def chunk_scalars(c):
        dt = _softplus(p_ref[rows(c), W_SSD + CONV_CH:W_SSD + CONV_CH + LANES] + dtb_ref[0])
        la = dt * a_neg
        a1 = la.astype(BF16)
        r1 = la - a1.astype(F32)
        a2 = r1.astype(BF16)
        a3 = (r1 - a2.astype(F32)).astype(BF16)
        pre = _dot(tri_l, a1) + _dot(tri_l, a2) + _dot(tri_l, a3)
        suf = _dot(tri_u, a1) + _dot(tri_u, a2) + _dot(tri_u, a3)
        return dt, pre, suf

    ds_f = [None] * nch
    ds_b = [None] * nch
    dec_f = [None] * nch
    dec_b = [None] * nch
    if need_states:
        for c in range(nch):
            dt, pre, suf = chunk_scalars(c)
            wf = jnp.exp(pre[C - 1:C, :] - pre) * dt
            wb = jnp.exp(suf[0:1, :] - suf) * dt
            xs = xs_ref[rows(c), :]
            xf = (xs * _head_lane_expand(wf, 0)).astype(BF16)
            xb = (xs * _head_lane_expand(wb, DT_BWD_LANE)).astype(BF16)
            bt0 = jnp.transpose(bc_ref[rows(c), 0:N_SSD]).astype(BF16)
            bt1 = jnp.transpose(bc_ref[rows(c), N_SSD:2 * N_SSD]).astype(BF16)
            ds_f[c] = jnp.where(grp0_s, _dot(bt0, xf), _dot(bt1, xf))
            ds_b[c] = jnp.where(grp0_s, _dot(bt0, xb), _dot(bt1, xb))
            dec_f[c] = _head_lane_expand(jnp.exp(pre[C - 1:C, :]), 0)
            dec_b[c] = _head_lane_expand(jnp.exp(suf[0:1, :]), DT_BWD_LANE)

    if emit_state:
        for h in range(H_SSD):
            st_ref[0, 0, h] = ds_f[0][:, h * P_SSD:(h + 1) * P_SSD]
            st_ref[0, 1, h] = ds_b[0][:, h * P_SSD:(h + 1) * P_SSD]

    sf_in = [None] * nch
    sb_in = [None] * nch
    if has_s0:
        s = s0_ref[0, 0, 0]
        for c in range(nch):
            sf_in[c] = s
            if c + 1 < nch:
                s = s * dec_f[c] + ds_f[c]
        s = s0_ref[0, 0, 1]
        for c in reversed(range(nch)):
            sb_in[c] = s
            if c > 0:
                s = s * dec_b[c] + ds_b[c]

    for c in range(nch):
        dt, pre, suf = chunk_scalars(c)
        pre_t = jnp.transpose(pre)
        suf_t = jnp.transpose(suf)
        dt_t = jnp.transpose(dt)
        xs = xs_ref[rows(c), :]
        xs_b = xs.astype(BF16)
        cm = [bc_ref[rows(c), 2 * N_SSD + g * N_SSD:2 * N_SSD + (g + 1) * N_SSD].astype(BF16)
              for g in range(G_SSD)]
        bm = [bc_ref[rows(c), g * N_SSD:(g + 1) * N_SSD].astype(BF16) for g in range(G_SSD)]
        gram = [_dot_nt(cm[g], bm[g]) for g in range(G_SSD)]
        heads = []
        for h in range(H_SSD):
            g = h // (H_SSD // G_SSD)
            hb = DT_BWD_LANE + h
            df = jnp.exp(jnp.where(lower, pre[:, h:h + 1] - pre_t[h:h + 1, :], neg_inf))
            db = jnp.exp(jnp.where(upper, suf[:, hb:hb + 1] - suf_t[hb:hb + 1, :], neg_inf))
            w = gram[g] * (df * dt_t[h:h + 1, :] + db * dt_t[hb:hb + 1, :])
            p = h // 2
            heads.append(_dot(w.astype(BF16), xs_b[:, p * HEAD_PAIR:(p + 1) * HEAD_PAIR]))
        o = jnp.concatenate([jnp.where(low, heads[2 * p], heads[2 * p + 1]) for p in range(H_SSD // 2)],
                            axis=1)
        if has_s0:
            sf = sf_in[c].astype(BF16)
            sb = sb_in[c].astype(BF16)
            o = o + _head_lane_expand(jnp.exp(pre), 0) * jnp.where(grp0, _dot(cm[0], sf), _dot(cm[1], sf))
            o = o + (_head_lane_expand(jnp.exp(suf), DT_BWD_LANE)
                     * jnp.where(grp0, _dot(cm[0], sb), _dot(cm[1], sb)))
        y = o + dskip_ref[0] * xs
        yz = y * _silu(p_ref[rows(c), 0:W_SSD])
        y_ref[rows(c), :] = (_rms_scale(yz) * gn_ref[0]).astype(BF16)


def _ssd(proj_s, cw, cb, dtb, alog, dskip, gn, seq_len, l, s0=None):
    n = proj_s.shape[0]
    nseq = n // seq_len
    has_s0 = s0 is not None
    emit_state = not has_s0
    in_specs = [pl.BlockSpec((seq_len, W_GRP_S), lambda i: (i, 0))]
    in_specs += [_layer_spec(a, l) for a in (cw, cb, dtb, alog, dskip, gn)]
    args = [proj_s, cw, cb, dtb, alog, dskip, gn]
    if has_s0:
        in_specs.append(pl.BlockSpec((1, 1, 2, N_SSD, W_SSD), lambda i: (i, l, 0, 0, 0)))
        args.append(s0)
    out_specs = [pl.BlockSpec((seq_len, W_SSD), lambda i: (i, 0))]
    out_shape = [jax.ShapeDtypeStruct((n, W_SSD), BF16)]
    if emit_state:
        out_specs.append(pl.BlockSpec((1, 2, H_SSD, N_SSD, P_SSD), lambda i: (i, 0, 0, 0, 0)))
        out_shape.append(jax.ShapeDtypeStruct((nseq, 2, H_SSD, N_SSD, P_SSD), F32))
    scratch = [pltpu.VMEM((seq_len, W_SSD), F32), pltpu.VMEM((seq_len, 2 * G_SSD * N_SSD), F32)]
    return pl.pallas_call(
        functools.partial(_ssd_body, seq_len=seq_len, has_s0=has_s0, emit_state=emit_state),
        grid=(nseq,), in_specs=in_specs, out_specs=out_specs, out_shape=out_shape,
        scratch_shapes=scratch, compiler_params=_params(1),
        name="ssd_lat" if has_s0 else "ssd_ctx",
    )(*args)


def _lru_body(*refs, seq_len, has_h0, emit_state):
    it = iter(refs)
    p_ref = next(it)
    cw_ref = next(it)
    cb_ref = next(it)
    wg_ref = next(it)
    bg_ref = next(it)
    lam_ref = next(it)
    h0_ref = st_ref = None
    if has_h0:
        h0_ref = next(it)
    y_ref = next(it)
    if emit_state:
        st_ref = next(it)

    T = seq_len
    pos = lax.broadcasted_iota(jnp.int32, (T, W_LRU), 0)
    xc = _dwconv4(p_ref[:, 0:W_LRU], cw_ref, cb_ref, 0, pos, T)
    gates = _sigmoid(_dot(xc.astype(BF16), wg_ref[0]) + bg_ref[0])
    decay_rate = _softplus(-lam_ref[0])
    total = None
    for d in range(2):
        r = gates[:, 2 * d * W_LRU:(2 * d + 1) * W_LRU]
        i = gates[:, (2 * d + 1) * W_LRU:(2 * d + 2) * W_LRU]
        log_a = (-LRU_C * r) * decay_rate[d:d + 1, :]
        a = jnp.exp(log_a)
        th = jnp.tanh(log_a)
        b = jnp.sqrt(-2.0 * th / (1.0 - th)) * (i * xc)
        step = 1
        while step < T:
            if d == 0:
                ok = pos >= step
                shift = step
            else:
                ok = pos <= T - 1 - step
                shift = T - step
            a_prev = jnp.where(ok, pltpu.roll(a, shift, 0), 1.0)
            b_prev = jnp.where(ok, pltpu.roll(b, shift, 0), 0.0)
            b = a * b_prev + b
            a = a * a_prev
            step *= 2
        hs = b
        if has_h0:
            hs = hs + a * h0_ref[0, 0, d:d + 1, :]
        if emit_state:
            last = T - 1 if d == 0 else 0
            st_ref[0, d:d + 1, :] = hs[last:last + 1, :]
        total = hs if total is None else total + hs
    y_ref[...] = (total * _gelu_tanh(p_ref[:, W_LRU:2 * W_LRU])).astype(BF16)


def _lru(proj_l, cw, cb, wg, bg, lam, seq_len, l, h0=None):
    n = proj_l.shape[0]
    nseq = n // seq_len
    has_h0 = h0 is not None
    emit_state = not has_h0
    in_specs = [pl.BlockSpec((seq_len, W_GRP_L), lambda i: (i, 0))]
    in_specs += [_layer_spec(a, l) for a in (cw, cb, wg, bg, lam)]
    args = [proj_l, cw, cb, wg, bg, lam]
    if has_h0:
        in_specs.append(pl.BlockSpec((1, 1, 2, W_LRU), lambda i: (i, l, 0, 0)))
        args.append(h0)
    out_specs = [pl.BlockSpec((seq_len, W_LRU), lambda i: (i, 0))]
    out_shape = [jax.ShapeDtypeStruct((n, W_LRU), BF16)]
    if emit_state:
        out_specs.append(pl.BlockSpec((1, 2, W_LRU), lambda i: (i, 0, 0)))
        out_shape.append(jax.ShapeDtypeStruct((nseq, 2, W_LRU), F32))
    return pl.pallas_call(
        functools.partial(_lru_body, seq_len=seq_len, has_h0=has_h0, emit_state=emit_state),
        grid=(nseq,), in_specs=in_specs, out_specs=out_specs, out_shape=out_shape,
        compiler_params=_params(1),
        name="lru_lat" if has_h0 else "lru_ctx",
    )(*args)


def _outproj_body(x_ref, yr_ref, ys_ref, yl_ref, mod_ref, g_ref, wo_ref, x1_ref, h2_ref):
    mix = _dot(yr_ref[...], wo_ref[0, 0:W_RET, :])
    mix = mix + _dot(ys_ref[...], wo_ref[0, W_RET:W_RET + W_SSD, :])
    mix = mix + _dot(yl_ref[...], wo_ref[0, W_RET + W_SSD:D_MODEL, :])
    x1 = x_ref[...] + mod_ref[0, 0, 2:3, :] * mix
    x1_ref[...] = x1
    y = _rms_scale(x1) * g_ref[0]
    h2_ref[...] = (y * (1.0 + mod_ref[0, 0, 4:5, :]) + mod_ref[0, 0, 3:4, :]).astype(BF16)


def _out_proj(x2d, y_ret, y_ssd, y_lru, mods, row0, rows_per_mod, gain, wo, l):
    n = x2d.shape[0]
    tm = ROW_TILE
    row_spec = lambda w: pl.BlockSpec((tm, w), lambda i: (i, 0))
    return pl.pallas_call(
        _outproj_body,
        grid=(n // tm,),
        in_specs=[row_spec(D_MODEL), row_spec(W_RET), row_spec(W_SSD), row_spec(W_LRU),
                  _mod_spec(row0, rows_per_mod, tm, l), _layer_spec(gain, l), _layer_spec(wo, l)],
        out_specs=[row_spec(D_MODEL), row_spec(D_MODEL)],
        out_shape=[jax.ShapeDtypeStruct((n, D_MODEL), F32), jax.ShapeDtypeStruct((n, D_MODEL), BF16)],
        compiler_params=_params(1),
        name="out_proj",
    )(x2d, y_ret, y_ssd, y_lru, mods, gain, wo)


def _ffn_body(h_ref, x1_ref, mod_ref, wup_ref, cw_ref, cb_ref, wd_ref, fg_ref, o_ref, *, seq_len, final_norm):
    tm = h_ref.shape[0]
    nseq = tm // seq_len
    row8 = lax.broadcasted_iota(jnp.int32, (SUBLANES, FF_BLOCK), 0)
    first_row = row8 == 0
    last_row = row8 == SUBLANES - 1

    def conv3(u, col0):
        cols = slice(col0, col0 + FF_BLOCK)
        w_prev, w_mid, w_next = (cw_ref[0, t:t + 1, cols] for t in range(3))
        bias = cb_ref[0, 0:1, cols]
        outs = []
        for s in range(nseq):
            us = u[s * seq_len:(s + 1) * seq_len]
            prev = pltpu.roll(us, 1, 0)
            nxt = pltpu.roll(us, seq_len - 1, 0)
            prev = jnp.concatenate([jnp.where(first_row, 0.0, prev[:SUBLANES]), prev[SUBLANES:]], axis=0)
            nxt = jnp.concatenate([nxt[:seq_len - SUBLANES], jnp.where(last_row, 0.0, nxt[seq_len - SUBLANES:])],
                                  axis=0)
            outs.append(us * w_mid + bias + prev * w_prev + nxt * w_next)
        return outs[0] if nseq == 1 else jnp.concatenate(outs, axis=0)

    h = h_ref[...]

    def up(j):
        return (_dot(h, wup_ref[0, :, j * FF_BLOCK:(j + 1) * FF_BLOCK]),
                _dot(h, wup_ref[0, :, D_FF + j * FF_BLOCK:D_FF + (j + 1) * FF_BLOCK]))

    acc = None
    u_next = up(0)
    for j in range(N_FF_BLOCKS):
        uv, ug = u_next
        if j + 1 < N_FF_BLOCKS:
            u_next = up(j + 1)
        val = conv3(uv, j * FF_BLOCK)
        gate = conv3(ug, D_FF + j * FF_BLOCK)
        act = (_silu(gate) * val).astype(BF16)
        part = _dot(act, wd_ref[0, j * FF_BLOCK:(j + 1) * FF_BLOCK, :])
        acc = part if acc is None else acc + part
    out = x1_ref[...] + mod_ref[0, 0, 5:6, :] * acc
    if final_norm:
        out = _rms_scale(out) * fg_ref[...]
    o_ref[...] = out


def _ffn(h2, x1, mods, row0, rows_per_mod, wup, cw, cb, wd, fgain, seq_len, final_norm, l):
    n = h2.shape[0]
    tm = FFN_ROW_TILE
    row_spec = pl.BlockSpec((tm, D_MODEL), lambda i: (i, 0))
    return pl.pallas_call(
        functools.partial(_ffn_body, seq_len=seq_len, final_norm=final_norm),
        grid=(n // tm,),
        in_specs=[row_spec, row_spec, _mod_spec(row0, rows_per_mod, tm, l),
                  _layer_spec(wup, l), _layer_spec(cw, l), _layer_spec(cb, l), _layer_spec(wd, l),
                  _const_spec(fgain.shape)],
        out_specs=row_spec,
        out_shape=jax.ShapeDtypeStruct((n, D_MODEL), F32),
        compiler_params=_params(1),
        name="ffn",
    )(h2, x1, mods, wup, cw, cb, wd, fgain)


def _rope_tables(t_len):
    tok = jnp.arange(t_len)
    row_pos = (tok // GRID_W).astype(F32)
    col_pos = (tok % GRID_W).astype(F32)
    lane = jnp.arange(HEAD_PAIR)
    d = lane % DK_RET
    use_row = d < (DK_RET // 2)
    half = DK_RET // 4
    freqs = ROPE_BASE ** (-(d % half).astype(F32) / half)
    ang = jnp.where(use_row[None, :], row_pos[:, None], col_pos[:, None]) * freqs[None, :]
    first = (d % (2 * half)) < half
    return jnp.cos(ang), jnp.where(first[None, :], -jnp.sin(ang), jnp.sin(ang))


def _prep_params(w_in, ret_decay, ret_norm_g, ssd_conv_b, ssd_dt_bias, ssd_a_log, ssd_d, ssd_norm_g,
                 lru_conv_b, lru_w_a, lru_b_a, lru_w_x, lru_b_x, w_out, ffn_w_up, ffn_conv_b, ffn_w_down):
    o_z = 4 * W_RET
    o_dt = o_z + W_SSD + CONV_CH
    o_xl = o_dt + H_SSD
    dt_cols = w_in[..., o_dt:o_dt + H_SSD]
    gap = jnp.zeros((DEPTH, D_MODEL, DT_BWD_LANE - H_SSD), F32)
    tail = jnp.zeros((DEPTH, D_MODEL, LANES - DT_BWD_LANE - H_SSD), F32)
    w_all = jnp.concatenate([w_in[..., :W_RET] * (DK_RET ** -0.5), w_in[..., W_RET:o_dt],
                             dt_cols, gap, dt_cols, tail, w_in[..., o_xl:o_xl + W_GRP_L]], axis=-1).astype(BF16)

    def lane_place(pair):
        z = jnp.zeros((DEPTH, LANES), F32)
        return jnp.concatenate([pair[:, 0], z[:, :DT_BWD_LANE - H_SSD], pair[:, 1],
                                z[:, :LANES - DT_BWD_LANE - H_SSD]], axis=-1)[:, None, :]

    eye = jnp.eye(LRU_BLOCKS, dtype=F32)
    gates = jnp.stack([lru_w_a, lru_w_x], axis=2)
    wgate = jnp.einsum("lxgncd,nm->lncxgmd", gates, eye).reshape(DEPTH, W_LRU, 4 * W_LRU).astype(BF16)
    bgate = jnp.stack([lru_b_a, lru_b_x], axis=2).reshape(DEPTH, 1, 4 * W_LRU)
    return dict(
        w_all=w_all,
        dec_lane=jnp.repeat(ret_decay, DK_RET, axis=-1),
        dec_wide=jnp.broadcast_to(ret_decay.reshape(DEPTH, 2 * H_RET, 1), (DEPTH, 2 * H_RET, SCAN_CHUNK)),
        ret_gn=ret_norm_g[:, None, :],
        ssd_cb=ssd_conv_b[:, None, :],
        ssd_dtb=lane_place(ssd_dt_bias), ssd_alog=lane_place(ssd_a_log),
        ssd_dskip=jnp.repeat(ssd_d, P_SSD, axis=-1)[:, None, :], ssd_gn=ssd_norm_g[:, None, :],
        lru_cb=lru_conv_b[:, None, :], lru_wg=wgate, lru_bg=bgate,
        wo=w_out.astype(BF16), wup=ffn_w_up.astype(BF16), ffn_cb=ffn_conv_b[:, None, :],
        wd=ffn_w_down.astype(BF16),
    )


def kernel(x_prompt, x_sample, c, c_ctx, state_ret, state_ssd, state_lru, w_ada, b_ada, norm1_g, norm2_g, w_in, ret_decay, ret_norm_g, ssd_conv_w, ssd_conv_b, ssd_dt_bias, ssd_a_log, ssd_d, ssd_norm_g, lru_conv_w, lru_conv_b, lru_w_a, lru_b_a, lru_w_x, lru_b_x, lru_lambda, w_out, ffn_w_up, ffn_conv_w, ffn_conv_b, ffn_w_down, final_norm_g):
    bp, tp, _ = x_prompt.shape
    bs, ts, _ = x_sample.shape
    assert tp == SCAN_CHUNK and ts % SCAN_CHUNK == 0 and ts % GRID_W == 0

    n_req = 8
    cvec = jnp.concatenate([c_ctx[None, :], c, jnp.zeros((n_req - 1 - bs, D_MODEL), F32)], axis=0)
    mods = _modulation(cvec, w_ada, b_ada).reshape(DEPTH, n_req, 6, D_MODEL)

    pp = _prep_params(w_in, ret_decay, ret_norm_g, ssd_conv_b, ssd_dt_bias, ssd_a_log, ssd_d, ssd_norm_g,
                      lru_conv_b, lru_w_a, lru_b_a, lru_w_x, lru_b_x, w_out, ffn_w_up, ffn_conv_b, ffn_w_down)
    n1 = norm1_g[:, None, :]
    n2 = norm2_g[:, None, :]
    rope_tabs = _rope_tables(ts)
    fgain = final_norm_g[None, :]

    npair = H_RET // 2
    sr = state_ret.reshape(bs, DEPTH, 2, npair, 2, DK_RET, DK_RET)
    s0_ret = jnp.einsum("bldparc,ax->bldparxc", sr, jnp.eye(2, dtype=F32)).reshape(
        bs, DEPTH, 2, npair, HEAD_PAIR, HEAD_PAIR)
    s0_ssd = state_ssd.transpose(0, 1, 2, 4, 3, 5).reshape(bs, DEPTH, 2, N_SSD, W_SSD)

    def trunk_layer(x2d, l, seq_len, row0, rows_per_mod, latent):
        pr, ps, plru = _in_proj(x2d, mods, row0, rows_per_mod, n1, pp["w_all"], l)
        ret = _retention(pr, pp["dec_lane"], pp["dec_wide"], pp["ret_gn"], seq_len, l,
                         rope_tabs if latent else None, s0_ret if latent else None)
        ssd = _ssd(ps, ssd_conv_w, pp["ssd_cb"], pp["ssd_dtb"], pp["ssd_alog"], pp["ssd_dskip"], pp["ssd_gn"],
                   seq_len, l, s0_ssd if latent else None)
        lru = _lru(plru, lru_conv_w, pp["lru_cb"], pp["lru_wg"], pp["lru_bg"], lru_lambda, seq_len, l,
                   state_lru if latent else None)
        x1, h2 = _out_proj(x2d, ret[0], ssd[0], lru[0], mods, row0, rows_per_mod, n2, pp["wo"], l)
        out = _ffn(h2, x1, mods, row0, rows_per_mod, pp["wup"], ffn_conv_w, pp["ffn_cb"], pp["wd"], fgain,
                   seq_len, l == DEPTH - 1, l)
        states = None if latent else (ret[1], ssd[1], lru[1])
        return out, states

    xp = x_prompt.reshape(bp * tp, D_MODEL)
    xs = x_sample.reshape(bs * ts, D_MODEL)
    new_ret, new_ssd, new_lru = [], [], []
    for l in range(DEPTH):
        xp, (s_r, s_s, s_l) = trunk_layer(xp, l, tp, 0, bp * tp, False)
        new_ret.append(s_r)
        new_ssd.append(s_s)
        new_lru.append(s_l)
        xs, _ = trunk_layer(xs, l, ts, 1, ts, True)

    return (xp.reshape(bp, tp, D_MODEL), xs.reshape(bs, ts, D_MODEL), jnp.stack(new_ret, axis=1),
            jnp.stack(new_ssd, axis=1), jnp.stack(new_lru, axis=1))
```

```python
import functools
import math

import jax
import jax.numpy as jnp
from jax import lax
from jax.experimental import pallas as pl
from jax.experimental.pallas import tpu as pltpu

F32 = jnp.float32
BF16 = jnp.bfloat16

D_MODEL = 1024
DEPTH = 2
GRID_W = 64
W_RET = 384
H_RET = 6
DK_RET = 64
W_SSD = 384
P_SSD = 64
H_SSD = 6
N_SSD = 128
G_SSD = 2
CONV_CH = W_SSD + 2 * G_SSD * N_SSD
W_LRU = 256
LRU_BLOCKS = 4
LRU_BW = 64
LRU_C = 8.0
D_FF = 2816
ROPE_BASE = 10000.0
EPS = 1e-6

LANES = 128
SUBLANES = 8
HEAD_PAIR = LANES
SCAN_CHUNK = 256
FF_BLOCK = 256
N_FF_BLOCKS = D_FF // FF_BLOCK
ROW_TILE = 512
FFN_ROW_TILE = 1024
PROJ_COL_BLOCK = 512
VMEM_LIMIT = 56 * 1024 * 1024

W_GRP_R = 4 * W_RET
W_GRP_S = W_SSD + CONV_CH + LANES
W_GRP_L = 2 * W_LRU
W_IN_PAD = W_GRP_R + W_GRP_S + W_GRP_L
DT_BWD_LANE = 8


def _dot(a, b):
    return jnp.dot(a, b, preferred_element_type=F32)


def _dot_nt(a, b):
    return lax.dot_general(a, b, (((1,), (1,)), ((), ())), preferred_element_type=F32)


def _sigmoid(x):
    return 1.0 / (1.0 + jnp.exp(-x))


def _silu(x):
    return x * _sigmoid(x)


def _softplus(x):
    return jnp.maximum(x, 0.0) + jnp.log1p(jnp.exp(-jnp.abs(x)))


def _log_sigmoid(x):
    return -_softplus(-x)


def _gelu_tanh(x):
    c = math.sqrt(2.0 / math.pi)
    return 0.5 * x * (1.0 + jnp.tanh(c * (x + 0.044715 * (x * x * x))))


def _rms_scale(x):
    return x * lax.rsqrt(jnp.mean(x * x, axis=-1, keepdims=True) + EPS)


def _const_spec(shape):
    zeros = (0,) * len(shape)
    return pl.BlockSpec(shape, lambda *_: zeros, pipeline_mode=pl.Buffered(1))


def _layer_spec(arr, l):
    rest = (0,) * (arr.ndim - 1)
    return pl.BlockSpec((1,) + arr.shape[1:], lambda *_: (l,) + rest, pipeline_mode=pl.Buffered(1))


def _mod_spec(row0, rows_per_mod, tm, l):
    return pl.BlockSpec((1, 1, 6, D_MODEL), lambda i: (l, row0 + i * tm // rows_per_mod, 0, 0))


def _params(n_axes):
    return pltpu.CompilerParams(dimension_semantics=("arbitrary",) * n_axes,
                                vmem_limit_bytes=VMEM_LIMIT)


def _mod_body(c_ref, w_ref, b_ref, o_ref):
    s = _silu(c_ref[...]).astype(BF16)
    o_ref[0] = _dot(s, w_ref[0].astype(BF16)) + b_ref[0]


def _modulation(cvec, w_ada, b_ada):
    rows = cvec.shape[0]
    nblk = (6 * D_MODEL) // D_MODEL
    return pl.pallas_call(
        _mod_body,
        grid=(DEPTH, nblk),
        in_specs=[pl.BlockSpec((rows, D_MODEL), lambda l, j: (0, 0)),
                  pl.BlockSpec((1, D_MODEL, D_MODEL), lambda l, j: (l, 0, j)),
                  pl.BlockSpec((1, 1, D_MODEL), lambda l, j: (l, 0, j))],
        out_specs=pl.BlockSpec((1, rows, D_MODEL), lambda l, j: (l, 0, j)),
        out_shape=jax.ShapeDtypeStruct((DEPTH, rows, 6 * D_MODEL), F32),
        compiler_params=_params(2),
        name="adaln_mod",
    )(cvec, w_ada, b_ada.reshape(DEPTH, 1, 6 * D_MODEL))


def _inproj_body(x_ref, mod_ref, g_ref, w_ref, or_ref, os_ref, ol_ref):
    y = _rms_scale(x_ref[...]) * g_ref[0]
    h = (y * (1.0 + mod_ref[0, 0, 1:2, :]) + mod_ref[0, 0, 0:1, :]).astype(BF16)
    col0 = 0
    for o_ref in (or_ref, os_ref, ol_ref):
        n = o_ref.shape[1]
        for j in range(0, n, PROJ_COL_BLOCK):
            jb = min(PROJ_COL_BLOCK, n - j)
            o_ref[:, j:j + jb] = _dot(h, w_ref[0, :, col0 + j:col0 + j + jb])
        col0 += n


def _in_proj(x2d, mods, row0, rows_per_mod, gain, w_all, l):
    n = x2d.shape[0]
    tm = ROW_TILE
    return pl.pallas_call(
        _inproj_body,
        grid=(n // tm,),
        in_specs=[pl.BlockSpec((tm, D_MODEL), lambda i: (i, 0)),
                  _mod_spec(row0, rows_per_mod, tm, l),
                  _layer_spec(gain, l), _layer_spec(w_all, l)],
        out_specs=[pl.BlockSpec((tm, W_GRP_R), lambda i: (i, 0)),
                   pl.BlockSpec((tm, W_GRP_S), lambda i: (i, 0)),
                   pl.BlockSpec((tm, W_GRP_L), lambda i: (i, 0))],
        out_shape=[jax.ShapeDtypeStruct((n, W_GRP_R), F32),
                   jax.ShapeDtypeStruct((n, W_GRP_S), F32),
                   jax.ShapeDtypeStruct((n, W_GRP_L), F32)],
        compiler_params=_params(1),
        name="in_proj",
    )(x2d, mods, gain, w_all)


def _shift_rows(x, k, pos, seq_len):
    n = x.shape[0]
    if k == 0:
        return x
    rolled = pltpu.roll(x, (-k) % n, 0)
    ok = (pos >= -k) if k < 0 else (pos <= seq_len - 1 - k)
    return jnp.where(ok, rolled, 0.0)


def _dwconv4(x, w_ref, b_ref, col0, pos, seq_len):
    width = x.shape[1]
    cols = slice(col0, col0 + width)
    acc = x * w_ref[0, 1:2, cols] + b_ref[0, 0:1, cols]
    acc = acc + _shift_rows(x, -1, pos, seq_len) * w_ref[0, 0:1, cols]
    acc = acc + _shift_rows(x, 1, pos, seq_len) * w_ref[0, 2:3, cols]
    acc = acc + _shift_rows(x, 2, pos, seq_len) * w_ref[0, 3:4, cols]
    return acc


def _zero_later_layers(st_ref):
    st_ref[0, 1:] = jnp.zeros((DEPTH - 1,) + tuple(st_ref.shape[2:]), F32)


def _state_output(nseq, tail, l, prev, n_inputs):
    zeros = (0,) * len(tail)
    shape = jax.ShapeDtypeStruct((nseq, DEPTH) + tail, F32)
    if l == 0:
        return pl.BlockSpec((1, DEPTH) + tail, lambda i: (i, 0) + zeros), shape, [], [], {}
    spec = pl.BlockSpec((1, 1) + tail, lambda i: (i, l) + zeros)
    return spec, shape, [pl.BlockSpec(memory_space=pl.ANY)], [prev], {n_inputs: 1}


def _head_lane_expand(x, lane0):
    rows = x.shape[0]
    low = lax.broadcasted_iota(jnp.int32, (rows, HEAD_PAIR), 1) < DK_RET
    blocks = []
    for p in range(H_RET // 2):
        a = jnp.broadcast_to(x[:, lane0 + 2 * p:lane0 + 2 * p + 1], (rows, HEAD_PAIR))
        b = jnp.broadcast_to(x[:, lane0 + 2 * p + 1:lane0 + 2 * p + 2], (rows, HEAD_PAIR))
        blocks.append(jnp.where(low, a, b))
    return jnp.concatenate(blocks, axis=1)


def _ret_body(*refs, seq_len, rope, has_s0, emit_state, first_layer):
    it = iter(refs)
    p_ref = next(it)
    dec_lane_ref = next(it)
    dec_wide_ref = next(it)
    gn_ref = next(it)
    cos_ref = sin_ref = s0_ref = st_ref = qk_ref = None
    if rope:
        cos_ref = next(it)
        sin_ref = next(it)
    if has_s0:
        s0_ref = next(it)
    if emit_state and not first_layer:
        next(it)
    y_ref = next(it)
    if emit_state:
        st_ref = next(it)
    dmat_ref = next(it)
    wts_ref = next(it)
    if rope:
        qk_ref = next(it)

    C = SCAN_CHUNK
    nch = seq_len // C
    npair = H_RET // 2
    lgl = _log_sigmoid(dec_lane_ref[0])

    @pl.when(pl.program_id(0) == 0)
    def _():
        lgw = _log_sigmoid(dec_wide_ref[0])
        ii = lax.broadcasted_iota(jnp.int32, (C, C), 0)
        mm = lax.broadcasted_iota(jnp.int32, (C, C), 1)
        dif = (ii - mm).astype(F32)
        neg_inf = -jnp.inf
        for h in range(H_RET):
            df = jnp.exp(jnp.where(ii >= mm, dif * lgw[h:h + 1, :], neg_inf))
            db = jnp.exp(jnp.where(mm >= ii, (-dif) * lgw[H_RET + h:H_RET + h + 1, :], neg_inf))
            dmat_ref[h] = df + db
        ri = lax.broadcasted_iota(jnp.int32, (C, W_RET), 0).astype(F32)
        wts_ref[0] = jnp.exp((C - 1.0 - ri) * lgl[0:1, :])
        wts_ref[1] = jnp.exp(ri * lgl[1:2, :])
        wts_ref[2] = jnp.exp((ri + 1.0) * lgl[0:1, :])
        wts_ref[3] = jnp.exp((C - ri) * lgl[1:2, :])

    lane = lax.broadcasted_iota(jnp.int32, (C, HEAD_PAIR), 1)
    low = lane < DK_RET
    r128 = lax.broadcasted_iota(jnp.int32, (HEAD_PAIR, HEAD_PAIR), 0) >= DK_RET
    c128 = lax.broadcasted_iota(jnp.int32, (HEAD_PAIR, HEAD_PAIR), 1) >= DK_RET
    same_head = r128 == c128
    avg = jnp.where(same_head, 1.0 / DK_RET, 0.0).astype(BF16)

    need_states = emit_state or (has_s0 and nch > 1)
    if has_s0:
        dec_f = jnp.exp(float(C) * lgl[0:1, :])
        dec_b = jnp.exp(float(C) * lgl[1:2, :])

    if rope:
        swap_low = (lax.broadcasted_iota(jnp.int32, (seq_len, HEAD_PAIR), 1) % 32) < 16
        cos = cos_ref[...]
        sin = sin_ref[...]
        for blk in range(2 * npair):
            cols_b = slice(blk * HEAD_PAIR, (blk + 1) * HEAD_PAIR)
            x = p_ref[:, cols_b]
            swapped = jnp.where(swap_low, pltpu.roll(x, HEAD_PAIR - 16, 1), pltpu.roll(x, 16, 1))
            qk_ref[:, cols_b] = x * cos + swapped * sin

    def rows(c):
        return slice(c * C, (c + 1) * C)

    def cols(p):
        return slice(p * HEAD_PAIR, (p + 1) * HEAD_PAIR)

    def get_q(c, p):
        src = qk_ref if rope else p_ref
        return src[rows(c), p * HEAD_PAIR:(p + 1) * HEAD_PAIR]

    def get_k(c, p):
        src = qk_ref if rope else p_ref
        return src[rows(c), W_RET + p * HEAD_PAIR:W_RET + (p + 1) * HEAD_PAIR]

    def get_v(c, p):
        return p_ref[rows(c), 2 * W_RET + p * HEAD_PAIR:2 * W_RET + (p + 1) * HEAD_PAIR]

    ds_f = [[None] * npair for _ in range(nch)]
    ds_b = [[None] * npair for _ in range(nch)]
    if need_states:
        for c in range(nch):
            for p in range(npair):
                k = get_k(c, p)
                vb = get_v(c, p).astype(BF16)
                kf = jnp.transpose(k * wts_ref[0, :, cols(p)]).astype(BF16)
                kb = jnp.transpose(k * wts_ref[1, :, cols(p)]).astype(BF16)
                ds_f[c][p] = _dot(kf, vb)
                ds_b[c][p] = _dot(kb, vb)

    if emit_state:
        if first_layer:
            _zero_later_layers(st_ref)
        for p in range(npair):
            for a in range(2):
                blk = slice(a * DK_RET, (a + 1) * DK_RET)
                st_ref[0, 0, 0, 2 * p + a] = ds_f[0][p][blk, blk]
                st_ref[0, 0, 1, 2 * p + a] = ds_b[0][p][blk, blk]

    sf_in = [[None] * npair for _ in range(nch)]
    sb_in = [[None] * npair for _ in range(nch)]
    if has_s0:
        for p in range(npair):
            s = s0_ref[0, 0, 0, p]
            for c in range(nch):
                sf_in[c][p] = s
                if c + 1 < nch:
                    s = s * dec_f[:, cols(p)] + jnp.where(same_head, ds_f[c][p], 0.0)
            s = s0_ref[0, 0, 1, p]
            for c in reversed(range(nch)):
                sb_in[c][p] = s
                if c > 0:
                    s = s * dec_b[:, cols(p)] + jnp.where(same_head, ds_b[c][p], 0.0)

    for c in range(nch):
        for p in range(npair):
            q = get_q(c, p)
            kb = get_k(c, p).astype(BF16)
            vb = get_v(c, p).astype(BF16)
            q0 = jnp.where(low, q, 0.0).astype(BF16)
            q1 = jnp.where(low, 0.0, q).astype(BF16)
            o0 = _dot((_dot_nt(q0, kb) * dmat_ref[2 * p]).astype(BF16), vb)
            o1 = _dot((_dot_nt(q1, kb) * dmat_ref[2 * p + 1]).astype(BF16), vb)
            o = jnp.where(low, o0, o1)
            if has_s0:
                o = o + _dot((q * wts_ref[2, :, cols(p)]).astype(BF16), sf_in[c][p].astype(BF16))
                o = o + _dot((q * wts_ref[3, :, cols(p)]).astype(BF16), sb_in[c][p].astype(BF16))
            mu = _dot(o.astype(BF16), avg)
            d = o - mu
            var = _dot((d * d).astype(BF16), avg)
            on = d * lax.rsqrt(var + EPS)
            g = p_ref[rows(c), 3 * W_RET + p * HEAD_PAIR:3 * W_RET + (p + 1) * HEAD_PAIR]
            y_ref[rows(c), cols(p)] = (on * gn_ref[0, 0:1, cols(p)] * _silu(g)).astype(BF16)


def _retention(proj_r, dec_lane, dec_wide, gn, seq_len, l, rope_tabs=None, s0=None, prev_states=None):
    n = proj_r.shape[0]
    nseq = n // seq_len
    rope = rope_tabs is not None
    has_s0 = s0 is not None
    emit_state = not has_s0
    npair = H_RET // 2
    in_specs = [pl.BlockSpec((seq_len, W_GRP_R), lambda i: (i, 0)),
                _layer_spec(dec_lane, l), _layer_spec(dec_wide, l), _layer_spec(gn, l)]
    args = [proj_r, dec_lane, dec_wide, gn]
    if rope:
        in_specs += [_const_spec(rope_tabs[0].shape), _const_spec(rope_tabs[1].shape)]
        args += list(rope_tabs)
    if has_s0:
        in_specs.append(pl.BlockSpec((1, 1, 2, npair, HEAD_PAIR, HEAD_PAIR), lambda i: (i, l, 0, 0, 0, 0)))
        args.append(s0)
    out_specs = [pl.BlockSpec((seq_len, W_RET), lambda i: (i, 0))]
    out_shape = [jax.ShapeDtypeStruct((n, W_RET), BF16)]
    aliases = {}
    if emit_state:
        spec, shape, extra_specs, extra_args, aliases = _state_output(
            nseq, (2, H_RET, DK_RET, DK_RET), l, prev_states, len(args))
        out_specs.append(spec)
        out_shape.append(shape)
        in_specs += extra_specs
        args += extra_args
    scratch = [pltpu.VMEM((H_RET, SCAN_CHUNK, SCAN_CHUNK), F32), pltpu.VMEM((4, SCAN_CHUNK, W_RET), F32)]
    if rope:
        scratch.append(pltpu.VMEM((seq_len, 2 * W_RET), F32))
    return pl.pallas_call(
        functools.partial(_ret_body, seq_len=seq_len, rope=rope, has_s0=has_s0, emit_state=emit_state,
                          first_layer=l == 0),
        grid=(nseq,), in_specs=in_specs, out_specs=out_specs, out_shape=out_shape,
        input_output_aliases=aliases, scratch_shapes=scratch, compiler_params=_params(1),
        name="retention_lat" if rope else "retention_ctx",
    )(*args)


def _ssd_body(*refs, seq_len, has_s0, emit_state, first_layer):
    it = iter(refs)
    p_ref = next(it)
    cw_ref = next(it)
    cb_ref = next(it)
    dtb_ref = next(it)
    alog_ref = next(it)
    dskip_ref = next(it)
    gn_ref = next(it)
    s0_ref = st_ref = None
    if has_s0:
        s0_ref = next(it)
    if emit_state and not first_layer:
        next(it)
    y_ref = next(it)
    if emit_state:
        st_ref = next(it)
    xs_ref = next(it)
    bc_ref = next(it)

    T = seq_len
    C = SCAN_CHUNK
    nch = T // C
    pos = lax.broadcasted_iota(jnp.int32, (T, LANES), 0)

    for blk in range(CONV_CH // LANES):
        x = p_ref[:, W_SSD + blk * LANES:W_SSD + (blk + 1) * LANES]
        a = _silu(_dwconv4(x, cw_ref, cb_ref, blk * LANES, pos, T))
        if blk < W_SSD // LANES:
            xs_ref[:, blk * LANES:(blk + 1) * LANES] = a
        else:
            o = blk * LANES - W_SSD
            bc_ref[:, o:o + LANES] = a

    a_neg = -jnp.exp(alog_ref[0])
    ii = lax.broadcasted_iota(jnp.int32, (C, C), 0)
    mm = lax.broadcasted_iota(jnp.int32, (C, C), 1)
    lower = ii >= mm
    upper = mm >= ii
    tri_l = jnp.where(lower, 1.0, 0.0).astype(BF16)
    tri_u = jnp.where(upper, 1.0, 0.0).astype(BF16)
    neg_inf = -jnp.inf
    lane384 = lax.broadcasted_iota(jnp.int32, (C, W_SSD), 1)
    grp0 = lane384 < (W_SSD // G_SSD)
    lane_s = lax.broadcasted_iota(jnp.int32, (N_SSD, W_SSD), 1)
    grp0_s = lane_s < (W_SSD // G_SSD)
    low = lax.broadcasted_iota(jnp.int32, (C, HEAD_PAIR), 1) < P_SSD
    need_states = emit_state or (has_s0 and nch > 1)

    def rows(c):
        return slice(c * C, (c + 1) * C)

    def chunk_scalars(c):
        dt = _softplus(p_ref[rows(c), W_SSD + CONV_CH:W_SSD + CONV_CH + LANES] + dtb_ref[0])
        la = dt * a_neg
        a1 = la.astype(BF16)
        r1 = la - a1.astype(F32)
        a2 = r1.astype(BF16)
        a3 = (r1 - a2.astype(F32)).astype(BF16)
        pre = _dot(tri_l, a1) + _dot(tri_l, a2) + _dot(tri_l, a3)
        suf = _dot(tri_u, a1) + _dot(tri_u, a2) + _dot(tri_u, a3)
        return dt, pre, suf

    ds_f = [None] * nch
    ds_b = [None] * nch
    dec_f = [None] * nch
    dec_b = [None] * nch
    if need_states:
        for c in range(nch):
            dt, pre, suf = chunk_scalars(c)
            wf = jnp.exp(pre[C - 1:C, :] - pre) * dt
            wb = jnp.exp(suf[0:1, :] - suf) * dt
            xs = xs_ref[rows(c), :]
            xf = (xs * _head_lane_expand(wf, 0)).astype(BF16)
            xb = (xs * _head_lane_expand(wb, DT_BWD_LANE)).astype(BF16)
            bt0 = jnp.transpose(bc_ref[rows(c), 0:N_SSD]).astype(BF16)
            bt1 = jnp.transpose(bc_ref[rows(c), N_SSD:2 * N_SSD]).astype(BF16)
            ds_f[c] = jnp.where(grp0_s, _dot(bt0, xf), _dot(bt1, xf))
            ds_b[c] = jnp.where(grp0_s, _dot(bt0, xb), _dot(bt1, xb))
            dec_f[c] = _head_lane_expand(jnp.exp(pre[C - 1:C, :]), 0)
            dec_b[c] = _head_lane_expand(jnp.exp(suf[0:1, :]), DT_BWD_LANE)

    if emit_state:
        if first_layer:
            _zero_later_layers(st_ref)
        for h in range(H_SSD):
            st_ref[0, 0, 0, h] = ds_f[0][:, h * P_SSD:(h + 1) * P_SSD]
            st_ref[0, 0, 1, h] = ds_b[0][:, h * P_SSD:(h + 1) * P_SSD]

    sf_in = [None] * nch
    sb_in = [None] * nch
    if has_s0:
        s = s0_ref[0, 0, 0]
        for c in range(nch):
            sf_in[c] = s
            if c + 1 < nch:
                s = s * dec_f[c] + ds_f[c]
        s = s0_ref[0, 0, 1]
        for c in reversed(range(nch)):
            sb_in[c] = s
            if c > 0:
                s = s * dec_b[c] + ds_b[c]

    lane128 = lax.broadcasted_iota(jnp.int32, (C, LANES), 1)
    grp0_bwd_lane = lane128 < DT_BWD_LANE + H_SSD // G_SSD
    log2e = math.log2(math.e)
    for c in range(nch):
        dt, pre, suf = chunk_scalars(c)
        log2_dt = jnp.log(dt) * log2e
        pre2 = pre * log2e
        suf2 = suf * log2e
        pre_t = jnp.transpose(pre2 - log2_dt)
        suf_t = jnp.transpose(suf2 - log2_dt)
        xs = xs_ref[rows(c), :]
        xs_b = xs.astype(BF16)
        cm = [bc_ref[rows(c), 2 * N_SSD + g * N_SSD:2 * N_SSD + (g + 1) * N_SSD].astype(BF16)
              for g in range(G_SSD)]
        bm = [bc_ref[rows(c), g * N_SSD:(g + 1) * N_SSD].astype(BF16) for g in range(G_SSD)]
        gram = [_dot_nt(cm[g], bm[g]) for g in range(G_SSD)]
        heads = []
        for h in range(H_SSD):
            g = h // (H_SSD // G_SSD)
            hb = DT_BWD_LANE + h
            arg = jnp.where(lower, pre2[:, h:h + 1] - pre_t[h:h + 1, :], suf2[:, hb:hb + 1] - suf_t[hb:hb + 1, :])
            w = gram[g] * jnp.exp2(arg)
            p = h // 2
            heads.append(_dot(w.astype(BF16), xs_b[:, p * HEAD_PAIR:(p + 1) * HEAD_PAIR]))
        o = jnp.concatenate([jnp.where(low, heads[2 * p], heads[2 * p + 1]) for p in range(H_SSD // 2)],
                            axis=1)
        cb = [jnp.sum(cm[g].astype(F32) * bm[g].astype(F32), axis=-1, keepdims=True) for g in range(G_SSD)]
        o = o + _head_lane_expand(jnp.where(grp0_bwd_lane, cb[0], cb[1]) * dt, DT_BWD_LANE) * xs
        if has_s0:
            sf = sf_in[c].astype(BF16)
            sb = sb_in[c].astype(BF16)
            o = o + _head_lane_expand(jnp.exp(pre), 0) * jnp.where(grp0, _dot(cm[0], sf), _dot(cm[1], sf))
            o = o + (_head_lane_expand(jnp.exp(suf), DT_BWD_LANE)
                     * jnp.where(grp0, _dot(cm[0], sb), _dot(cm[1], sb)))
        y = o + dskip_ref[0] * xs
        yz = y * _silu(p_ref[rows(c), 0:W_SSD])
        y_ref[rows(c), :] = (_rms_scale(yz) * gn_ref[0]).astype(BF16)


def _ssd(proj_s, cw, cb, dtb, alog, dskip, gn, seq_len, l, s0=None, prev_states=None):
    n = proj_s.shape[0]
    nseq = n // seq_len
    has_s0 = s0 is not None
    emit_state = not has_s0
    in_specs = [pl.BlockSpec((seq_len, W_GRP_S), lambda i: (i, 0))]
    in_specs += [_layer_spec(a, l) for a in (cw, cb, dtb, alog, dskip, gn)]
    args = [proj_s, cw, cb, dtb, alog, dskip, gn]
    if has_s0:
        in_specs.append(pl.BlockSpec((1, 1, 2, N_SSD, W_SSD), lambda i: (i, l, 0, 0, 0)))
        args.append(s0)
    out_specs = [pl.BlockSpec((seq_len, W_SSD), lambda i: (i, 0))]
    out_shape = [jax.ShapeDtypeStruct((n, W_SSD), BF16)]
    aliases = {}
    if emit_state:
        spec, shape, extra_specs, extra_args, aliases = _state_output(
            nseq, (2, H_SSD, N_SSD, P_SSD), l, prev_states, len(args))
        out_specs.append(spec)
        out_shape.append(shape)
        in_specs += extra_specs
        args += extra_args
    scratch = [pltpu.VMEM((seq_len, W_SSD), F32), pltpu.VMEM((seq_len, 2 * G_SSD * N_SSD), F32)]
    return pl.pallas_call(
        functools.partial(_ssd_body, seq_len=seq_len, has_s0=has_s0, emit_state=emit_state, first_layer=l == 0),
        grid=(nseq,), in_specs=in_specs, out_specs=out_specs, out_shape=out_shape,
        input_output_aliases=aliases, scratch_shapes=scratch, compiler_params=_params(1),
        name="ssd_lat" if has_s0 else "ssd_ctx",
    )(*args)


def _lru_body(*refs, seq_len, has_h0, emit_state, first_layer):
    it = iter(refs)
    p_ref = next(it)
    cw_ref = next(it)
    cb_ref = next(it)
    wg_ref = next(it)
    bg_ref = next(it)
    lam_ref = next(it)
    h0_ref = st_ref = None
    if has_h0:
        h0_ref = next(it)
    if emit_state and not first_layer:
        next(it)
    y_ref = next(it)
    if emit_state:
        st_ref = next(it)

    T = seq_len
    pos = lax.broadcasted_iota(jnp.int32, (T, W_LRU), 0)
    sub = pos & (SUBLANES - 1)
    xc = _dwconv4(p_ref[:, 0:W_LRU], cw_ref, cb_ref, 0, pos, T)
    gates = _sigmoid(_dot(xc.astype(BF16), wg_ref[0]) + bg_ref[0])
    decay_rate = _softplus(-lam_ref[0])
    if emit_state and first_layer:
        _zero_later_layers(st_ref)
    nblk = T // SUBLANES
    total = None
    for d in range(2):
        r = gates[:, 2 * d * W_LRU:(2 * d + 1) * W_LRU]
        i = gates[:, (2 * d + 1) * W_LRU:(2 * d + 2) * W_LRU]
        log_a = (-LRU_C * r) * decay_rate[d:d + 1, :]
        a = jnp.exp(log_a)
        th = jnp.tanh(log_a)
        b = jnp.sqrt(-2.0 * th / (1.0 - th)) * (i * xc)
        step = 1
        while step < SUBLANES:
            if d == 0:
                ok = sub >= step
                shift = step
            else:
                ok = sub <= SUBLANES - 1 - step
                shift = T - step
            a_prev = jnp.where(ok, pltpu.roll(a, shift, 0), 1.0)
            b_prev = jnp.where(ok, pltpu.roll(b, shift, 0), 0.0)
            b = a * b_prev + b
            a = a * a_prev
            step *= 2
        carry = h0_ref[0, 0, d:d + 1, :] if has_h0 else None
        edge = SUBLANES - 1 if d == 0 else 0
        tiles = [None] * nblk
        for v in (range(nblk) if d == 0 else reversed(range(nblk))):
            rows = slice(v * SUBLANES, (v + 1) * SUBLANES)
            hv = b[rows]
            if carry is not None:
                hv = hv + a[rows] * carry
            tiles[v] = hv
            carry = hv[edge:edge + 1, :]
        if emit_state:
            st_ref[0, 0, d:d + 1, :] = carry
        hs = jnp.concatenate(tiles, axis=0)
        total = hs if total is None else total + hs
    y_ref[...] = (total * _gelu_tanh(p_ref[:, W_LRU:2 * W_LRU])).astype(BF16)


def _lru(proj_l, cw, cb, wg, bg, lam, seq_len, l, h0=None, prev_states=None):
    n = proj_l.shape[0]
    nseq = n // seq_len
    has_h0 = h0 is not None
    emit_state = not has_h0
    in_specs = [pl.BlockSpec((seq_len, W_GRP_L), lambda i: (i, 0))]
    in_specs += [_layer_spec(a, l) for a in (cw, cb, wg, bg, lam)]
    args = [proj_l, cw, cb, wg, bg, lam]
    if has_h0:
        in_specs.append(pl.BlockSpec((1, 1, 2, W_LRU), lambda i: (i, l, 0, 0)))
        args.append(h0)
    out_specs = [pl.BlockSpec((seq_len, W_LRU), lambda i: (i, 0))]
    out_shape = [jax.ShapeDtypeStruct((n, W_LRU), BF16)]
    aliases = {}
    if emit_state:
        spec, shape, extra_specs, extra_args, aliases = _state_output(nseq, (2, W_LRU), l, prev_states, len(args))
        out_specs.append(spec)
        out_shape.append(shape)
        in_specs += extra_specs
        args += extra_args
    return pl.pallas_call(
        functools.partial(_lru_body, seq_len=seq_len, has_h0=has_h0, emit_state=emit_state, first_layer=l == 0),
        grid=(nseq,), in_specs=in_specs, out_specs=out_specs, out_shape=out_shape,
        input_output_aliases=aliases, compiler_params=_params(1),
        name="lru_lat" if has_h0 else "lru_ctx",
    )(*args)


def _outproj_body(x_ref, yr_ref, ys_ref, yl_ref, mod_ref, g_ref, wo_ref, x1_ref, h2_ref):
    mix = _dot(yr_ref[...], wo_ref[0, 0:W_RET, :])
    mix = mix + _dot(ys_ref[...], wo_ref[0, W_RET:W_RET + W_SSD, :])
    mix = mix + _dot(yl_ref[...], wo_ref[0, W_RET + W_SSD:D_MODEL, :])
    x1 = x_ref[...] + mod_ref[0, 0, 2:3, :] * mix
    x1_ref[...] = x1
    y = _rms_scale(x1) * g_ref[0]
    h2_ref[...] = (y * (1.0 + mod_ref[0, 0, 4:5, :]) + mod_ref[0, 0, 3:4, :]).astype(BF16)


def _out_proj(x2d, y_ret, y_ssd, y_lru, mods, row0, rows_per_mod, gain, wo, l):
    n = x2d.shape[0]
    tm = ROW_TILE
    row_spec = lambda w: pl.BlockSpec((tm, w), lambda i: (i, 0))
    return pl.pallas_call(
        _outproj_body,
        grid=(n // tm,),
        in_specs=[row_spec(D_MODEL), row_spec(W_RET), row_spec(W_SSD), row_spec(W_LRU),
                  _mod_spec(row0, rows_per_mod, tm, l), _layer_spec(gain, l), _layer_spec(wo, l)],
        out_specs=[row_spec(D_MODEL), row_spec(D_MODEL)],
        out_shape=[jax.ShapeDtypeStruct((n, D_MODEL), F32), jax.ShapeDtypeStruct((n, D_MODEL), BF16)],
        compiler_params=_params(1),
        name="out_proj",
    )(x2d, y_ret, y_ssd, y_lru, mods, gain, wo)


def _ffn_body(h_ref, x1_ref, mod_ref, wup_ref, cw_ref, cb_ref, wd_ref, fg_ref, o_ref, act_ref, *, seq_len,
              final_norm):
    tm = h_ref.shape[0]
    nseq = tm // seq_len
    row8 = lax.broadcasted_iota(jnp.int32, (SUBLANES, FF_BLOCK), 0)
    first_row = row8 == 0
    last_row = row8 == SUBLANES - 1

    def conv3(u, col0):
        cols = slice(col0, col0 + FF_BLOCK)
        w_prev, w_mid, w_next = (cw_ref[0, t:t + 1, cols] for t in range(3))
        bias = cb_ref[0, 0:1, cols]
        outs = []
        for s in range(nseq):
            us = u[s * seq_len:(s + 1) * seq_len]
            prev = pltpu.roll(us, 1, 0)
            nxt = pltpu.roll(us, seq_len - 1, 0)
            prev = jnp.concatenate([jnp.where(first_row, 0.0, prev[:SUBLANES]), prev[SUBLANES:]], axis=0)
            nxt = jnp.concatenate([nxt[:seq_len - SUBLANES], jnp.where(last_row, 0.0, nxt[seq_len - SUBLANES:])],
                                  axis=0)
            outs.append(us * w_mid + bias + prev * w_prev + nxt * w_next)
        return outs[0] if nseq == 1 else jnp.concatenate(outs, axis=0)

    h = h_ref[...]

    def up(j):
        return (_dot(h, wup_ref[0, :, j * FF_BLOCK:(j + 1) * FF_BLOCK]),
                _dot(h, wup_ref[0, :, D_FF + j * FF_BLOCK:D_FF + (j + 1) * FF_BLOCK]))

    for j in range(N_FF_BLOCKS):
        uv, ug = up(j)
        val = conv3(uv, j * FF_BLOCK)
        gate = conv3(ug, D_FF + j * FF_BLOCK)
        act_ref[:, j * FF_BLOCK:(j + 1) * FF_BLOCK] = (_silu(gate) * val).astype(BF16)
    out = x1_ref[...] + mod_ref[0, 0, 5:6, :] * _dot(act_ref[...], wd_ref[0])
    if final_norm:
        out = _rms_scale(out) * fg_ref[...]
    o_ref[...] = out


def _ffn(h2, x1, mods, row0, rows_per_mod, wup, cw, cb, wd, fgain, seq_len, final_norm, l):
    n = h2.shape[0]
    tm = FFN_ROW_TILE
    row_spec = pl.BlockSpec((tm, D_MODEL), lambda i: (i, 0))
    return pl.pallas_call(
        functools.partial(_ffn_body, seq_len=seq_len, final_norm=final_norm),
        grid=(n // tm,),
        in_specs=[row_spec, row_spec, _mod_spec(row0, rows_per_mod, tm, l),
                  _layer_spec(wup, l), _layer_spec(cw, l), _layer_spec(cb, l), _layer_spec(wd, l),
                  _const_spec(fgain.shape)],
        out_specs=row_spec,
        out_shape=jax.ShapeDtypeStruct((n, D_MODEL), F32),
        scratch_shapes=[pltpu.VMEM((tm, D_FF), BF16)],
        compiler_params=_params(1),
        name="ffn",
    )(h2, x1, mods, wup, cw, cb, wd, fgain)


def _rope_tables(t_len):
    tok = jnp.arange(t_len)
    row_pos = (tok // GRID_W).astype(F32)
    col_pos = (tok % GRID_W).astype(F32)
    lane = jnp.arange(HEAD_PAIR)
    d = lane % DK_RET
    use_row = d < (DK_RET // 2)
    half = DK_RET // 4
    freqs = ROPE_BASE ** (-(d % half).astype(F32) / half)
    ang = jnp.where(use_row[None, :], row_pos[:, None], col_pos[:, None]) * freqs[None, :]
    first = (d % (2 * half)) < half
    return jnp.cos(ang), jnp.where(first[None, :], -jnp.sin(ang), jnp.sin(ang))


def _prep_params(w_in, ret_decay, ret_norm_g, ssd_conv_b, ssd_dt_bias, ssd_a_log, ssd_d, ssd_norm_g,
                 lru_conv_b, lru_w_a, lru_b_a, lru_w_x, lru_b_x, w_out, ffn_w_up, ffn_conv_b, ffn_w_down):
    o_z = 4 * W_RET
    o_dt = o_z + W_SSD + CONV_CH
    o_xl = o_dt + H_SSD
    dt_cols = w_in[..., o_dt:o_dt + H_SSD]
    gap = jnp.zeros((DEPTH, D_MODEL, DT_BWD_LANE - H_SSD), F32)
    tail = jnp.zeros((DEPTH, D_MODEL, LANES - DT_BWD_LANE - H_SSD), F32)
    w_all = jnp.concatenate([w_in[..., :W_RET] * (DK_RET ** -0.5), w_in[..., W_RET:o_dt],
                             dt_cols, gap, dt_cols, tail, w_in[..., o_xl:o_xl + W_GRP_L]], axis=-1).astype(BF16)

    def lane_place(pair):
        z = jnp.zeros((DEPTH, LANES), F32)
        return jnp.concatenate([pair[:, 0], z[:, :DT_BWD_LANE - H_SSD], pair[:, 1],
                                z[:, :LANES - DT_BWD_LANE - H_SSD]], axis=-1)[:, None, :]

    eye = jnp.eye(LRU_BLOCKS, dtype=F32)
    gates = jnp.stack([lru_w_a, lru_w_x], axis=2)
    wgate = jnp.einsum("lxgncd,nm->lncxgmd", gates, eye).reshape(DEPTH, W_LRU, 4 * W_LRU).astype(BF16)
    bgate = jnp.stack([lru_b_a, lru_b_x], axis=2).reshape(DEPTH, 1, 4 * W_LRU)
    return dict(
        w_all=w_all,
        dec_lane=jnp.repeat(ret_decay, DK_RET, axis=-1),
        dec_wide=jnp.broadcast_to(ret_decay.reshape(DEPTH, 2 * H_RET, 1), (DEPTH, 2 * H_RET, SCAN_CHUNK)),
        ret_gn=ret_norm_g[:, None, :],
        ssd_cb=ssd_conv_b[:, None, :],
        ssd_dtb=lane_place(ssd_dt_bias), ssd_alog=lane_place(ssd_a_log),
        ssd_dskip=jnp.repeat(ssd_d, P_SSD, axis=-1)[:, None, :], ssd_gn=ssd_norm_g[:, None, :],
        lru_cb=lru_conv_b[:, None, :], lru_wg=wgate, lru_bg=bgate,
        wo=w_out.astype(BF16), wup=ffn_w_up.astype(BF16), ffn_cb=ffn_conv_b[:, None, :],
        wd=ffn_w_down.astype(BF16),
    )


def kernel(x_prompt, x_sample, c, c_ctx, state_ret, state_ssd, state_lru, w_ada, b_ada, norm1_g, norm2_g, w_in, ret_decay, ret_norm_g, ssd_conv_w, ssd_conv_b, ssd_dt_bias, ssd_a_log, ssd_d, ssd_norm_g, lru_conv_w, lru_conv_b, lru_w_a, lru_b_a, lru_w_x, lru_b_x, lru_lambda, w_out, ffn_w_up, ffn_conv_w, ffn_conv_b, ffn_w_down, final_norm_g):
    bp, tp, _ = x_prompt.shape
    bs, ts, _ = x_sample.shape
    assert tp == SCAN_CHUNK and ts % SCAN_CHUNK == 0 and ts % GRID_W == 0

    n_req = 8
    cvec = jnp.concatenate([c_ctx[None, :], c, jnp.zeros((n_req - 1 - bs, D_MODEL), F32)], axis=0)
    mods = _modulation(cvec, w_ada, b_ada).reshape(DEPTH, n_req, 6, D_MODEL)

    pp = _prep_params(w_in, ret_decay, ret_norm_g, ssd_conv_b, ssd_dt_bias, ssd_a_log, ssd_d, ssd_norm_g,
                      lru_conv_b, lru_w_a, lru_b_a, lru_w_x, lru_b_x, w_out, ffn_w_up, ffn_conv_b, ffn_w_down)
    n1 = norm1_g[:, None, :]
    n2 = norm2_g[:, None, :]
    rope_tabs = _rope_tables(ts)
    fgain = final_norm_g[None, :]

    npair = H_RET // 2
    sr = state_ret.reshape(bs, DEPTH, 2, npair, 2, DK_RET, DK_RET)
    s0_ret = jnp.einsum("bldparc,ax->bldparxc", sr, jnp.eye(2, dtype=F32)).reshape(
        bs, DEPTH, 2, npair, HEAD_PAIR, HEAD_PAIR)
    s0_ssd = state_ssd.transpose(0, 1, 2, 4, 3, 5).reshape(bs, DEPTH, 2, N_SSD, W_SSD)

    def trunk_layer(x2d, l, seq_len, row0, rows_per_mod, latent, prev=(None, None, None)):
        pr, ps, plru = _in_proj(x2d, mods, row0, rows_per_mod, n1, pp["w_all"], l)
        ret = _retention(pr, pp["dec_lane"], pp["dec_wide"], pp["ret_gn"], seq_len, l,
                         rope_tabs if latent else None, s0_ret if latent else None, prev[0])
        ssd = _ssd(ps, ssd_conv_w, pp["ssd_cb"], pp["ssd_dtb"], pp["ssd_alog"], pp["ssd_dskip"], pp["ssd_gn"],
                   seq_len, l, s0_ssd if latent else None, prev[1])
        lru = _lru(plru, lru_conv_w, pp["lru_cb"], pp["lru_wg"], pp["lru_bg"], lru_lambda, seq_len, l,
                   state_lru if latent else None, prev[2])
        x1, h2 = _out_proj(x2d, ret[0], ssd[0], lru[0], mods, row0, rows_per_mod, n2, pp["wo"], l)
        out = _ffn(h2, x1, mods, row0, rows_per_mod, pp["wup"], ffn_conv_w, pp["ffn_cb"], pp["wd"], fgain,
                   seq_len, l == DEPTH - 1, l)
        states = None if latent else (ret[1], ssd[1], lru[1])
        return out, states

    xp = x_prompt.reshape(bp * tp, D_MODEL)
    xs = x_sample.reshape(bs * ts, D_MODEL)
    states = (None, None, None)
    for l in range(DEPTH):
        xp, states = trunk_layer(xp, l, tp, 0, bp * tp, False, states)
        xs, _ = trunk_layer(xs, l, ts, 1, ts, True)

    return (xp.reshape(bp, tp, D_MODEL), xs.reshape(bs, ts, D_MODEL)) + tuple(states)
```

```python
import functools
import math

import jax
import jax.numpy as jnp
from jax import lax
from jax.experimental import pallas as pl
from jax.experimental.pallas import tpu as pltpu

F32 = jnp.float32
BF16 = jnp.bfloat16

D_MODEL = 1024
DEPTH = 2
GRID_W = 64
W_RET = 384
H_RET = 6
DK_RET = 64
W_SSD = 384
P_SSD = 64
H_SSD = 6
N_SSD = 128
G_SSD = 2
CONV_CH = W_SSD + 2 * G_SSD * N_SSD
W_LRU = 256
LRU_BLOCKS = 4
LRU_BW = 64
LRU_C = 8.0
D_FF = 2816
ROPE_BASE = 10000.0
EPS = 1e-6

LANES = 128
SUBLANES = 8
HEAD_PAIR = LANES
SCAN_CHUNK = 256
FF_BLOCK = 256
N_FF_BLOCKS = D_FF // FF_BLOCK
ROW_TILE = 512
SEQ_TILE = 1024
PROJ_COL_BLOCK = 512
VMEM_LIMIT = 56 * 1024 * 1024

W_GRP_R = 4 * W_RET
W_GRP_S = W_SSD + CONV_CH + LANES
W_GRP_L = 2 * W_LRU
W_SLOT_S = W_GRP_R
COL_R, COL_S, COL_L = 0, W_GRP_R, W_GRP_R + W_SLOT_S
W_IN_PAD = COL_L + W_GRP_L
DT_BWD_LANE = 8


def _dot(a, b):
    return jnp.dot(a, b, preferred_element_type=F32)


def _dot_nt(a, b):
    return lax.dot_general(a, b, (((1,), (1,)), ((), ())), preferred_element_type=F32)


def _sigmoid(x):
    return 1.0 / (1.0 + jnp.exp(-x))


def _silu(x):
    return x * _sigmoid(x)


def _softplus(x):
    return jnp.maximum(x, 0.0) + jnp.log1p(jnp.exp(-jnp.abs(x)))


def _log_sigmoid(x):
    return -_softplus(-x)


def _gelu_tanh(x):
    c = math.sqrt(2.0 / math.pi)
    return 0.5 * x * (1.0 + jnp.tanh(c * (x + 0.044715 * (x * x * x))))


def _rms_scale(x):
    return x * lax.rsqrt(jnp.mean(x * x, axis=-1, keepdims=True) + EPS)


def _const_spec(shape):
    zeros = (0,) * len(shape)
    return pl.BlockSpec(shape, lambda *_: zeros, pipeline_mode=pl.Buffered(1))


def _layer_spec(arr, l):
    rest = (0,) * (arr.ndim - 1)
    return pl.BlockSpec((1,) + arr.shape[1:], lambda *_: (l,) + rest, pipeline_mode=pl.Buffered(1))


def _w_in_spec(l, col0, width):
    assert col0 % width == 0
    return pl.BlockSpec((1, D_MODEL, width), lambda *_: (l, 0, col0 // width), pipeline_mode=pl.Buffered(1))


def _mod_spec(row0, rows_per_mod, tm, l):
    return pl.BlockSpec((1, 1, 6, D_MODEL), lambda i: (l, row0 + i * tm // rows_per_mod, 0, 0))


def _params(n_axes):
    return pltpu.CompilerParams(dimension_semantics=("arbitrary",) * n_axes,
                                vmem_limit_bytes=VMEM_LIMIT)


def _mod_body(c_ref, w_ref, b_ref, o_ref):
    s = _silu(c_ref[...]).astype(BF16)
    o_ref[0] = _dot(s, w_ref[0].astype(BF16)) + b_ref[0]


def _modulation(cvec, w_ada, b_ada):
    rows = cvec.shape[0]
    nblk = (6 * D_MODEL) // D_MODEL
    return pl.pallas_call(
        _mod_body,
        grid=(DEPTH, nblk),
        in_specs=[pl.BlockSpec((rows, D_MODEL), lambda l, j: (0, 0)),
                  pl.BlockSpec((1, D_MODEL, D_MODEL), lambda l, j: (l, 0, j)),
                  pl.BlockSpec((1, 1, D_MODEL), lambda l, j: (l, 0, j))],
        out_specs=pl.BlockSpec((1, rows, D_MODEL), lambda l, j: (l, 0, j)),
        out_shape=jax.ShapeDtypeStruct((DEPTH, rows, 6 * D_MODEL), F32),
        compiler_params=_params(2),
        name="adaln_mod",
    )(cvec, w_ada, b_ada.reshape(DEPTH, 1, 6 * D_MODEL))


def _project(x_ref, mod_ref, g_ref, w_ref, p_ref):
    y = _rms_scale(x_ref[...]) * g_ref[0]
    h = (y * (1.0 + mod_ref[0, 0, 1:2, :]) + mod_ref[0, 0, 0:1, :]).astype(BF16)
    width = p_ref.shape[1]
    for j in range(0, width, PROJ_COL_BLOCK):
        jb = min(PROJ_COL_BLOCK, width - j)
        p_ref[:, j:j + jb] = _dot(h, w_ref[0, :, j:j + jb])


def _shift_rows(x, k, pos, seq_len):
    n = x.shape[0]
    if k == 0:
        return x
    rolled = pltpu.roll(x, (-k) % n, 0)
    ok = (pos >= -k) if k < 0 else (pos <= seq_len - 1 - k)
    return jnp.where(ok, rolled, 0.0)


def _dwconv4(x, w_ref, b_ref, col0, pos, seq_len):
    width = x.shape[1]
    cols = slice(col0, col0 + width)
    acc = x * w_ref[0, 1:2, cols] + b_ref[0, 0:1, cols]
    acc = acc + _shift_rows(x, -1, pos, seq_len) * w_ref[0, 0:1, cols]
    acc = acc + _shift_rows(x, 1, pos, seq_len) * w_ref[0, 2:3, cols]
    acc = acc + _shift_rows(x, 2, pos, seq_len) * w_ref[0, 3:4, cols]
    return acc


def _zero_later_layers(st_ref):
    st_ref[:, 1:] = jnp.zeros((st_ref.shape[0], DEPTH - 1) + tuple(st_ref.shape[2:]), F32)


def _state_output(nseq_total, nseq_tile, tail, l, prev, n_inputs):
    zeros = (0,) * len(tail)
    shape = jax.ShapeDtypeStruct((nseq_total, DEPTH) + tail, F32)
    if l == 0:
        return pl.BlockSpec((nseq_tile, DEPTH) + tail, lambda i: (i, 0) + zeros), shape, [], [], {}
    spec = pl.BlockSpec((nseq_tile, 1) + tail, lambda i: (i, l) + zeros)
    return spec, shape, [pl.BlockSpec(memory_space=pl.ANY)], [prev], {n_inputs: 1}


def _head_lane_expand(x, lane0):
    rows = x.shape[0]
    low = lax.broadcasted_iota(jnp.int32, (rows, HEAD_PAIR), 1) < DK_RET
    blocks = []
    for p in range(H_RET // 2):
        a = jnp.broadcast_to(x[:, lane0 + 2 * p:lane0 + 2 * p + 1], (rows, HEAD_PAIR))
        b = jnp.broadcast_to(x[:, lane0 + 2 * p + 1:lane0 + 2 * p + 2], (rows, HEAD_PAIR))
        blocks.append(jnp.where(low, a, b))
    return jnp.concatenate(blocks, axis=1)


def _seq_call(body, name, x2d, mods, row0, rows_per_mod, gain, w_all, col0, width, l, seq_len, params,
              extra_in, y_width, state_tail, prev_states, scratch):
    n = x2d.shape[0]
    tm = SEQ_TILE
    nseq_tile = tm // seq_len
    emit_state = state_tail is not None
    in_specs = [pl.BlockSpec((tm, D_MODEL), lambda i: (i, 0)), _mod_spec(row0, rows_per_mod, tm, l),
                _layer_spec(gain, l), _w_in_spec(l, col0, width)]
    in_specs += [_layer_spec(a, l) for a in params]
    in_specs += [spec for _, spec in extra_in]
    args = [x2d, mods, gain, w_all] + list(params) + [a for a, _ in extra_in]
    out_specs = [pl.BlockSpec((tm, y_width), lambda i: (i, 0))]
    out_shape = [jax.ShapeDtypeStruct((n, y_width), BF16)]
    aliases = {}
    if emit_state:
        spec, shape, extra_specs, extra_args, aliases = _state_output(
            n // seq_len, nseq_tile, state_tail, l, prev_states, len(args))
        out_specs.append(spec)
        out_shape.append(shape)
        in_specs += extra_specs
        args += extra_args
    return pl.pallas_call(
        functools.partial(body, seq_len=seq_len, emit_state=emit_state, first_layer=l == 0),
        grid=(n // tm,), in_specs=in_specs, out_specs=out_specs, out_shape=out_shape,
        input_output_aliases=aliases, scratch_shapes=scratch, compiler_params=_params(1), name=name,
    )(*args)


def _ret_body(*refs, seq_len, rope, has_s0, emit_state, first_layer):
    it = iter(refs)
    x_ref, mod_ref, g_ref, w_ref = next(it), next(it), next(it), next(it)
    dec_lane_ref = next(it)
    dec_wide_ref = next(it)
    gn_ref = next(it)
    cos_ref = sin_ref = s0_ref = st_ref = qk_ref = None
    if rope:
        cos_ref = next(it)
        sin_ref = next(it)
    if has_s0:
        s0_ref = next(it)
    if emit_state and not first_layer:
        next(it)
    y_ref = next(it)
    if emit_state:
        st_ref = next(it)
    p_ref = next(it)
    dmat_ref = next(it)
    wts_ref = next(it)
    if rope:
        qk_ref = next(it)

    C = SCAN_CHUNK
    nch = seq_len // C
    nseq = p_ref.shape[0] // seq_len
    npair = H_RET // 2
    assert not (has_s0 or rope) or nseq == 1
    _project(x_ref, mod_ref, g_ref, w_ref, p_ref)
    lgl = _log_sigmoid(dec_lane_ref[0])

    @pl.when(pl.program_id(0) == 0)
    def _():
        lgw = _log_sigmoid(dec_wide_ref[0])
        ii = lax.broadcasted_iota(jnp.int32, (C, C), 0)
        mm = lax.broadcasted_iota(jnp.int32, (C, C), 1)
        dif = (ii - mm).astype(F32)
        neg_inf = -jnp.inf
        for h in range(H_RET):
            df = jnp.exp(jnp.where(ii >= mm, dif * lgw[h:h + 1, :], neg_inf))
            db = jnp.exp(jnp.where(mm >= ii, (-dif) * lgw[H_RET + h:H_RET + h + 1, :], neg_inf))
            dmat_ref[h] = df + db
        ri = lax.broadcasted_iota(jnp.int32, (C, W_RET), 0).astype(F32)
        wts_ref[0] = jnp.exp((C - 1.0 - ri) * lgl[0:1, :])
        wts_ref[1] = jnp.exp(ri * lgl[1:2, :])
        wts_ref[2] = jnp.exp((ri + 1.0) * lgl[0:1, :])
        wts_ref[3] = jnp.exp((C - ri) * lgl[1:2, :])

    lane = lax.broadcasted_iota(jnp.int32, (C, HEAD_PAIR), 1)
    low = lane < DK_RET
    r128 = lax.broadcasted_iota(jnp.int32, (HEAD_PAIR, HEAD_PAIR), 0) >= DK_RET
    c128 = lax.broadcasted_iota(jnp.int32, (HEAD_PAIR, HEAD_PAIR), 1) >= DK_RET
    same_head = r128 == c128
    avg = jnp.where(same_head, 1.0 / DK_RET, 0.0).astype(BF16)

    need_states = emit_state or (has_s0 and nch > 1)
    if has_s0:
        dec_f = jnp.exp(float(C) * lgl[0:1, :])
        dec_b = jnp.exp(float(C) * lgl[1:2, :])

    if rope:
        swap_low = (lax.broadcasted_iota(jnp.int32, (seq_len, HEAD_PAIR), 1) % 32) < 16
        cos = cos_ref[...]
        sin = sin_ref[...]
        for blk in range(2 * npair):
            cols_b = slice(blk * HEAD_PAIR, (blk + 1) * HEAD_PAIR)
            x = p_ref[:, cols_b]
            swapped = jnp.where(swap_low, pltpu.roll(x, HEAD_PAIR - 16, 1), pltpu.roll(x, 16, 1))
            qk_ref[:, cols_b] = x * cos + swapped * sin

    def cols(p):
        return slice(p * HEAD_PAIR, (p + 1) * HEAD_PAIR)

    if emit_state and first_layer:
        _zero_later_layers(st_ref)

    for s in range(nseq):
        def rows(c, s=s):
            return slice(s * seq_len + c * C, s * seq_len + (c + 1) * C)

        def get_q(c, p):
            src = qk_ref if rope else p_ref
            return src[rows(c), p * HEAD_PAIR:(p + 1) * HEAD_PAIR]

        def get_k(c, p):
            src = qk_ref if rope else p_ref
            return src[rows(c), W_RET + p * HEAD_PAIR:W_RET + (p + 1) * HEAD_PAIR]

        def get_v(c, p):
            return p_ref[rows(c), 2 * W_RET + p * HEAD_PAIR:2 * W_RET + (p + 1) * HEAD_PAIR]

        ds_f = [[None] * npair for _ in range(nch)]
        ds_b = [[None] * npair for _ in range(nch)]
        if need_states:
            for c in range(nch):
                for p in range(npair):
                    k = get_k(c, p)
                    vb = get_v(c, p).astype(BF16)
                    kf = jnp.transpose(k * wts_ref[0, :, cols(p)]).astype(BF16)
                    kb = jnp.transpose(k * wts_ref[1, :, cols(p)]).astype(BF16)
                    ds_f[c][p] = _dot(kf, vb)
                    ds_b[c][p] = _dot(kb, vb)

        if emit_state:
            for p in range(npair):
                for a in range(2):
                    blk = slice(a * DK_RET, (a + 1) * DK_RET)
                    st_ref[s, 0, 0, 2 * p + a] = ds_f[0][p][blk, blk]
                    st_ref[s, 0, 1, 2 * p + a] = ds_b[0][p][blk, blk]

        sf_in = [[None] * npair for _ in range(nch)]
        sb_in = [[None] * npair for _ in range(nch)]
        if has_s0:
            for p in range(npair):
                st = s0_ref[0, 0, 0, p]
                for c in range(nch):
                    sf_in[c][p] = st
                    if c + 1 < nch:
                        st = st * dec_f[:, cols(p)] + jnp.where(same_head, ds_f[c][p], 0.0)
                st = s0_ref[0, 0, 1, p]
                for c in reversed(range(nch)):
                    sb_in[c][p] = st
                    if c > 0:
                        st = st * dec_b[:, cols(p)] + jnp.where(same_head, ds_b[c][p], 0.0)

        for c in range(nch):
            for p in range(npair):
                q = get_q(c, p)
                kb = get_k(c, p).astype(BF16)
                vb = get_v(c, p).astype(BF16)
                q0 = jnp.where(low, q, 0.0).astype(BF16)
                q1 = jnp.where(low, 0.0, q).astype(BF16)
                o0 = _dot((_dot_nt(q0, kb) * dmat_ref[2 * p]).astype(BF16), vb)
                o1 = _dot((_dot_nt(q1, kb) * dmat_ref[2 * p + 1]).astype(BF16), vb)
                o = jnp.where(low, o0, o1)
                if has_s0:
                    o = o + _dot((q * wts_ref[2, :, cols(p)]).astype(BF16), sf_in[c][p].astype(BF16))
                    o = o + _dot((q * wts_ref[3, :, cols(p)]).astype(BF16), sb_in[c][p].astype(BF16))
                mu = _dot(o.astype(BF16), avg)
                d = o - mu
                var = _dot((d * d).astype(BF16), avg)
                on = d * lax.rsqrt(var + EPS)
                g = p_ref[rows(c), 3 * W_RET + p * HEAD_PAIR:3 * W_RET + (p + 1) * HEAD_PAIR]
                y_ref[rows(c), cols(p)] = (on * gn_ref[0, 0:1, cols(p)] * _silu(g)).astype(BF16)


def _retention(x2d, mods, row0, rows_per_mod, gain, w_all, dec_lane, dec_wide, gn, seq_len, l,
               rope_tabs=None, s0=None, prev_states=None):
    rope = rope_tabs is not None
    has_s0 = s0 is not None
    npair = H_RET // 2
    extra = []
    if rope:
        extra += [(t, _const_spec(t.shape)) for t in rope_tabs]
    if has_s0:
        extra.append((s0, pl.BlockSpec((1, 1, 2, npair, HEAD_PAIR, HEAD_PAIR), lambda i: (i, l, 0, 0, 0, 0))))
    scratch = [pltpu.VMEM((SEQ_TILE, W_GRP_R), F32),
               pltpu.VMEM((H_RET, SCAN_CHUNK, SCAN_CHUNK), F32), pltpu.VMEM((4, SCAN_CHUNK, W_RET), F32)]
    if rope:
        scratch.append(pltpu.VMEM((seq_len, 2 * W_RET), F32))
    return _seq_call(functools.partial(_ret_body, rope=rope, has_s0=has_s0),
                     "retention_lat" if rope else "retention_ctx",
                     x2d, mods, row0, rows_per_mod, gain, w_all, COL_R, W_GRP_R, l, seq_len,
                     (dec_lane, dec_wide, gn), extra, W_RET,
                     None if has_s0 else (2, H_RET, DK_RET, DK_RET), prev_states, scratch)


def _ssd_body(*refs, seq_len, has_s0, emit_state, first_layer):
    it = iter(refs)
    x_ref, mod_ref, g_ref, w_ref = next(it), next(it), next(it), next(it)
    cw_ref = next(it)
    cb_ref = next(it)
    dtb_ref = next(it)
    alog_ref = next(it)
    dskip_ref = next(it)
    gn_ref = next(it)
    s0_ref = st_ref = None
    if has_s0:
        s0_ref = next(it)
    if emit_state and not first_layer:
        next(it)
    y_ref = next(it)
    if emit_state:
        st_ref = next(it)
    p_ref = next(it)
    xs_ref = next(it)
    bc_ref = next(it)

    T = seq_len
    C = SCAN_CHUNK
    nch = T // C
    n_rows = p_ref.shape[0]
    nseq = n_rows // T
    assert not has_s0 or nseq == 1
    assert T & (T - 1) == 0
    _project(x_ref, mod_ref, g_ref, w_ref, p_ref)
    pos = lax.broadcasted_iota(jnp.int32, (n_rows, LANES), 0) & (T - 1)

    for blk in range(CONV_CH // LANES):
        x = p_ref[:, W_SSD + blk * LANES:W_SSD + (blk + 1) * LANES]
        a = _silu(_dwconv4(x, cw_ref, cb_ref, blk * LANES, pos, T))
        if blk < W_SSD // LANES:
            xs_ref[:, blk * LANES:(blk + 1) * LANES] = a
        else:
            o = blk * LANES - W_SSD
            bc_ref[:, o:o + LANES] = a

    a_neg = -jnp.exp(alog_ref[0])
    ii = lax.broadcasted_iota(jnp.int32, (C, C), 0)
    mm = lax.broadcasted_iota(jnp.int32, (C, C), 1)
    lower = ii >= mm
    upper = mm >= ii
    tri_l = jnp.where(lower, 1.0, 0.0).astype(BF16)
    tri_u = jnp.where(upper, 1.0, 0.0).astype(BF16)
    neg_inf = -jnp.inf
    grp0 = lax.broadcasted_iota(jnp.int32, (C, W_SSD), 1) < (W_SSD // G_SSD)
    grp0_s = lax.broadcasted_iota(jnp.int32, (N_SSD, W_SSD), 1) < (W_SSD // G_SSD)
    grp0_t = lax.broadcasted_iota(jnp.int32, (W_SSD, N_SSD), 0) < (W_SSD // G_SSD)
    low = lax.broadcasted_iota(jnp.int32, (C, HEAD_PAIR), 1) < P_SSD

    if emit_state and first_layer:
        _zero_later_layers(st_ref)

    for s in range(nseq):
        def rows(c, s=s):
            return slice(s * T + c * C, s * T + (c + 1) * C)

        def chunk_scalars(c):
            dt = _softplus(p_ref[rows(c), W_SSD + CONV_CH:W_SSD + CONV_CH + LANES] + dtb_ref[0])
            la = dt * a_neg
            a1 = la.astype(BF16)
            r1 = la - a1.astype(F32)
            a2 = r1.astype(BF16)
            a3 = (r1 - a2.astype(F32)).astype(BF16)
            pre = _dot(tri_l, a1) + _dot(tri_l, a2) + _dot(tri_l, a3)
            suf = _dot(tri_u, a1) + _dot(tri_u, a2) + _dot(tri_u, a3)
            return dt, pre, suf

        def state_weights(c):
            dt, pre, suf = chunk_scalars(c)
            wf = jnp.exp(pre[C - 1:C, :] - pre) * dt
            wb = jnp.exp(suf[0:1, :] - suf) * dt
            xs = xs_ref[rows(c), :]
            return (xs * _head_lane_expand(wf, 0), xs * _head_lane_expand(wb, DT_BWD_LANE), pre, suf)

        if emit_state:
            xf, xb, _, _ = state_weights(0)
            bm = [bc_ref[rows(0), g * N_SSD:(g + 1) * N_SSD].astype(BF16) for g in range(G_SSD)]
            for d, xw in enumerate((xf, xb)):
                xt = jnp.transpose(xw).astype(BF16)
                st = jnp.where(grp0_t, _dot(xt, bm[0]), _dot(xt, bm[1]))
                for h in range(H_SSD):
                    st_ref[s, 0, d, h] = st[h * P_SSD:(h + 1) * P_SSD, :]

        ds_f = [None] * nch
        ds_b = [None] * nch
        dec_f = [None] * nch
        dec_b = [None] * nch
        if has_s0 and nch > 1:
            for c in range(nch):
                xf, xb, pre, suf = state_weights(c)
                bt0 = jnp.transpose(bc_ref[rows(c), 0:N_SSD]).astype(BF16)
                bt1 = jnp.transpose(bc_ref[rows(c), N_SSD:2 * N_SSD]).astype(BF16)
                ds_f[c] = jnp.where(grp0_s, _dot(bt0, xf.astype(BF16)), _dot(bt1, xf.astype(BF16)))
                ds_b[c] = jnp.where(grp0_s, _dot(bt0, xb.astype(BF16)), _dot(bt1, xb.astype(BF16)))
                dec_f[c] = _head_lane_expand(jnp.exp(pre[C - 1:C, :]), 0)
                dec_b[c] = _head_lane_expand(jnp.exp(suf[0:1, :]), DT_BWD_LANE)

        sf_in = [None] * nch
        sb_in = [None] * nch
        if has_s0:
            st = s0_ref[0, 0, 0]
            for c in range(nch):
                sf_in[c] = st
                if c + 1 < nch:
                    st = st * dec_f[c] + ds_f[c]
            st = s0_ref[0, 0, 1]
            for c in reversed(range(nch)):
                sb_in[c] = st
                if c > 0:
                    st = st * dec_b[c] + ds_b[c]

        for c in range(nch):
            dt, pre, suf = chunk_scalars(c)
            pre_t = jnp.transpose(pre)
            suf_t = jnp.transpose(suf)
            dt_t = jnp.transpose(dt)
            xs = xs_ref[rows(c), :]
            xs_b = xs.astype(BF16)
            cm = [bc_ref[rows(c), 2 * N_SSD + g * N_SSD:2 * N_SSD + (g + 1) * N_SSD].astype(BF16)
                  for g in range(G_SSD)]
            bm = [bc_ref[rows(c), g * N_SSD:(g + 1) * N_SSD].astype(BF16) for g in range(G_SSD)]
            gram = [_dot_nt(cm[g], bm[g]) for g in range(G_SSD)]
            heads = []
            for h in range(H_SSD):
                g = h // (H_SSD // G_SSD)
                hb = DT_BWD_LANE + h
                df = jnp.exp(jnp.where(lower, pre[:, h:h + 1] - pre_t[h:h + 1, :], neg_inf))
                db = jnp.exp(jnp.where(upper, suf[:, hb:hb + 1] - suf_t[hb:hb + 1, :], neg_inf))
                w = gram[g] * (df * dt_t[h:h + 1, :] + db * dt_t[hb:hb + 1, :])
                p = h // 2
                heads.append(_dot(w.astype(BF16), xs_b[:, p * HEAD_PAIR:(p + 1) * HEAD_PAIR]))
            o = jnp.concatenate([jnp.where(low, heads[2 * p], heads[2 * p + 1]) for p in range(H_SSD // 2)],
                                axis=1)
            if has_s0:
                sf = sf_in[c].astype(BF16)
                sb = sb_in[c].astype(BF16)
                o = o + _head_lane_expand(jnp.exp(pre), 0) * jnp.where(grp0, _dot(cm[0], sf), _dot(cm[1], sf))
                o = o + (_head_lane_expand(jnp.exp(suf), DT_BWD_LANE)
                         * jnp.where(grp0, _dot(cm[0], sb), _dot(cm[1], sb)))
            y = o + dskip_ref[0] * xs
            yz = y * _silu(p_ref[rows(c), 0:W_SSD])
            y_ref[rows(c), :] = (_rms_scale(yz) * gn_ref[0]).astype(BF16)


def _ssd(x2d, mods, row0, rows_per_mod, gain, w_all, cw, cb, dtb, alog, dskip, gn, seq_len, l,
         s0=None, prev_states=None):
    has_s0 = s0 is not None
    extra = []
    if has_s0:
        extra.append((s0, pl.BlockSpec((1, 1, 2, N_SSD, W_SSD), lambda i: (i, l, 0, 0, 0))))
    scratch = [pltpu.VMEM((SEQ_TILE, W_GRP_S), F32), pltpu.VMEM((SEQ_TILE, W_SSD), F32),
               pltpu.VMEM((SEQ_TILE, 2 * G_SSD * N_SSD), F32)]
    return _seq_call(functools.partial(_ssd_body, has_s0=has_s0), "ssd_lat" if has_s0 else "ssd_ctx",
                     x2d, mods, row0, rows_per_mod, gain, w_all, COL_S, W_SLOT_S, l, seq_len,
                     (cw, cb, dtb, alog, dskip, gn), extra, W_SSD,
                     None if has_s0 else (2, H_SSD, P_SSD, N_SSD), prev_states, scratch)


def _lru_body(*refs, seq_len, has_h0, emit_state, first_layer):
    it = iter(refs)
    x_ref, mod_ref, g_ref, w_ref = next(it), next(it), next(it), next(it)
    cw_ref = next(it)
    cb_ref = next(it)
    wg_ref = next(it)
    bg_ref = next(it)
    lam_ref = next(it)
    h0_ref = st_ref = None
    if has_h0:
        h0_ref = next(it)
    if emit_state and not first_layer:
        next(it)
    y_ref = next(it)
    if emit_state:
        st_ref = next(it)
    p_ref = next(it)

    T = seq_len
    n_rows = p_ref.shape[0]
    nseq = n_rows // T
    assert not has_h0 or nseq == 1
    assert T & (T - 1) == 0
    _project(x_ref, mod_ref, g_ref, w_ref, p_ref)
    row = lax.broadcasted_iota(jnp.int32, (n_rows, W_LRU), 0)
    pos = row & (T - 1)
    sub = row & (SUBLANES - 1)
    xc = _dwconv4(p_ref[:, 0:W_LRU], cw_ref, cb_ref, 0, pos, T)
    gates = _sigmoid(_dot(xc.astype(BF16), wg_ref[0]) + bg_ref[0])
    decay_rate = _softplus(-lam_ref[0])
    if emit_state and first_layer:
        _zero_later_layers(st_ref)
    nblk = T // SUBLANES
    total = None
    for d in range(2):
        r = gates[:, 2 * d * W_LRU:(2 * d + 1) * W_LRU]
        i = gates[:, (2 * d + 1) * W_LRU:(2 * d + 2) * W_LRU]
        log_a = (-LRU_C * r) * decay_rate[d:d + 1, :]
        a = jnp.exp(log_a)
        th = jnp.tanh(log_a)
        b = jnp.sqrt(-2.0 * th / (1.0 - th)) * (i * xc)
        step = 1
        while step < SUBLANES:
            if d == 0:
                ok = sub >= step
                shift = step
            else:
                ok = sub <= SUBLANES - 1 - step
                shift = n_rows - step
            a_prev = jnp.where(ok, pltpu.roll(a, shift, 0), 1.0)
            b_prev = jnp.where(ok, pltpu.roll(b, shift, 0), 0.0)
            b = a * b_prev + b
            a = a * a_prev
            step *= 2
        edge = SUBLANES - 1 if d == 0 else 0
        tiles = [None] * (nseq * nblk)
        for s in range(nseq):
            carry = h0_ref[0, 0, d:d + 1, :] if has_h0 else None
            for v in (range(nblk) if d == 0 else reversed(range(nblk))):
                t = s * nblk + v
                rows = slice(t * SUBLANES, (t + 1) * SUBLANES)
                hv = b[rows]
                if carry is not None:
                    hv = hv + a[rows] * carry
                tiles[t] = hv
                carry = hv[edge:edge + 1, :]
            if emit_state:
                st_ref[s, 0, d:d + 1, :] = carry
        hs = jnp.concatenate(tiles, axis=0)
        total = hs if total is None else total + hs
    y_ref[...] = (total * _gelu_tanh(p_ref[:, W_LRU:2 * W_LRU])).astype(BF16)


def _lru(x2d, mods, row0, rows_per_mod, gain, w_all, cw, cb, wg, bg, lam, seq_len, l, h0=None, prev_states=None):
    has_h0 = h0 is not None
    extra = []
    if has_h0:
        extra.append((h0, pl.BlockSpec((1, 1, 2, W_LRU), lambda i: (i, l, 0, 0))))
    scratch = [pltpu.VMEM((SEQ_TILE, W_GRP_L), F32)]
    return _seq_call(functools.partial(_lru_body, has_h0=has_h0), "lru_lat" if has_h0 else "lru_ctx",
                     x2d, mods, row0, rows_per_mod, gain, w_all, COL_L, W_GRP_L, l, seq_len,
                     (cw, cb, wg, bg, lam), extra, W_LRU,
                     None if has_h0 else (2, W_LRU), prev_states, scratch)


def _outproj_body(x_ref, yr_ref, ys_ref, yl_ref, mod_ref, g_ref, wo_ref, x1_ref, h2_ref):
    mix = _dot(yr_ref[...], wo_ref[0, 0:W_RET, :])
    mix = mix + _dot(ys_ref[...], wo_ref[0, W_RET:W_RET + W_SSD, :])
    mix = mix + _dot(yl_ref[...], wo_ref[0, W_RET + W_SSD:D_MODEL, :])
    x1 = x_ref[...] + mod_ref[0, 0, 2:3, :] * mix
    x1_ref[...] = x1
    y = _rms_scale(x1) * g_ref[0]
    h2_ref[...] = (y * (1.0 + mod_ref[0, 0, 4:5, :]) + mod_ref[0, 0, 3:4, :]).astype(BF16)


def _out_proj(x2d, y_ret, y_ssd, y_lru, mods, row0, rows_per_mod, gain, wo, l):
    n = x2d.shape[0]
    tm = ROW_TILE
    row_spec = lambda w: pl.BlockSpec((tm, w), lambda i: (i, 0))
    return pl.pallas_call(
        _outproj_body,
        grid=(n // tm,),
        in_specs=[row_spec(D_MODEL), row_spec(W_RET), row_spec(W_SSD), row_spec(W_LRU),
                  _mod_spec(row0, rows_per_mod, tm, l), _layer_spec(gain, l), _layer_spec(wo, l)],
        out_specs=[row_spec(D_MODEL), row_spec(D_MODEL)],
        out_shape=[jax.ShapeDtypeStruct((n, D_MODEL), F32), jax.ShapeDtypeStruct((n, D_MODEL), BF16)],
        compiler_params=_params(1),
        name="out_proj",
    )(x2d, y_ret, y_ssd, y_lru, mods, gain, wo)


def _ffn_body(h_ref, x1_ref, mod_ref, wup_ref, cw_ref, cb_ref, wd_ref, fg_ref, o_ref, act_ref, *, seq_len,
              final_norm):
    tm = h_ref.shape[0]
    nseq = tm // seq_len
    row8 = lax.broadcasted_iota(jnp.int32, (SUBLANES, FF_BLOCK), 0)
    first_row = row8 == 0
    last_row = row8 == SUBLANES - 1

    def conv3(u, col0):
        cols = slice(col0, col0 + FF_BLOCK)
        w_prev, w_mid, w_next = (cw_ref[0, t:t + 1, cols] for t in range(3))
        bias = cb_ref[0, 0:1, cols]
        outs = []
        for s in range(nseq):
            us = u[s * seq_len:(s + 1) * seq_len]
            prev = pltpu.roll(us, 1, 0)
            nxt = pltpu.roll(us, seq_len - 1, 0)
            prev = jnp.concatenate([jnp.where(first_row, 0.0, prev[:SUBLANES]), prev[SUBLANES:]], axis=0)
            nxt = jnp.concatenate([nxt[:seq_len - SUBLANES], jnp.where(last_row, 0.0, nxt[seq_len - SUBLANES:])],
                                  axis=0)
            outs.append(us * w_mid + bias + prev * w_prev + nxt * w_next)
        return outs[0] if nseq == 1 else jnp.concatenate(outs, axis=0)

    h = h_ref[...]

    def up(j):
        return (_dot(h, wup_ref[0, :, j * FF_BLOCK:(j + 1) * FF_BLOCK]),
                _dot(h, wup_ref[0, :, D_FF + j * FF_BLOCK:D_FF + (j + 1) * FF_BLOCK]))

    for j in range(N_FF_BLOCKS):
        uv, ug = up(j)
        val = conv3(uv, j * FF_BLOCK)
        gate = conv3(ug, D_FF + j * FF_BLOCK)
        act_ref[:, j * FF_BLOCK:(j + 1) * FF_BLOCK] = (_silu(gate) * val).astype(BF16)
    out = x1_ref[...] + mod_ref[0, 0, 5:6, :] * _dot(act_ref[...], wd_ref[0])
    if final_norm:
        out = _rms_scale(out) * fg_ref[...]
    o_ref[...] = out


def _ffn(h2, x1, mods, row0, rows_per_mod, wup, cw, cb, wd, fgain, seq_len, final_norm, l):
    n = h2.shape[0]
    tm = SEQ_TILE
    row_spec = pl.BlockSpec((tm, D_MODEL), lambda i: (i, 0))
    return pl.pallas_call(
        functools.partial(_ffn_body, seq_len=seq_len, final_norm=final_norm),
        grid=(n // tm,),
        in_specs=[row_spec, row_spec, _mod_spec(row0, rows_per_mod, tm, l),
                  _layer_spec(wup, l), _layer_spec(cw, l), _layer_spec(cb, l), _layer_spec(wd, l),
                  _const_spec(fgain.shape)],
        out_specs=row_spec,
        out_shape=jax.ShapeDtypeStruct((n, D_MODEL), F32),
        scratch_shapes=[pltpu.VMEM((tm, D_FF), BF16)],
        compiler_params=_params(1),
        name="ffn",
    )(h2, x1, mods, wup, cw, cb, wd, fgain)


def _rope_tables(t_len):
    tok = jnp.arange(t_len)
    row_pos = (tok // GRID_W).astype(F32)
    col_pos = (tok % GRID_W).astype(F32)
    lane = jnp.arange(HEAD_PAIR)
    d = lane % DK_RET
    use_row = d < (DK_RET // 2)
    half = DK_RET // 4
    freqs = ROPE_BASE ** (-(d % half).astype(F32) / half)
    ang = jnp.where(use_row[None, :], row_pos[:, None], col_pos[:, None]) * freqs[None, :]
    first = (d % (2 * half)) < half
    return jnp.cos(ang), jnp.where(first[None, :], -jnp.sin(ang), jnp.sin(ang))


def _prep_params(w_in, ret_decay, ret_norm_g, ssd_conv_b, ssd_dt_bias, ssd_a_log, ssd_d, ssd_norm_g,
                 lru_conv_b, lru_w_a, lru_b_a, lru_w_x, lru_b_x, w_out, ffn_w_up, ffn_conv_b, ffn_w_down):
    o_z = 4 * W_RET
    o_dt = o_z + W_SSD + CONV_CH
    o_xl = o_dt + H_SSD
    dt_cols = w_in[..., o_dt:o_dt + H_SSD]
    zeros = lambda n: jnp.zeros((DEPTH, D_MODEL, n), F32)
    w_all = jnp.concatenate([w_in[..., :W_RET] * (DK_RET ** -0.5), w_in[..., W_RET:o_dt],
                             dt_cols, zeros(DT_BWD_LANE - H_SSD), dt_cols,
                             zeros(LANES - DT_BWD_LANE - H_SSD + W_SLOT_S - W_GRP_S),
                             w_in[..., o_xl:o_xl + W_GRP_L]], axis=-1).astype(BF16)

    def lane_place(pair):
        z = jnp.zeros((DEPTH, LANES), F32)
        return jnp.concatenate([pair[:, 0], z[:, :DT_BWD_LANE - H_SSD], pair[:, 1],
                                z[:, :LANES - DT_BWD_LANE - H_SSD]], axis=-1)[:, None, :]

    eye = jnp.eye(LRU_BLOCKS, dtype=F32)
    gates = jnp.stack([lru_w_a, lru_w_x], axis=2)
    wgate = jnp.einsum("lxgncd,nm->lncxgmd", gates, eye).reshape(DEPTH, W_LRU, 4 * W_LRU).astype(BF16)
    bgate = jnp.stack([lru_b_a, lru_b_x], axis=2).reshape(DEPTH, 1, 4 * W_LRU)
    return dict(
        w_all=w_all,
        dec_lane=jnp.repeat(ret_decay, DK_RET, axis=-1),
        dec_wide=jnp.broadcast_to(ret_decay.reshape(DEPTH, 2 * H_RET, 1), (DEPTH, 2 * H_RET, SCAN_CHUNK)),
        ret_gn=ret_norm_g[:, None, :],
        ssd_cb=ssd_conv_b[:, None, :],
        ssd_dtb=lane_place(ssd_dt_bias), ssd_alog=lane_place(ssd_a_log),
        ssd_dskip=jnp.repeat(ssd_d, P_SSD, axis=-1)[:, None, :], ssd_gn=ssd_norm_g[:, None, :],
        lru_cb=lru_conv_b[:, None, :], lru_wg=wgate, lru_bg=bgate,
        wo=w_out.astype(BF16), wup=ffn_w_up.astype(BF16), ffn_cb=ffn_conv_b[:, None, :],
        wd=ffn_w_down.astype(BF16),
    )


def kernel(x_prompt, x_sample, c, c_ctx, state_ret, state_ssd, state_lru, w_ada, b_ada, norm1_g, norm2_g, w_in, ret_decay, ret_norm_g, ssd_conv_w, ssd_conv_b, ssd_dt_bias, ssd_a_log, ssd_d, ssd_norm_g, lru_conv_w, lru_conv_b, lru_w_a, lru_b_a, lru_w_x, lru_b_x, lru_lambda, w_out, ffn_w_up, ffn_conv_w, ffn_conv_b, ffn_w_down, final_norm_g):
    bp, tp, _ = x_prompt.shape
    bs, ts, _ = x_sample.shape
    assert tp == SCAN_CHUNK and ts == SEQ_TILE and ts % GRID_W == 0 and (bp * tp) % SEQ_TILE == 0

    n_req = 8
    cvec = jnp.concatenate([c_ctx[None, :], c, jnp.zeros((n_req - 1 - bs, D_MODEL), F32)], axis=0)
    mods = _modulation(cvec, w_ada, b_ada).reshape(DEPTH, n_req, 6, D_MODEL)

    pp = _prep_params(w_in, ret_decay, ret_norm_g, ssd_conv_b, ssd_dt_bias, ssd_a_log, ssd_d, ssd_norm_g,
                      lru_conv_b, lru_w_a, lru_b_a, lru_w_x, lru_b_x, w_out, ffn_w_up, ffn_conv_b, ffn_w_down)
    n1 = norm1_g[:, None, :]
    n2 = norm2_g[:, None, :]
    rope_tabs = _rope_tables(ts)
    fgain = final_norm_g[None, :]

    npair = H_RET // 2
    sr = state_ret.reshape(bs, DEPTH, 2, npair, 2, DK_RET, DK_RET)
    s0_ret = jnp.einsum("bldparc,ax->bldparxc", sr, jnp.eye(2, dtype=F32)).reshape(
        bs, DEPTH, 2, npair, HEAD_PAIR, HEAD_PAIR)
    s0_ssd = state_ssd.transpose(0, 1, 2, 4, 3, 5).reshape(bs, DEPTH, 2, N_SSD, W_SSD)

    def trunk_layer(x2d, l, seq_len, row0, rows_per_mod, latent, prev=(None, None, None)):
        common = (x2d, mods, row0, rows_per_mod, n1, pp["w_all"])
        ret = _retention(*common, pp["dec_lane"], pp["dec_wide"], pp["ret_gn"], seq_len, l,
                         rope_tabs if latent else None, s0_ret if latent else None, prev[0])
        ssd = _ssd(*common, ssd_conv_w, pp["ssd_cb"], pp["ssd_dtb"], pp["ssd_alog"], pp["ssd_dskip"],
                   pp["ssd_gn"], seq_len, l, s0_ssd if latent else None, prev[1])
        lru = _lru(*common, lru_conv_w, pp["lru_cb"], pp["lru_wg"], pp["lru_bg"], lru_lambda, seq_len, l,
                   state_lru if latent else None, prev[2])
        x1, h2 = _out_proj(x2d, ret[0], ssd[0], lru[0], mods, row0, rows_per_mod, n2, pp["wo"], l)
        out = _ffn(h2, x1, mods, row0, rows_per_mod, pp["wup"], ffn_conv_w, pp["ffn_cb"], pp["wd"], fgain,
                   seq_len, l == DEPTH - 1, l)
        states = None if latent else (ret[1], ssd[1], lru[1])
        return out, states

    xp = x_prompt.reshape(bp * tp, D_MODEL)
    xs = x_sample.reshape(bs * ts, D_MODEL)
    states = (None, None, None)
    for l in range(DEPTH):
        xp, states = trunk_layer(xp, l, tp, 0, bp * tp, False, states)
        xs, _ = trunk_layer(xs, l, ts, 1, ts, True)

    return (xp.reshape(bp, tp, D_MODEL), xs.reshape(bs, ts, D_MODEL), states[0],
            jnp.swapaxes(states[1], -1, -2), states[2])
```

```python
import functools
import math

import jax
import jax.numpy as jnp
from jax import lax
from jax.experimental import pallas as pl
from jax.experimental.pallas import tpu as pltpu

F32 = jnp.float32
BF16 = jnp.bfloat16

D_MODEL = 1024
DEPTH = 2
GRID_W = 64
W_RET = 384
H_RET = 6
DK_RET = 64
W_SSD = 384
P_SSD = 64
H_SSD = 6
N_SSD = 128
G_SSD = 2
CONV_CH = W_SSD + 2 * G_SSD * N_SSD
W_LRU = 256
LRU_BLOCKS = 4
LRU_BW = 64
LRU_C = 8.0
D_FF = 2816
ROPE_BASE = 10000.0
EPS = 1e-6

LANES = 128
SUBLANES = 8
HEAD_PAIR = LANES
SCAN_CHUNK = 256
FF_BLOCK = 256
N_FF_BLOCKS = D_FF // FF_BLOCK
ROW_TILE = 512
SEQ_TILE = 1024
PROJ_COL_BLOCK = 512
VMEM_LIMIT = 56 * 1024 * 1024

W_GRP_R = 4 * W_RET
W_GRP_S = W_SSD + CONV_CH + LANES
W_GRP_L = 2 * W_LRU
W_SLOT_S = W_GRP_R
COL_R, COL_S, COL_L = 0, W_GRP_R, W_GRP_R + W_SLOT_S
W_IN_PAD = COL_L + W_GRP_L
DT_BWD_LANE = 8


def _dot(a, b):
    return jnp.dot(a, b, preferred_element_type=F32)


def _dot_nt(a, b):
    return lax.dot_general(a, b, (((1,), (1,)), ((), ())), preferred_element_type=F32)


def _sigmoid(x):
    return 1.0 / (1.0 + jnp.exp(-x))


def _silu(x):
    return x * _sigmoid(x)


def _softplus(x):
    return jnp.maximum(x, 0.0) + jnp.log1p(jnp.exp(-jnp.abs(x)))


def _log_sigmoid(x):
    return -_softplus(-x)


def _gelu_tanh(x):
    c = math.sqrt(2.0 / math.pi)
    return 0.5 * x * (1.0 + jnp.tanh(c * (x + 0.044715 * (x * x * x))))


def _rms_scale(x):
    return x * lax.rsqrt(jnp.mean(x * x, axis=-1, keepdims=True) + EPS)


def _const_spec(shape):
    zeros = (0,) * len(shape)
    return pl.BlockSpec(shape, lambda *_: zeros, pipeline_mode=pl.Buffered(1))


def _layer_spec(arr, l):
    rest = (0,) * (arr.ndim - 1)
    return pl.BlockSpec((1,) + arr.shape[1:], lambda *_: (l,) + rest, pipeline_mode=pl.Buffered(1))


def _w_in_spec(l, col0, width):
    assert col0 % width == 0
    return pl.BlockSpec((1, D_MODEL, width), lambda *_: (l, 0, col0 // width), pipeline_mode=pl.Buffered(1))


def _mod_spec(row0, rows_per_mod, tm, l):
    return pl.BlockSpec((1, 1, 6, D_MODEL), lambda i: (l, row0 + i * tm // rows_per_mod, 0, 0))


def _params(n_axes):
    return pltpu.CompilerParams(dimension_semantics=("arbitrary",) * n_axes,
                                vmem_limit_bytes=VMEM_LIMIT)


def _mod_body(c_ref, w_ref, b_ref, o_ref):
    s = _silu(c_ref[...]).astype(BF16)
    o_ref[0] = _dot(s, w_ref[0].astype(BF16)) + b_ref[0]


def _modulation(cvec, w_ada, b_ada):
    rows = cvec.shape[0]
    nblk = (6 * D_MODEL) // D_MODEL
    return pl.pallas_call(
        _mod_body,
        grid=(DEPTH, nblk),
        in_specs=[pl.BlockSpec((rows, D_MODEL), lambda l, j: (0, 0)),
                  pl.BlockSpec((1, D_MODEL, D_MODEL), lambda l, j: (l, 0, j)),
                  pl.BlockSpec((1, 1, D_MODEL), lambda l, j: (l, 0, j))],
        out_specs=pl.BlockSpec((1, rows, D_MODEL), lambda l, j: (l, 0, j)),
        out_shape=jax.ShapeDtypeStruct((DEPTH, rows, 6 * D_MODEL), F32),
        compiler_params=_params(2),
        name="adaln_mod",
    )(cvec, w_ada, b_ada.reshape(DEPTH, 1, 6 * D_MODEL))


IN_DIM = 4 * W_RET + W_SSD + CONV_CH + H_SSD + 2 * W_LRU
REGROUP_ROWS = 256


def _regroup_body(w_ref, o_ref):
    rows = w_ref.shape[1]
    o_dt = COL_S + W_SSD + CONV_CH
    o_ref[0, :, 0:W_RET] = (w_ref[0, :, 0:W_RET] * (DK_RET ** -0.5)).astype(BF16)
    o_ref[0, :, W_RET:o_dt] = w_ref[0, :, W_RET:o_dt].astype(BF16)
    tail = w_ref[0, :, o_dt:IN_DIM]
    first = tail[:, 0:LANES]
    lane = lax.broadcasted_iota(jnp.int32, (rows, LANES), 1)
    dt_fwd = jnp.where(lane < H_SSD, first, 0.0)
    dt_bwd = jnp.where((lane >= DT_BWD_LANE) & (lane < DT_BWD_LANE + H_SSD), pltpu.roll(first, DT_BWD_LANE, 1), 0.0)
    o_ref[0, :, o_dt:o_dt + LANES] = (dt_fwd + dt_bwd).astype(BF16)
    o_ref[0, :, o_dt + LANES:COL_L] = jnp.zeros((rows, COL_L - o_dt - LANES), BF16)
    o_ref[0, :, COL_L:W_IN_PAD] = tail[:, H_SSD:H_SSD + W_GRP_L].astype(BF16)


def _regroup_w_in(w_in):
    rb = REGROUP_ROWS
    return pl.pallas_call(
        _regroup_body,
        grid=(DEPTH, D_MODEL // rb),
        in_specs=[pl.BlockSpec((1, rb, IN_DIM), lambda l, i: (l, i, 0))],
        out_specs=pl.BlockSpec((1, rb, W_IN_PAD), lambda l, i: (l, i, 0)),
        out_shape=jax.ShapeDtypeStruct((DEPTH, D_MODEL, W_IN_PAD), BF16),
        compiler_params=_params(2),
        name="w_in_regroup",
    )(w_in)


def _project(x_ref, mod_ref, g_ref, w_ref, p_ref, rows=slice(None)):
    y = _rms_scale(x_ref[rows, :]) * g_ref[0]
    h = (y * (1.0 + mod_ref[0, 0, 1:2, :]) + mod_ref[0, 0, 0:1, :]).astype(BF16)
    width = p_ref.shape[1]
    for j in range(0, width, PROJ_COL_BLOCK):
        jb = min(PROJ_COL_BLOCK, width - j)
        p_ref[rows, j:j + jb] = _dot(h, w_ref[0, :, j:j + jb])


def _shift_rows(x, k, pos, seq_len):
    n = x.shape[0]
    if k == 0:
        return x
    rolled = pltpu.roll(x, (-k) % n, 0)
    ok = (pos >= -k) if k < 0 else (pos <= seq_len - 1 - k)
    return jnp.where(ok, rolled, 0.0)


def _dwconv4(x, w_ref, b_ref, col0, pos, seq_len):
    width = x.shape[1]
    cols = slice(col0, col0 + width)
    acc = x * w_ref[0, 1:2, cols] + b_ref[0, 0:1, cols]
    acc = acc + _shift_rows(x, -1, pos, seq_len) * w_ref[0, 0:1, cols]
    acc = acc + _shift_rows(x, 1, pos, seq_len) * w_ref[0, 2:3, cols]
    acc = acc + _shift_rows(x, 2, pos, seq_len) * w_ref[0, 3:4, cols]
    return acc


def _zero_later_layers(st_ref):
    st_ref[:, 1:] = jnp.zeros((st_ref.shape[0], DEPTH - 1) + tuple(st_ref.shape[2:]), F32)


def _state_output(nseq_total, nseq_tile, tail, l, prev, n_inputs):
    zeros = (0,) * len(tail)
    shape = jax.ShapeDtypeStruct((nseq_total, DEPTH) + tail, F32)
    if l == 0:
        return pl.BlockSpec((nseq_tile, DEPTH) + tail, lambda i: (i, 0) + zeros), shape, [], [], {}
    spec = pl.BlockSpec((nseq_tile, 1) + tail, lambda i: (i, l) + zeros)
    return spec, shape, [pl.BlockSpec(memory_space=pl.ANY)], [prev], {n_inputs: 1}


def _head_lane_expand(x, lane0):
    rows = x.shape[0]
    low = lax.broadcasted_iota(jnp.int32, (rows, HEAD_PAIR), 1) < DK_RET
    blocks = []
    for p in range(H_RET // 2):
        a = jnp.broadcast_to(x[:, lane0 + 2 * p:lane0 + 2 * p + 1], (rows, HEAD_PAIR))
        b = jnp.broadcast_to(x[:, lane0 + 2 * p + 1:lane0 + 2 * p + 2], (rows, HEAD_PAIR))
        blocks.append(jnp.where(low, a, b))
    return jnp.concatenate(blocks, axis=1)


def _seq_call(body, name, x2d, mods, row0, rows_per_mod, gain, w_all, col0, width, l, seq_len, params,
              extra_in, y_width, state_tail, prev_states, scratch):
    n = x2d.shape[0]
    tm = SEQ_TILE
    nseq_tile = tm // seq_len
    emit_state = state_tail is not None
    in_specs = [pl.BlockSpec((tm, D_MODEL), lambda i: (i, 0)), _mod_spec(row0, rows_per_mod, tm, l),
                _layer_spec(gain, l), _w_in_spec(l, col0, width)]
    in_specs += [_layer_spec(a, l) for a in params]
    in_specs += [spec for _, spec in extra_in]
    args = [x2d, mods, gain, w_all] + list(params) + [a for a, _ in extra_in]
    out_specs = [pl.BlockSpec((tm, y_width), lambda i: (i, 0))]
    out_shape = [jax.ShapeDtypeStruct((n, y_width), BF16)]
    aliases = {}
    if emit_state:
        spec, shape, extra_specs, extra_args, aliases = _state_output(
            n // seq_len, nseq_tile, state_tail, l, prev_states, len(args))
        out_specs.append(spec)
        out_shape.append(shape)
        in_specs += extra_specs
        args += extra_args
    return pl.pallas_call(
        functools.partial(body, seq_len=seq_len, emit_state=emit_state, first_layer=l == 0),
        grid=(n // tm,), in_specs=in_specs, out_specs=out_specs, out_shape=out_shape,
        input_output_aliases=aliases, scratch_shapes=scratch, compiler_params=_params(1), name=name,
    )(*args)


def _ret_body(*refs, seq_len, rope, has_s0, emit_state, first_layer):
    it = iter(refs)
    x_ref, mod_ref, g_ref, w_ref = next(it), next(it), next(it), next(it)
    dec_lane_ref = next(it)
    dec_wide_ref = next(it)
    gn_ref = next(it)
    cos_ref = sin_ref = s0_ref = st_ref = qk_ref = None
    if rope:
        cos_ref = next(it)
        sin_ref = next(it)
    if has_s0:
        s0_ref = next(it)
    if emit_state and not first_layer:
        next(it)
    y_ref = next(it)
    if emit_state:
        st_ref = next(it)
    p_ref = next(it)
    dmat_ref = next(it)
    wts_ref = next(it)
    if rope:
        qk_ref = next(it)

    C = SCAN_CHUNK
    nch = seq_len // C
    nseq = p_ref.shape[0] // seq_len
    npair = H_RET // 2
    assert not (has_s0 or rope) or nseq == 1
    if nseq == 1:
        _project(x_ref, mod_ref, g_ref, w_ref, p_ref)
    lgl = _log_sigmoid(dec_lane_ref[0])

    @pl.when(pl.program_id(0) == 0)
    def _():
        lgw = _log_sigmoid(dec_wide_ref[0])
        ii = lax.broadcasted_iota(jnp.int32, (C, C), 0)
        mm = lax.broadcasted_iota(jnp.int32, (C, C), 1)
        dif = (ii - mm).astype(F32)
        neg_inf = -jnp.inf
        for h in range(H_RET):
            df = jnp.exp(jnp.where(ii >= mm, dif * lgw[h:h + 1, :], neg_inf))
            db = jnp.exp(jnp.where(mm >= ii, (-dif) * lgw[H_RET + h:H_RET + h + 1, :], neg_inf))
            dmat_ref[h] = df + db
        ri = lax.broadcasted_iota(jnp.int32, (C, W_RET), 0).astype(F32)
        wts_ref[0] = jnp.exp((C - 1.0 - ri) * lgl[0:1, :])
        wts_ref[1] = jnp.exp(ri * lgl[1:2, :])
        wts_ref[2] = jnp.exp((ri + 1.0) * lgl[0:1, :])
        wts_ref[3] = jnp.exp((C - ri) * lgl[1:2, :])

    lane = lax.broadcasted_iota(jnp.int32, (C, HEAD_PAIR), 1)
    low = lane < DK_RET
    r128 = lax.broadcasted_iota(jnp.int32, (HEAD_PAIR, HEAD_PAIR), 0) >= DK_RET
    c128 = lax.broadcasted_iota(jnp.int32, (HEAD_PAIR, HEAD_PAIR), 1) >= DK_RET
    same_head = r128 == c128
    avg = jnp.where(same_head, 1.0 / DK_RET, 0.0).astype(BF16)

    need_states = emit_state or (has_s0 and nch > 1)
    if has_s0:
        dec_f = jnp.exp(float(C) * lgl[0:1, :])
        dec_b = jnp.exp(float(C) * lgl[1:2, :])

    if rope:
        swap_low = (lax.broadcasted_iota(jnp.int32, (seq_len, HEAD_PAIR), 1) % 32) < 16
        cos = cos_ref[...]
        sin = sin_ref[...]
        for blk in range(2 * npair):
            cols_b = slice(blk * HEAD_PAIR, (blk + 1) * HEAD_PAIR)
            x = p_ref[:, cols_b]
            swapped = jnp.where(swap_low, pltpu.roll(x, HEAD_PAIR - 16, 1), pltpu.roll(x, 16, 1))
            qk_ref[:, cols_b] = x * cos + swapped * sin

    def cols(p):
        return slice(p * HEAD_PAIR, (p + 1) * HEAD_PAIR)

    if emit_state and first_layer:
        _zero_later_layers(st_ref)

    def project_seq(s):
        _project(x_ref, mod_ref, g_ref, w_ref, p_ref, slice(s * seq_len, (s + 1) * seq_len))

    if nseq > 1:
        project_seq(0)
    for s in range(nseq):
        if s + 1 < nseq:
            project_seq(s + 1)

        def rows(c, s=s):
            return slice(s * seq_len + c * C, s * seq_len + (c + 1) * C)

        def get_q(c, p):
            src = qk_ref if rope else p_ref
            return src[rows(c), p * HEAD_PAIR:(p + 1) * HEAD_PAIR]

        def get_k(c, p):
            src = qk_ref if rope else p_ref
            return src[rows(c), W_RET + p * HEAD_PAIR:W_RET + (p + 1) * HEAD_PAIR]

        def get_v(c, p):
            return p_ref[rows(c), 2 * W_RET + p * HEAD_PAIR:2 * W_RET + (p + 1) * HEAD_PAIR]

        ds_f = [[None] * npair for _ in range(nch)]
        ds_b = [[None] * npair for _ in range(nch)]
        if need_states:
            for c in range(nch):
                for p in range(npair):
                    k = get_k(c, p)
                    vb = get_v(c, p).astype(BF16)
                    kf = jnp.transpose(k * wts_ref[0, :, cols(p)]).astype(BF16)
                    kb = jnp.transpose(k * wts_ref[1, :, cols(p)]).astype(BF16)
                    ds_f[c][p] = _dot(kf, vb)
                    ds_b[c][p] = _dot(kb, vb)

        if emit_state:
            for p in range(npair):
                for a in range(2):
                    blk = slice(a * DK_RET, (a + 1) * DK_RET)
                    st_ref[s, 0, 0, 2 * p + a] = ds_f[0][p][blk, blk]
                    st_ref[s, 0, 1, 2 * p + a] = ds_b[0][p][blk, blk]

        sf_in = [[None] * npair for _ in range(nch)]
        sb_in = [[None] * npair for _ in range(nch)]
        if has_s0:
            low_half = lax.broadcasted_iota(jnp.int32, (DK_RET, HEAD_PAIR), 1) < DK_RET

            def pair_state(d, p):
                u = s0_ref[0, 0, d, p]
                return jnp.concatenate([jnp.where(low_half, u, 0.0), jnp.where(low_half, 0.0, u)], axis=0)

            for p in range(npair):
                st = pair_state(0, p)
                for c in range(nch):
                    sf_in[c][p] = st
                    if c + 1 < nch:
                        st = st * dec_f[:, cols(p)] + jnp.where(same_head, ds_f[c][p], 0.0)
                st = pair_state(1, p)
                for c in reversed(range(nch)):
                    sb_in[c][p] = st
                    if c > 0:
                        st = st * dec_b[:, cols(p)] + jnp.where(same_head, ds_b[c][p], 0.0)

        for c in range(nch):
            pairs = range(npair)
            q = [get_q(c, p) for p in pairs]
            kb = [get_k(c, p).astype(BF16) for p in pairs]
            vb = [get_v(c, p).astype(BF16) for p in pairs]
            scores = []
            for p in pairs:
                scores.append(_dot_nt(jnp.where(low, q[p], 0.0).astype(BF16), kb[p]))
                scores.append(_dot_nt(jnp.where(low, 0.0, q[p]).astype(BF16), kb[p]))
            inter = [None] * npair
            if has_s0:
                for p in pairs:
                    inter[p] = (_dot((q[p] * wts_ref[2, :, cols(p)]).astype(BF16), sf_in[c][p].astype(BF16))
                                + _dot((q[p] * wts_ref[3, :, cols(p)]).astype(BF16), sb_in[c][p].astype(BF16)))
            o = []
            for p in pairs:
                o0 = _dot((scores[2 * p] * dmat_ref[2 * p]).astype(BF16), vb[p])
                o1 = _dot((scores[2 * p + 1] * dmat_ref[2 * p + 1]).astype(BF16), vb[p])
                op = jnp.where(low, o0, o1)
                o.append(op + inter[p] if has_s0 else op)
            mu = [_dot(o[p].astype(BF16), avg) for p in pairs]
            d = [o[p] - mu[p] for p in pairs]
            var = [_dot((d[p] * d[p]).astype(BF16), avg) for p in pairs]
            for p in pairs:
                on = d[p] * lax.rsqrt(var[p] + EPS)
                g = p_ref[rows(c), 3 * W_RET + p * HEAD_PAIR:3 * W_RET + (p + 1) * HEAD_PAIR]
                y_ref[rows(c), cols(p)] = (on * gn_ref[0, 0:1, cols(p)] * _silu(g)).astype(BF16)


def _retention(x2d, mods, row0, rows_per_mod, gain, w_all, dec_lane, dec_wide, gn, seq_len, l,
               rope_tabs=None, s0=None, prev_states=None):
    rope = rope_tabs is not None
    has_s0 = s0 is not None
    npair = H_RET // 2
    extra = []
    if rope:
        extra += [(t, _const_spec(t.shape)) for t in rope_tabs]
    if has_s0:
        extra.append((s0, pl.BlockSpec((1, 1, 2, npair, DK_RET, HEAD_PAIR), lambda i: (i, l, 0, 0, 0, 0))))
    scratch = [pltpu.VMEM((SEQ_TILE, W_GRP_R), F32),
               pltpu.VMEM((H_RET, SCAN_CHUNK, SCAN_CHUNK), F32), pltpu.VMEM((4, SCAN_CHUNK, W_RET), F32)]
    if rope:
        scratch.append(pltpu.VMEM((seq_len, 2 * W_RET), F32))
    return _seq_call(functools.partial(_ret_body, rope=rope, has_s0=has_s0),
                     "retention_lat" if rope else "retention_ctx",
                     x2d, mods, row0, rows_per_mod, gain, w_all, COL_R, W_GRP_R, l, seq_len,
                     (dec_lane, dec_wide, gn), extra, W_RET,
                     None if has_s0 else (2, H_RET, DK_RET, DK_RET), prev_states, scratch)


def _ssd_body(*refs, seq_len, has_s0, emit_state, first_layer):
    it = iter(refs)
    x_ref, mod_ref, g_ref, w_ref = next(it), next(it), next(it), next(it)
    cw_ref = next(it)
    cb_ref = next(it)
    dtb_ref = next(it)
    alog_ref = next(it)
    dskip_ref = next(it)
    gn_ref = next(it)
    s0_ref = st_ref = None
    if has_s0:
        s0_ref = next(it)
    if emit_state and not first_layer:
        next(it)
    y_ref = next(it)
    if emit_state:
        st_ref = next(it)
    p_ref = next(it)
    xs_ref = next(it)
    bc_ref = next(it)

    T = seq_len
    C = SCAN_CHUNK
    nch = T // C
    n_rows = p_ref.shape[0]
    nseq = n_rows // T
    assert not has_s0 or nseq == 1
    assert T & (T - 1) == 0
    pos = lax.broadcasted_iota(jnp.int32, (T, LANES), 0)

    def project_and_conv(s):
        seq_rows = slice(s * T, (s + 1) * T)
        _project(x_ref, mod_ref, g_ref, w_ref, p_ref, seq_rows)
        for blk in range(CONV_CH // LANES):
            x = p_ref[seq_rows, W_SSD + blk * LANES:W_SSD + (blk + 1) * LANES]
            a = _silu(_dwconv4(x, cw_ref, cb_ref, blk * LANES, pos, T))
            if blk < W_SSD // LANES:
                xs_ref[seq_rows, blk * LANES:(blk + 1) * LANES] = a
            else:
                o = blk * LANES - W_SSD
                bc_ref[seq_rows, o:o + LANES] = a

    a_neg = -jnp.exp(alog_ref[0])
    ii = lax.broadcasted_iota(jnp.int32, (C, C), 0)
    mm = lax.broadcasted_iota(jnp.int32, (C, C), 1)
    lower = ii >= mm
    upper = mm >= ii
    tri_l = jnp.where(lower, 1.0, 0.0).astype(BF16)
    tri_u = jnp.where(upper, 1.0, 0.0).astype(BF16)
    neg_inf = -jnp.inf
    grp0 = lax.broadcasted_iota(jnp.int32, (C, W_SSD), 1) < (W_SSD // G_SSD)
    grp0_s = lax.broadcasted_iota(jnp.int32, (N_SSD, W_SSD), 1) < (W_SSD // G_SSD)
    grp0_t = lax.broadcasted_iota(jnp.int32, (W_SSD, N_SSD), 0) < (W_SSD // G_SSD)
    low = lax.broadcasted_iota(jnp.int32, (C, HEAD_PAIR), 1) < P_SSD

    if emit_state and first_layer:
        _zero_later_layers(st_ref)

    project_and_conv(0)
    for s in range(nseq):
        if s + 1 < nseq:
            project_and_conv(s + 1)

        def rows(c, s=s):
            return slice(s * T + c * C, s * T + (c + 1) * C)

        def chunk_scalars(c):
            dt = _softplus(p_ref[rows(c), W_SSD + CONV_CH:W_SSD + CONV_CH + LANES] + dtb_ref[0])
            la = dt * a_neg
            a1 = la.astype(BF16)
            r1 = la - a1.astype(F32)
            a2 = r1.astype(BF16)
            a3 = (r1 - a2.astype(F32)).astype(BF16)
            pre = _dot(tri_l, a1) + _dot(tri_l, a2) + _dot(tri_l, a3)
            suf = _dot(tri_u, a1) + _dot(tri_u, a2) + _dot(tri_u, a3)
            return dt, pre, suf

        def state_weights(c):
            dt, pre, suf = chunk_scalars(c)
            wf = jnp.exp(pre[C - 1:C, :] - pre) * dt
            wb = jnp.exp(suf[0:1, :] - suf) * dt
            xs = xs_ref[rows(c), :]
            return (xs * _head_lane_expand(wf, 0), xs * _head_lane_expand(wb, DT_BWD_LANE), pre, suf)

        if emit_state:
            xf, xb, _, _ = state_weights(0)
            bm = [bc_ref[rows(0), g * N_SSD:(g + 1) * N_SSD].astype(BF16) for g in range(G_SSD)]
            for d, xw in enumerate((xf, xb)):
                xt = jnp.transpose(xw).astype(BF16)
                st = jnp.where(grp0_t, _dot(xt, bm[0]), _dot(xt, bm[1]))
                for h in range(H_SSD):
                    st_ref[s, 0, d, h] = st[h * P_SSD:(h + 1) * P_SSD, :]

        ds_f = [None] * nch
        ds_b = [None] * nch
        dec_f = [None] * nch
        dec_b = [None] * nch
        if has_s0 and nch > 1:
            for c in range(nch):
                xf, xb, pre, suf = state_weights(c)
                bt0 = jnp.transpose(bc_ref[rows(c), 0:N_SSD]).astype(BF16)
                bt1 = jnp.transpose(bc_ref[rows(c), N_SSD:2 * N_SSD]).astype(BF16)
                ds_f[c] = jnp.where(grp0_s, _dot(bt0, xf.astype(BF16)), _dot(bt1, xf.astype(BF16)))
                ds_b[c] = jnp.where(grp0_s, _dot(bt0, xb.astype(BF16)), _dot(bt1, xb.astype(BF16)))
                dec_f[c] = _head_lane_expand(jnp.exp(pre[C - 1:C, :]), 0)
                dec_b[c] = _head_lane_expand(jnp.exp(suf[0:1, :]), DT_BWD_LANE)

        sf_in = [None] * nch
        sb_in = [None] * nch
        if has_s0:
            st = s0_ref[0, 0, 0]
            for c in range(nch):
                sf_in[c] = st
                if c + 1 < nch:
                    st = st * dec_f[c] + ds_f[c]
            st = s0_ref[0, 0, 1]
            for c in reversed(range(nch)):
                sb_in[c] = st
                if c > 0:
                    st = st * dec_b[c] + ds_b[c]

        for c in range(nch):
            dt, pre, suf = chunk_scalars(c)
            pre_t = jnp.transpose(pre)
            suf_t = jnp.transpose(suf)
            dt_t = jnp.transpose(dt)
            xs = xs_ref[rows(c), :]
            xs_b = xs.astype(BF16)
            cm = [bc_ref[rows(c), 2 * N_SSD + g * N_SSD:2 * N_SSD + (g + 1) * N_SSD].astype(BF16)
                  for g in range(G_SSD)]
            bm = [bc_ref[rows(c), g * N_SSD:(g + 1) * N_SSD].astype(BF16) for g in range(G_SSD)]
            gram = [_dot_nt(cm[g], bm[g]) for g in range(G_SSD)]
            heads = []
            for h in range(H_SSD):
                g = h // (H_SSD // G_SSD)
                hb = DT_BWD_LANE + h
                df = jnp.exp(jnp.where(lower, pre[:, h:h + 1] - pre_t[h:h + 1, :], neg_inf))
                db = jnp.exp(jnp.where(upper, suf[:, hb:hb + 1] - suf_t[hb:hb + 1, :], neg_inf))
                w = gram[g] * (df * dt_t[h:h + 1, :] + db * dt_t[hb:hb + 1, :])
                p = h // 2
                heads.append(_dot(w.astype(BF16), xs_b[:, p * HEAD_PAIR:(p + 1) * HEAD_PAIR]))
            o = jnp.concatenate([jnp.where(low, heads[2 * p], heads[2 * p + 1]) for p in range(H_SSD // 2)],
                                axis=1)
            if has_s0:
                sf = sf_in[c].astype(BF16)
                sb = sb_in[c].astype(BF16)
                o = o + _head_lane_expand(jnp.exp(pre), 0) * jnp.where(grp0, _dot(cm[0], sf), _dot(cm[1], sf))
                o = o + (_head_lane_expand(jnp.exp(suf), DT_BWD_LANE)
                         * jnp.where(grp0, _dot(cm[0], sb), _dot(cm[1], sb)))
            y = o + dskip_ref[0] * xs
            yz = y * _silu(p_ref[rows(c), 0:W_SSD])
            y_ref[rows(c), :] = (_rms_scale(yz) * gn_ref[0]).astype(BF16)


def _ssd(x2d, mods, row0, rows_per_mod, gain, w_all, cw, cb, dtb, alog, dskip, gn, seq_len, l,
         s0=None, prev_states=None):
    has_s0 = s0 is not None
    extra = []
    if has_s0:
        extra.append((s0, pl.BlockSpec((1, 1, 2, N_SSD, W_SSD), lambda i: (i, l, 0, 0, 0))))
    scratch = [pltpu.VMEM((SEQ_TILE, W_GRP_S), F32), pltpu.VMEM((SEQ_TILE, W_SSD), F32),
               pltpu.VMEM((SEQ_TILE, 2 * G_SSD * N_SSD), F32)]
    return _seq_call(functools.partial(_ssd_body, has_s0=has_s0), "ssd_lat" if has_s0 else "ssd_ctx",
                     x2d, mods, row0, rows_per_mod, gain, w_all, COL_S, W_SLOT_S, l, seq_len,
                     (cw, cb, dtb, alog, dskip, gn), extra, W_SSD,
                     None if has_s0 else (2, H_SSD, P_SSD, N_SSD), prev_states, scratch)


def _lru_body(*refs, seq_len, has_h0, emit_state, first_layer):
    it = iter(refs)
    x_ref, mod_ref, g_ref, w_ref = next(it), next(it), next(it), next(it)
    cw_ref = next(it)
    cb_ref = next(it)
    wg_ref = next(it)
    bg_ref = next(it)
    lam_ref = next(it)
    h0_ref = st_ref = None
    if has_h0:
        h0_ref = next(it)
    if emit_state and not first_layer:
        next(it)
    y_ref = next(it)
    if emit_state:
        st_ref = next(it)
    p_ref = next(it)

    T = seq_len
    n_rows = p_ref.shape[0]
    nseq = n_rows // T
    assert not has_h0 or nseq == 1
    assert T & (T - 1) == 0
    _project(x_ref, mod_ref, g_ref, w_ref, p_ref)
    row = lax.broadcasted_iota(jnp.int32, (n_rows, W_LRU), 0)
    pos = row & (T - 1)
    sub = row & (SUBLANES - 1)
    xc = _dwconv4(p_ref[:, 0:W_LRU], cw_ref, cb_ref, 0, pos, T)
    gates = _sigmoid(_dot(xc.astype(BF16), wg_ref[0]) + bg_ref[0])
    decay_rate = _softplus(-lam_ref[0])
    if emit_state and first_layer:
        _zero_later_layers(st_ref)
    nblk = T // SUBLANES
    total = None
    for d in range(2):
        r = gates[:, 2 * d * W_LRU:(2 * d + 1) * W_LRU]
        i = gates[:, (2 * d + 1) * W_LRU:(2 * d + 2) * W_LRU]
        log_a = (-LRU_C * r) * decay_rate[d:d + 1, :]
        a = jnp.exp(log_a)
        th = jnp.tanh(log_a)
        b = jnp.sqrt(-2.0 * th / (1.0 - th)) * (i * xc)
        step = 1
        while step < SUBLANES:
            if d == 0:
                ok = sub >= step
                shift = step
            else:
                ok = sub <= SUBLANES - 1 - step
                shift = n_rows - step
            a_prev = jnp.where(ok, pltpu.roll(a, shift, 0), 1.0)
            b_prev = jnp.where(ok, pltpu.roll(b, shift, 0), 0.0)
            b = a * b_prev + b
            a = a * a_prev
            step *= 2
        edge = SUBLANES - 1 if d == 0 else 0
        tiles = [None] * (nseq * nblk)
        for s in range(nseq):
            carry = h0_ref[0, 0, d:d + 1, :] if has_h0 else None
            for v in (range(nblk) if d == 0 else reversed(range(nblk))):
                t = s * nblk + v
                rows = slice(t * SUBLANES, (t + 1) * SUBLANES)
                hv = b[rows]
                if carry is not None:
                    hv = hv + a[rows] * carry
                tiles[t] = hv
                carry = hv[edge:edge + 1, :]
            if emit_state:
                st_ref[s, 0, d:d + 1, :] = carry
        hs = jnp.concatenate(tiles, axis=0)
        total = hs if total is None else total + hs
    y_ref[...] = (total * _gelu_tanh(p_ref[:, W_LRU:2 * W_LRU])).astype(BF16)


def _lru(x2d, mods, row0, rows_per_mod, gain, w_all, cw, cb, wg, bg, lam, seq_len, l, h0=None, prev_states=None):
    has_h0 = h0 is not None
    extra = []
    if has_h0:
        extra.append((h0, pl.BlockSpec((1, 1, 2, W_LRU), lambda i: (i, l, 0, 0))))
    scratch = [pltpu.VMEM((SEQ_TILE, W_GRP_L), F32)]
    return _seq_call(functools.partial(_lru_body, has_h0=has_h0), "lru_lat" if has_h0 else "lru_ctx",
                     x2d, mods, row0, rows_per_mod, gain, w_all, COL_L, W_GRP_L, l, seq_len,
                     (cw, cb, wg, bg, lam), extra, W_LRU,
                     None if has_h0 else (2, W_LRU), prev_states, scratch)


def _outproj_body(x_ref, yr_ref, ys_ref, yl_ref, mod_ref, g_ref, wo_ref, x1_ref, h2_ref):
    mix = _dot(yr_ref[...], wo_ref[0, 0:W_RET, :])
    mix = mix + _dot(ys_ref[...], wo_ref[0, W_RET:W_RET + W_SSD, :])
    mix = mix + _dot(yl_ref[...], wo_ref[0, W_RET + W_SSD:D_MODEL, :])
    x1 = x_ref[...] + mod_ref[0, 0, 2:3, :] * mix
    x1_ref[...] = x1
    y = _rms_scale(x1) * g_ref[0]
    h2_ref[...] = (y * (1.0 + mod_ref[0, 0, 4:5, :]) + mod_ref[0, 0, 3:4, :]).astype(BF16)


def _out_proj(x2d, y_ret, y_ssd, y_lru, mods, row0, rows_per_mod, gain, wo, l):
    n = x2d.shape[0]
    tm = ROW_TILE
    row_spec = lambda w: pl.BlockSpec((tm, w), lambda i: (i, 0))
    return pl.pallas_call(
        _outproj_body,
        grid=(n // tm,),
        in_specs=[row_spec(D_MODEL), row_spec(W_RET), row_spec(W_SSD), row_spec(W_LRU),
                  _mod_spec(row0, rows_per_mod, tm, l), _layer_spec(gain, l), _layer_spec(wo, l)],
        out_specs=[row_spec(D_MODEL), row_spec(D_MODEL)],
        out_shape=[jax.ShapeDtypeStruct((n, D_MODEL), F32), jax.ShapeDtypeStruct((n, D_MODEL), BF16)],
        compiler_params=_params(1),
        name="out_proj",
    )(x2d, y_ret, y_ssd, y_lru, mods, gain, wo)


def _ffn_body(h_ref, x1_ref, mod_ref, wup_ref, cw_ref, cb_ref, wd_ref, fg_ref, o_ref, act_ref, *, seq_len,
              final_norm):
    tm = h_ref.shape[0]
    nseq = tm // seq_len
    row8 = lax.broadcasted_iota(jnp.int32, (SUBLANES, FF_BLOCK), 0)
    first_row = row8 == 0
    last_row = row8 == SUBLANES - 1

    def conv3(u, col0):
        cols = slice(col0, col0 + FF_BLOCK)
        w_prev, w_mid, w_next = (cw_ref[0, t:t + 1, cols] for t in range(3))
        bias = cb_ref[0, 0:1, cols]
        outs = []
        for s in range(nseq):
            us = u[s * seq_len:(s + 1) * seq_len]
            prev = pltpu.roll(us, 1, 0)
            nxt = pltpu.roll(us, seq_len - 1, 0)
            prev = jnp.concatenate([jnp.where(first_row, 0.0, prev[:SUBLANES]), prev[SUBLANES:]], axis=0)
            nxt = jnp.concatenate([nxt[:seq_len - SUBLANES], jnp.where(last_row, 0.0, nxt[seq_len - SUBLANES:])],
                                  axis=0)
            outs.append(us * w_mid + bias + prev * w_prev + nxt * w_next)
        return outs[0] if nseq == 1 else jnp.concatenate(outs, axis=0)

    h = h_ref[...]

    def up(j):
        return (_dot(h, wup_ref[0, :, j * FF_BLOCK:(j + 1) * FF_BLOCK]),
                _dot(h, wup_ref[0, :, D_FF + j * FF_BLOCK:D_FF + (j + 1) * FF_BLOCK]))

    for j in range(N_FF_BLOCKS):
        uv, ug = up(j)
        val = conv3(uv, j * FF_BLOCK)
        gate = conv3(ug, D_FF + j * FF_BLOCK)
        act_ref[:, j * FF_BLOCK:(j + 1) * FF_BLOCK] = (_silu(gate) * val).astype(BF16)
    out = x1_ref[...] + mod_ref[0, 0, 5:6, :] * _dot(act_ref[...], wd_ref[0])
    if final_norm:
        out = _rms_scale(out) * fg_ref[...]
    o_ref[...] = out


def _ffn(h2, x1, mods, row0, rows_per_mod, wup, cw, cb, wd, fgain, seq_len, final_norm, l):
    n = h2.shape[0]
    tm = SEQ_TILE
    row_spec = pl.BlockSpec((tm, D_MODEL), lambda i: (i, 0))
    return pl.pallas_call(
        functools.partial(_ffn_body, seq_len=seq_len, final_norm=final_norm),
        grid=(n // tm,),
        in_specs=[row_spec, row_spec, _mod_spec(row0, rows_per_mod, tm, l),
                  _layer_spec(wup, l), _layer_spec(cw, l), _layer_spec(cb, l), _layer_spec(wd, l),
                  _const_spec(fgain.shape)],
        out_specs=row_spec,
        out_shape=jax.ShapeDtypeStruct((n, D_MODEL), F32),
        scratch_shapes=[pltpu.VMEM((tm, D_FF), BF16)],
        compiler_params=_params(1),
        name="ffn",
    )(h2, x1, mods, wup, cw, cb, wd, fgain)


def _rope_tables(t_len):
    tok = jnp.arange(t_len)
    row_pos = (tok // GRID_W).astype(F32)
    col_pos = (tok % GRID_W).astype(F32)
    lane = jnp.arange(HEAD_PAIR)
    d = lane % DK_RET
    use_row = d < (DK_RET // 2)
    half = DK_RET // 4
    freqs = ROPE_BASE ** (-(d % half).astype(F32) / half)
    ang = jnp.where(use_row[None, :], row_pos[:, None], col_pos[:, None]) * freqs[None, :]
    first = (d % (2 * half)) < half
    return jnp.cos(ang), jnp.where(first[None, :], -jnp.sin(ang), jnp.sin(ang))


def _prep_params(w_in, ret_decay, ret_norm_g, ssd_conv_b, ssd_dt_bias, ssd_a_log, ssd_d, ssd_norm_g,
                 lru_conv_b, lru_w_a, lru_b_a, lru_w_x, lru_b_x, w_out, ffn_w_up, ffn_conv_b, ffn_w_down):
    w_all = _regroup_w_in(w_in)

    def lane_place(pair):
        z = jnp.zeros((DEPTH, LANES), F32)
        return jnp.concatenate([pair[:, 0], z[:, :DT_BWD_LANE - H_SSD], pair[:, 1],
                                z[:, :LANES - DT_BWD_LANE - H_SSD]], axis=-1)[:, None, :]

    eye = jnp.eye(LRU_BLOCKS, dtype=F32)
    gates = jnp.stack([lru_w_a, lru_w_x], axis=2)
    wgate = jnp.einsum("lxgncd,nm->lncxgmd", gates, eye).reshape(DEPTH, W_LRU, 4 * W_LRU).astype(BF16)
    bgate = jnp.stack([lru_b_a, lru_b_x], axis=2).reshape(DEPTH, 1, 4 * W_LRU)
    return dict(
        w_all=w_all,
        dec_lane=jnp.repeat(ret_decay, DK_RET, axis=-1),
        dec_wide=jnp.broadcast_to(ret_decay.reshape(DEPTH, 2 * H_RET, 1), (DEPTH, 2 * H_RET, SCAN_CHUNK)),
        ret_gn=ret_norm_g[:, None, :],
        ssd_cb=ssd_conv_b[:, None, :],
        ssd_dtb=lane_place(ssd_dt_bias), ssd_alog=lane_place(ssd_a_log),
        ssd_dskip=jnp.repeat(ssd_d, P_SSD, axis=-1)[:, None, :], ssd_gn=ssd_norm_g[:, None, :],
        lru_cb=lru_conv_b[:, None, :], lru_wg=wgate, lru_bg=bgate,
        wo=w_out.astype(BF16), wup=ffn_w_up.astype(BF16), ffn_cb=ffn_conv_b[:, None, :],
        wd=ffn_w_down.astype(BF16),
    )


def kernel(x_prompt, x_sample, c, c_ctx, state_ret, state_ssd, state_lru, w_ada, b_ada, norm1_g, norm2_g, w_in, ret_decay, ret_norm_g, ssd_conv_w, ssd_conv_b, ssd_dt_bias, ssd_a_log, ssd_d, ssd_norm_g, lru_conv_w, lru_conv_b, lru_w_a, lru_b_a, lru_w_x, lru_b_x, lru_lambda, w_out, ffn_w_up, ffn_conv_w, ffn_conv_b, ffn_w_down, final_norm_g):
    bp, tp, _ = x_prompt.shape
    bs, ts, _ = x_sample.shape
    assert tp == SCAN_CHUNK and ts == SEQ_TILE and ts % GRID_W == 0 and (bp * tp) % SEQ_TILE == 0

    n_req = 8
    cvec = jnp.concatenate([c_ctx[None, :], c, jnp.zeros((n_req - 1 - bs, D_MODEL), F32)], axis=0)
    mods = _modulation(cvec, w_ada, b_ada).reshape(DEPTH, n_req, 6, D_MODEL)

    pp = _prep_params(w_in, ret_decay, ret_norm_g, ssd_conv_b, ssd_dt_bias, ssd_a_log, ssd_d, ssd_norm_g,
                      lru_conv_b, lru_w_a, lru_b_a, lru_w_x, lru_b_x, w_out, ffn_w_up, ffn_conv_b, ffn_w_down)
    n1 = norm1_g[:, None, :]
    n2 = norm2_g[:, None, :]
    rope_tabs = _rope_tables(ts)
    fgain = final_norm_g[None, :]

    npair = H_RET // 2
    s0_ret = state_ret.reshape(bs, DEPTH, 2, npair, 2, DK_RET, DK_RET).transpose(0, 1, 2, 3, 5, 4, 6).reshape(
        bs, DEPTH, 2, npair, DK_RET, HEAD_PAIR)
    s0_ssd = state_ssd.transpose(0, 1, 2, 4, 3, 5).reshape(bs, DEPTH, 2, N_SSD, W_SSD)

    def trunk_layer(x2d, l, seq_len, row0, rows_per_mod, latent, prev=(None, None, None)):
        common = (x2d, mods, row0, rows_per_mod, n1, pp["w_all"])
        ret = _retention(*common, pp["dec_lane"], pp["dec_wide"], pp["ret_gn"], seq_len, l,
                         rope_tabs if latent else None, s0_ret if latent else None, prev[0])
        ssd = _ssd(*common, ssd_conv_w, pp["ssd_cb"], pp["ssd_dtb"], pp["ssd_alog"], pp["ssd_dskip"],
                   pp["ssd_gn"], seq_len, l, s0_ssd if latent else None, prev[1])
        lru = _lru(*common, lru_conv_w, pp["lru_cb"], pp["lru_wg"], pp["lru_bg"], lru_lambda, seq_len, l,
                   state_lru if latent else None, prev[2])
        x1, h2 = _out_proj(x2d, ret[0], ssd[0], lru[0], mods, row0, rows_per_mod, n2, pp["wo"], l)
        out = _ffn(h2, x1, mods, row0, rows_per_mod, pp["wup"], ffn_conv_w, pp["ffn_cb"], pp["wd"], fgain,
                   seq_len, l == DEPTH - 1, l)
        states = None if latent else (ret[1], ssd[1], lru[1])
        return out, states

    xp = x_prompt.reshape(bp * tp, D_MODEL)
    xs = x_sample.reshape(bs * ts, D_MODEL)
    states = (None, None, None)
    for l in range(DEPTH):
        xp, states = trunk_layer(xp, l, tp, 0, bp * tp, False, states)
        xs, _ = trunk_layer(xs, l, ts, 1, ts, True)

    return (xp.reshape(bp, tp, D_MODEL), xs.reshape(bs, ts, D_MODEL), states[0],
            jnp.swapaxes(states[1], -1, -2), states[2])
```

```python
import functools
import math

import jax
import jax.numpy as jnp
from jax import lax
from jax.experimental import pallas as pl
from jax.experimental.pallas import tpu as pltpu

F32 = jnp.float32
BF16 = jnp.bfloat16

D_MODEL = 1024
DEPTH = 2
GRID_W = 64
W_RET = 384
H_RET = 6
DK_RET = 64
W_SSD = 384
P_SSD = 64
H_SSD = 6
N_SSD = 128
G_SSD = 2
CONV_CH = W_SSD + 2 * G_SSD * N_SSD
W_LRU = 256
LRU_BLOCKS = 4
LRU_BW = 64
LRU_C = 8.0
D_FF = 2816
ROPE_BASE = 10000.0
EPS = 1e-6

LANES = 128
SUBLANES = 8
HEAD_PAIR = LANES
SCAN_CHUNK = 256
FF_BLOCK = 256
N_FF_BLOCKS = D_FF // FF_BLOCK
ROW_TILE = 512
SEQ_TILE = 1024
PROJ_COL_BLOCK = 512
VMEM_LIMIT = 56 * 1024 * 1024

W_GRP_R = 4 * W_RET
W_GRP_S = W_SSD + CONV_CH + LANES
W_GRP_L = 2 * W_LRU
W_SLOT_S = W_GRP_R
COL_R, COL_S, COL_L = 0, W_GRP_R, W_GRP_R + W_SLOT_S
W_IN_PAD = COL_L + W_GRP_L
DT_BWD_LANE = 8


def _dot(a, b):
    return jnp.dot(a, b, preferred_element_type=F32)


def _dot_nt(a, b):
    return lax.dot_general(a, b, (((1,), (1,)), ((), ())), preferred_element_type=F32)


def _sigmoid(x):
    return 1.0 / (1.0 + jnp.exp(-x))


def _silu(x):
    return x * _sigmoid(x)


def _softplus(x):
    return jnp.maximum(x, 0.0) + jnp.log1p(jnp.exp(-jnp.abs(x)))


def _log_sigmoid(x):
    return -_softplus(-x)


def _gelu_tanh(x):
    c = math.sqrt(2.0 / math.pi)
    return 0.5 * x * (1.0 + jnp.tanh(c * (x + 0.044715 * (x * x * x))))


def _rms_scale(x):
    return x * lax.rsqrt(jnp.mean(x * x, axis=-1, keepdims=True) + EPS)


def _const_spec(shape):
    zeros = (0,) * len(shape)
    return pl.BlockSpec(shape, lambda *_: zeros, pipeline_mode=pl.Buffered(1))


def _layer_spec(arr, l):
    rest = (0,) * (arr.ndim - 1)
    return pl.BlockSpec((1,) + arr.shape[1:], lambda *_: (l,) + rest, pipeline_mode=pl.Buffered(1))


def _w_in_spec(l, col0, width):
    assert col0 % width == 0
    return pl.BlockSpec((1, D_MODEL, width), lambda *_: (l, 0, col0 // width), pipeline_mode=pl.Buffered(1))


def _mod_spec(row0, rows_per_mod, tm, l):
    return pl.BlockSpec((1, 1, 6, D_MODEL), lambda i: (l, row0 + i * tm // rows_per_mod, 0, 0))


def _params(n_axes):
    return pltpu.CompilerParams(dimension_semantics=("arbitrary",) * n_axes,
                                vmem_limit_bytes=VMEM_LIMIT)


def _mod_body(c_ref, w_ref, b_ref, o_ref):
    s = _silu(c_ref[...]).astype(BF16)
    o_ref[0] = _dot(s, w_ref[0].astype(BF16)) + b_ref[0]


def _modulation(cvec, w_ada, b_ada):
    rows = cvec.shape[0]
    nblk = (6 * D_MODEL) // D_MODEL
    return pl.pallas_call(
        _mod_body,
        grid=(DEPTH, nblk),
        in_specs=[pl.BlockSpec((rows, D_MODEL), lambda l, j: (0, 0)),
                  pl.BlockSpec((1, D_MODEL, D_MODEL), lambda l, j: (l, 0, j)),
                  pl.BlockSpec((1, 1, D_MODEL), lambda l, j: (l, 0, j))],
        out_specs=pl.BlockSpec((1, rows, D_MODEL), lambda l, j: (l, 0, j)),
        out_shape=jax.ShapeDtypeStruct((DEPTH, rows, 6 * D_MODEL), F32),
        compiler_params=_params(2),
        name="adaln_mod",
    )(cvec, w_ada, b_ada.reshape(DEPTH, 1, 6 * D_MODEL))


IN_DIM = 4 * W_RET + W_SSD + CONV_CH + H_SSD + 2 * W_LRU
REGROUP_ROWS = 256


def _regroup_body(w_ref, o_ref):
    rows = w_ref.shape[1]
    o_dt = COL_S + W_SSD + CONV_CH
    o_ref[0, :, 0:W_RET] = (w_ref[0, :, 0:W_RET] * (DK_RET ** -0.5)).astype(BF16)
    o_ref[0, :, W_RET:o_dt] = w_ref[0, :, W_RET:o_dt].astype(BF16)
    tail = w_ref[0, :, o_dt:IN_DIM]
    first = tail[:, 0:LANES]
    lane = lax.broadcasted_iota(jnp.int32, (rows, LANES), 1)
    dt_fwd = jnp.where(lane < H_SSD, first, 0.0)
    dt_bwd = jnp.where((lane >= DT_BWD_LANE) & (lane < DT_BWD_LANE + H_SSD), pltpu.roll(first, DT_BWD_LANE, 1), 0.0)
    o_ref[0, :, o_dt:o_dt + LANES] = (dt_fwd + dt_bwd).astype(BF16)
    o_ref[0, :, o_dt + LANES:COL_L] = jnp.zeros((rows, COL_L - o_dt - LANES), BF16)
    o_ref[0, :, COL_L:W_IN_PAD] = tail[:, H_SSD:H_SSD + W_GRP_L].astype(BF16)


def _regroup_w_in(w_in):
    rb = REGROUP_ROWS
    return pl.pallas_call(
        _regroup_body,
        grid=(DEPTH, D_MODEL // rb),
        in_specs=[pl.BlockSpec((1, rb, IN_DIM), lambda l, i: (l, i, 0))],
        out_specs=pl.BlockSpec((1, rb, W_IN_PAD), lambda l, i: (l, i, 0)),
        out_shape=jax.ShapeDtypeStruct((DEPTH, D_MODEL, W_IN_PAD), BF16),
        compiler_params=_params(2),
        name="w_in_regroup",
    )(w_in)


def _project(x_ref, mod_ref, g_ref, w_ref, p_ref, rows=slice(None)):
    y = _rms_scale(x_ref[rows, :]) * g_ref[0]
    h = (y * (1.0 + mod_ref[0, 0, 1:2, :]) + mod_ref[0, 0, 0:1, :]).astype(BF16)
    width = p_ref.shape[1]
    for j in range(0, width, PROJ_COL_BLOCK):
        jb = min(PROJ_COL_BLOCK, width - j)
        p_ref[rows, j:j + jb] = _dot(h, w_ref[0, :, j:j + jb])


def _shift_rows(x, k, pos, seq_len):
    n = x.shape[0]
    if k == 0:
        return x
    rolled = pltpu.roll(x, (-k) % n, 0)
    ok = (pos >= -k) if k < 0 else (pos <= seq_len - 1 - k)
    return jnp.where(ok, rolled, 0.0)


def _dwconv4(x, w_ref, b_ref, col0, pos, seq_len):
    width = x.shape[1]
    cols = slice(col0, col0 + width)
    acc = x * w_ref[0, 1:2, cols] + b_ref[0, 0:1, cols]
    acc = acc + _shift_rows(x, -1, pos, seq_len) * w_ref[0, 0:1, cols]
    acc = acc + _shift_rows(x, 1, pos, seq_len) * w_ref[0, 2:3, cols]
    acc = acc + _shift_rows(x, 2, pos, seq_len) * w_ref[0, 3:4, cols]
    return acc


def _zero_later_layers(st_ref):
    st_ref[:, 1:] = jnp.zeros((st_ref.shape[0], DEPTH - 1) + tuple(st_ref.shape[2:]), F32)


def _state_output(nseq_total, nseq_tile, tail, l, prev, n_inputs):
    zeros = (0,) * len(tail)
    shape = jax.ShapeDtypeStruct((nseq_total, DEPTH) + tail, F32)
    if l == 0:
        return pl.BlockSpec((nseq_tile, DEPTH) + tail, lambda i: (i, 0) + zeros), shape, [], [], {}
    spec = pl.BlockSpec((nseq_tile, 1) + tail, lambda i: (i, l) + zeros)
    return spec, shape, [pl.BlockSpec(memory_space=pl.ANY)], [prev], {n_inputs: 1}


def _head_lane_expand(x, lane0):
    rows = x.shape[0]
    low = lax.broadcasted_iota(jnp.int32, (rows, HEAD_PAIR), 1) < DK_RET
    blocks = []
    for p in range(H_RET // 2):
        a = jnp.broadcast_to(x[:, lane0 + 2 * p:lane0 + 2 * p + 1], (rows, HEAD_PAIR))
        b = jnp.broadcast_to(x[:, lane0 + 2 * p + 1:lane0 + 2 * p + 2], (rows, HEAD_PAIR))
        blocks.append(jnp.where(low, a, b))
    return jnp.concatenate(blocks, axis=1)


def _seq_call(body, name, x2d, mods, row0, rows_per_mod, gain, w_all, col0, width, l, seq_len, params,
              extra_in, y_width, state_tail, prev_states, scratch):
    n = x2d.shape[0]
    tm = SEQ_TILE
    nseq_tile = tm // seq_len
    emit_state = state_tail is not None
    in_specs = [pl.BlockSpec((tm, D_MODEL), lambda i: (i, 0)), _mod_spec(row0, rows_per_mod, tm, l),
                _layer_spec(gain, l), _w_in_spec(l, col0, width)]
    in_specs += [_layer_spec(a, l) for a in params]
    in_specs += [spec for _, spec in extra_in]
    args = [x2d, mods, gain, w_all] + list(params) + [a for a, _ in extra_in]
    out_specs = [pl.BlockSpec((tm, y_width), lambda i: (i, 0))]
    out_shape = [jax.ShapeDtypeStruct((n, y_width), BF16)]
    aliases = {}
    if emit_state:
        spec, shape, extra_specs, extra_args, aliases = _state_output(
            n // seq_len, nseq_tile, state_tail, l, prev_states, len(args))
        out_specs.append(spec)
        out_shape.append(shape)
        in_specs += extra_specs
        args += extra_args
    return pl.pallas_call(
        functools.partial(body, seq_len=seq_len, emit_state=emit_state, first_layer=l == 0),
        grid=(n // tm,), in_specs=in_specs, out_specs=out_specs, out_shape=out_shape,
        input_output_aliases=aliases, scratch_shapes=scratch, compiler_params=_params(1), name=name,
    )(*args)


def _ret_body(*refs, seq_len, rope, has_s0, emit_state, first_layer):
    it = iter(refs)
    x_ref, mod_ref, g_ref, w_ref = next(it), next(it), next(it), next(it)
    dec_lane_ref = next(it)
    dec_wide_ref = next(it)
    gn_ref = next(it)
    cos_ref = sin_ref = s0_ref = st_ref = qk_ref = None
    if rope:
        cos_ref = next(it)
        sin_ref = next(it)
    if has_s0:
        s0_ref = next(it)
    if emit_state and not first_layer:
        next(it)
    y_ref = next(it)
    if emit_state:
        st_ref = next(it)
    p_ref = next(it)
    dmat_ref = next(it)
    wts_ref = next(it)
    if rope:
        qk_ref = next(it)

    C = SCAN_CHUNK
    nch = seq_len // C
    nseq = p_ref.shape[0] // seq_len
    npair = H_RET // 2
    assert not (has_s0 or rope) or nseq == 1
    if nseq == 1:
        _project(x_ref, mod_ref, g_ref, w_ref, p_ref)
    lgl = _log_sigmoid(dec_lane_ref[0])

    @pl.when(pl.program_id(0) == 0)
    def _():
        lgw = _log_sigmoid(dec_wide_ref[0])
        ii = lax.broadcasted_iota(jnp.int32, (C, C), 0)
        mm = lax.broadcasted_iota(jnp.int32, (C, C), 1)
        dif = (ii - mm).astype(F32)
        neg_inf = -jnp.inf
        for h in range(H_RET):
            df = jnp.exp(jnp.where(ii >= mm, dif * lgw[h:h + 1, :], neg_inf))
            db = jnp.exp(jnp.where(mm >= ii, (-dif) * lgw[H_RET + h:H_RET + h + 1, :], neg_inf))
            dmat_ref[h] = df + db
        ri = lax.broadcasted_iota(jnp.int32, (C, W_RET), 0).astype(F32)
        wts_ref[0] = jnp.exp((C - 1.0 - ri) * lgl[0:1, :])
        wts_ref[1] = jnp.exp(ri * lgl[1:2, :])
        wts_ref[2] = jnp.exp((ri + 1.0) * lgl[0:1, :])
        wts_ref[3] = jnp.exp((C - ri) * lgl[1:2, :])

    lane = lax.broadcasted_iota(jnp.int32, (C, HEAD_PAIR), 1)
    low = lane < DK_RET
    r128 = lax.broadcasted_iota(jnp.int32, (HEAD_PAIR, HEAD_PAIR), 0) >= DK_RET
    c128 = lax.broadcasted_iota(jnp.int32, (HEAD_PAIR, HEAD_PAIR), 1) >= DK_RET
    same_head = r128 == c128
    avg = jnp.where(same_head, 1.0 / DK_RET, 0.0).astype(BF16)

    need_states = emit_state or (has_s0 and nch > 1)
    if has_s0:
        dec_f = jnp.exp(float(C) * lgl[0:1, :])
        dec_b = jnp.exp(float(C) * lgl[1:2, :])

    if rope:
        swap_low = (lax.broadcasted_iota(jnp.int32, (seq_len, HEAD_PAIR), 1) % 32) < 16
        cos = cos_ref[...]
        sin = sin_ref[...]
        for blk in range(2 * npair):
            cols_b = slice(blk * HEAD_PAIR, (blk + 1) * HEAD_PAIR)
            x = p_ref[:, cols_b]
            swapped = jnp.where(swap_low, pltpu.roll(x, HEAD_PAIR - 16, 1), pltpu.roll(x, 16, 1))
            qk_ref[:, cols_b] = x * cos + swapped * sin

    def cols(p):
        return slice(p * HEAD_PAIR, (p + 1) * HEAD_PAIR)

    if emit_state and first_layer:
        _zero_later_layers(st_ref)

    def project_seq(s):
        _project(x_ref, mod_ref, g_ref, w_ref, p_ref, slice(s * seq_len, (s + 1) * seq_len))

    if nseq > 1:
        project_seq(0)
    for s in range(nseq):
        if s + 1 < nseq:
            project_seq(s + 1)

        def rows(c, s=s):
            return slice(s * seq_len + c * C, s * seq_len + (c + 1) * C)

        def get_q(c, p):
            src = qk_ref if rope else p_ref
            return src[rows(c), p * HEAD_PAIR:(p + 1) * HEAD_PAIR]

        def get_k(c, p):
            src = qk_ref if rope else p_ref
            return src[rows(c), W_RET + p * HEAD_PAIR:W_RET + (p + 1) * HEAD_PAIR]

        def get_v(c, p):
            return p_ref[rows(c), 2 * W_RET + p * HEAD_PAIR:2 * W_RET + (p + 1) * HEAD_PAIR]

        ds_f = [[None] * npair for _ in range(nch)]
        ds_b = [[None] * npair for _ in range(nch)]
        if need_states:
            for c in range(nch):
                for p in range(npair):
                    k = get_k(c, p)
                    vb = get_v(c, p).astype(BF16)
                    kf = jnp.transpose(k * wts_ref[0, :, cols(p)]).astype(BF16)
                    kb = jnp.transpose(k * wts_ref[1, :, cols(p)]).astype(BF16)
                    ds_f[c][p] = _dot(kf, vb)
                    ds_b[c][p] = _dot(kb, vb)

        if emit_state:
            for p in range(npair):
                for a in range(2):
                    blk = slice(a * DK_RET, (a + 1) * DK_RET)
                    st_ref[s, 0, 0, 2 * p + a] = ds_f[0][p][blk, blk]
                    st_ref[s, 0, 1, 2 * p + a] = ds_b[0][p][blk, blk]

        sf_in = [[None] * npair for _ in range(nch)]
        sb_in = [[None] * npair for _ in range(nch)]
        if has_s0:
            low_half = lax.broadcasted_iota(jnp.int32, (DK_RET, HEAD_PAIR), 1) < DK_RET

            def pair_state(d, p):
                u = s0_ref[0, 0, d, p]
                return jnp.concatenate([jnp.where(low_half, u, 0.0), jnp.where(low_half, 0.0, u)], axis=0)

            for p in range(npair):
                st = pair_state(0, p)
                for c in range(nch):
                    sf_in[c][p] = st
                    if c + 1 < nch:
                        st = st * dec_f[:, cols(p)] + jnp.where(same_head, ds_f[c][p], 0.0)
                st = pair_state(1, p)
                for c in reversed(range(nch)):
                    sb_in[c][p] = st
                    if c > 0:
                        st = st * dec_b[:, cols(p)] + jnp.where(same_head, ds_b[c][p], 0.0)

        for c in range(nch):
            pairs = range(npair)
            q = [get_q(c, p) for p in pairs]
            kb = [get_k(c, p).astype(BF16) for p in pairs]
            vb = [get_v(c, p).astype(BF16) for p in pairs]
            scores = []
            for p in pairs:
                scores.append(_dot_nt(jnp.where(low, q[p], 0.0).astype(BF16), kb[p]))
                scores.append(_dot_nt(jnp.where(low, 0.0, q[p]).astype(BF16), kb[p]))
            inter = [None] * npair
            if has_s0:
                for p in pairs:
                    inter[p] = (_dot((q[p] * wts_ref[2, :, cols(p)]).astype(BF16), sf_in[c][p].astype(BF16))
                                + _dot((q[p] * wts_ref[3, :, cols(p)]).astype(BF16), sb_in[c][p].astype(BF16)))
            o = []
            for p in pairs:
                o0 = _dot((scores[2 * p] * dmat_ref[2 * p]).astype(BF16), vb[p])
                o1 = _dot((scores[2 * p + 1] * dmat_ref[2 * p + 1]).astype(BF16), vb[p])
                op = jnp.where(low, o0, o1)
                o.append(op + inter[p] if has_s0 else op)
            mu = [_dot(o[p].astype(BF16), avg) for p in pairs]
            d = [o[p] - mu[p] for p in pairs]
            var = [_dot((d[p] * d[p]).astype(BF16), avg) for p in pairs]
            for p in pairs:
                on = d[p] * lax.rsqrt(var[p] + EPS)
                g = p_ref[rows(c), 3 * W_RET + p * HEAD_PAIR:3 * W_RET + (p + 1) * HEAD_PAIR]
                y_ref[rows(c), cols(p)] = (on * gn_ref[0, 0:1, cols(p)] * _silu(g)).astype(BF16)


def _retention(x2d, mods, row0, rows_per_mod, gain, w_all, dec_lane, dec_wide, gn, seq_len, l,
               rope_tabs=None, s0=None, prev_states=None):
    rope = rope_tabs is not None
    has_s0 = s0 is not None
    npair = H_RET // 2
    extra = []
    if rope:
        extra += [(t, _const_spec(t.shape)) for t in rope_tabs]
    if has_s0:
        extra.append((s0, pl.BlockSpec((1, 1, 2, npair, DK_RET, HEAD_PAIR), lambda i: (i, l, 0, 0, 0, 0))))
    scratch = [pltpu.VMEM((SEQ_TILE, W_GRP_R), F32),
               pltpu.VMEM((H_RET, SCAN_CHUNK, SCAN_CHUNK), F32), pltpu.VMEM((4, SCAN_CHUNK, W_RET), F32)]
    if rope:
        scratch.append(pltpu.VMEM((seq_len, 2 * W_RET), F32))
    return _seq_call(functools.partial(_ret_body, rope=rope, has_s0=has_s0),
                     "retention_lat" if rope else "retention_ctx",
                     x2d, mods, row0, rows_per_mod, gain, w_all, COL_R, W_GRP_R, l, seq_len,
                     (dec_lane, dec_wide, gn), extra, W_RET,
                     None if has_s0 else (2, H_RET, DK_RET, DK_RET), prev_states, scratch)


def _ssd_body(*refs, seq_len, has_s0, emit_state, first_layer):
    it = iter(refs)
    x_ref, mod_ref, g_ref, w_ref = next(it), next(it), next(it), next(it)
    cw_ref = next(it)
    cb_ref = next(it)
    dtb_ref = next(it)
    alog_ref = next(it)
    dskip_ref = next(it)
    gn_ref = next(it)
    s0_ref = st_ref = None
    if has_s0:
        s0_ref = next(it)
    if emit_state and not first_layer:
        next(it)
    y_ref = next(it)
    if emit_state:
        st_ref = next(it)
    p_ref = next(it)
    xs_ref = next(it)
    bc_ref = next(it)

    T = seq_len
    C = SCAN_CHUNK
    nch = T // C
    n_rows = p_ref.shape[0]
    nseq = n_rows // T
    assert not has_s0 or nseq == 1
    assert T & (T - 1) == 0
    pos = lax.broadcasted_iota(jnp.int32, (T, LANES), 0)

    def project_and_conv(s):
        seq_rows = slice(s * T, (s + 1) * T)
        _project(x_ref, mod_ref, g_ref, w_ref, p_ref, seq_rows)
        for blk in range(CONV_CH // LANES):
            x = p_ref[seq_rows, W_SSD + blk * LANES:W_SSD + (blk + 1) * LANES]
            a = _silu(_dwconv4(x, cw_ref, cb_ref, blk * LANES, pos, T))
            if blk < W_SSD // LANES:
                xs_ref[seq_rows, blk * LANES:(blk + 1) * LANES] = a
            else:
                o = blk * LANES - W_SSD
                bc_ref[seq_rows, o:o + LANES] = a

    a_neg = -jnp.exp(alog_ref[0])
    ii = lax.broadcasted_iota(jnp.int32, (C, C), 0)
    mm = lax.broadcasted_iota(jnp.int32, (C, C), 1)
    lower = ii >= mm
    upper = mm >= ii
    tri_l = jnp.where(lower, 1.0, 0.0).astype(BF16)
    tri_u = jnp.where(upper, 1.0, 0.0).astype(BF16)
    neg_inf = -jnp.inf
    grp0 = lax.broadcasted_iota(jnp.int32, (C, W_SSD), 1) < (W_SSD // G_SSD)
    grp0_s = lax.broadcasted_iota(jnp.int32, (N_SSD, W_SSD), 1) < (W_SSD // G_SSD)
    grp0_t = lax.broadcasted_iota(jnp.int32, (W_SSD, N_SSD), 0) < (W_SSD // G_SSD)
    low = lax.broadcasted_iota(jnp.int32, (C, HEAD_PAIR), 1) < P_SSD

    if emit_state and first_layer:
        _zero_later_layers(st_ref)

    scalars = {}

    def chunk_scalars(s, c):
        if (s, c) not in scalars:
            rows = slice(s * T + c * C, s * T + (c + 1) * C)
            dt = _softplus(p_ref[rows, W_SSD + CONV_CH:W_SSD + CONV_CH + LANES] + dtb_ref[0])
            la = dt * a_neg
            a1 = la.astype(BF16)
            r1 = la - a1.astype(F32)
            a2 = r1.astype(BF16)
            a3 = (r1 - a2.astype(F32)).astype(BF16)
            pre = _dot(tri_l, a1) + _dot(tri_l, a2) + _dot(tri_l, a3)
            suf = _dot(tri_u, a1) + _dot(tri_u, a2) + _dot(tri_u, a3)
            scalars[(s, c)] = (dt, pre, suf)
        return scalars[(s, c)]

    def light_stage(s):
        def rows(c):
            return slice(s * T + c * C, s * T + (c + 1) * C)

        def state_weights(c):
            dt, pre, suf = chunk_scalars(s, c)
            wf = jnp.exp(pre[C - 1:C, :] - pre) * dt
            wb = jnp.exp(suf[0:1, :] - suf) * dt
            xs = xs_ref[rows(c), :]
            return (xs * _head_lane_expand(wf, 0), xs * _head_lane_expand(wb, DT_BWD_LANE), pre, suf)

        if emit_state:
            xf, xb, _, _ = state_weights(0)
            bm = [bc_ref[rows(0), g * N_SSD:(g + 1) * N_SSD].astype(BF16) for g in range(G_SSD)]
            for d, xw in enumerate((xf, xb)):
                xt = jnp.transpose(xw).astype(BF16)
                st = jnp.where(grp0_t, _dot(xt, bm[0]), _dot(xt, bm[1]))
                for h in range(H_SSD):
                    st_ref[s, 0, d, h] = st[h * P_SSD:(h + 1) * P_SSD, :]

        ds_f = [None] * nch
        ds_b = [None] * nch
        dec_f = [None] * nch
        dec_b = [None] * nch
        if has_s0 and nch > 1:
            for c in range(nch):
                xf, xb, pre, suf = state_weights(c)
                bt0 = jnp.transpose(bc_ref[rows(c), 0:N_SSD]).astype(BF16)
                bt1 = jnp.transpose(bc_ref[rows(c), N_SSD:2 * N_SSD]).astype(BF16)
                ds_f[c] = jnp.where(grp0_s, _dot(bt0, xf.astype(BF16)), _dot(bt1, xf.astype(BF16)))
                ds_b[c] = jnp.where(grp0_s, _dot(bt0, xb.astype(BF16)), _dot(bt1, xb.astype(BF16)))
                dec_f[c] = _head_lane_expand(jnp.exp(pre[C - 1:C, :]), 0)
                dec_b[c] = _head_lane_expand(jnp.exp(suf[0:1, :]), DT_BWD_LANE)

        sf_in = [None] * nch
        sb_in = [None] * nch
        if has_s0:
            st = s0_ref[0, 0, 0]
            for c in range(nch):
                sf_in[c] = st
                if c + 1 < nch:
                    st = st * dec_f[c] + ds_f[c]
            st = s0_ref[0, 0, 1]
            for c in reversed(range(nch)):
                sb_in[c] = st
                if c > 0:
                    st = st * dec_b[c] + ds_b[c]

        staged = []
        for c in range(nch):
            dt, pre, suf = chunk_scalars(s, c)
            cm = [bc_ref[rows(c), 2 * N_SSD + g * N_SSD:2 * N_SSD + (g + 1) * N_SSD].astype(BF16)
                  for g in range(G_SSD)]
            bm = [bc_ref[rows(c), g * N_SSD:(g + 1) * N_SSD].astype(BF16) for g in range(G_SSD)]
            gram = [_dot_nt(cm[g], bm[g]) for g in range(G_SSD)]
            inter = None
            if has_s0:
                sf = sf_in[c].astype(BF16)
                sb = sb_in[c].astype(BF16)
                inter = (jnp.where(grp0, _dot(cm[0], sf), _dot(cm[1], sf)),
                         jnp.where(grp0, _dot(cm[0], sb), _dot(cm[1], sb)))
            staged.append((dt, pre, suf, jnp.transpose(pre), jnp.transpose(suf), jnp.transpose(dt), gram, inter))
        return staged

    def heavy_stage(s, staged):
        for c in range(nch):
            rows_c = slice(s * T + c * C, s * T + (c + 1) * C)
            dt, pre, suf, pre_t, suf_t, dt_t, gram, inter = staged[c]
            xs = xs_ref[rows_c, :]
            xs_b = xs.astype(BF16)
            heads = []
            for h in range(H_SSD):
                g = h // (H_SSD // G_SSD)
                hb = DT_BWD_LANE + h
                df = jnp.exp(jnp.where(lower, pre[:, h:h + 1] - pre_t[h:h + 1, :], neg_inf))
                db = jnp.exp(jnp.where(upper, suf[:, hb:hb + 1] - suf_t[hb:hb + 1, :], neg_inf))
                w = gram[g] * (df * dt_t[h:h + 1, :] + db * dt_t[hb:hb + 1, :])
                p = h // 2
                heads.append(_dot(w.astype(BF16), xs_b[:, p * HEAD_PAIR:(p + 1) * HEAD_PAIR]))
            o = jnp.concatenate([jnp.where(low, heads[2 * p], heads[2 * p + 1]) for p in range(H_SSD // 2)],
                                axis=1)
            if has_s0:
                o = o + _head_lane_expand(jnp.exp(pre), 0) * inter[0]
                o = o + _head_lane_expand(jnp.exp(suf), DT_BWD_LANE) * inter[1]
            y = o + dskip_ref[0] * xs
            yz = y * _silu(p_ref[rows_c, 0:W_SSD])
            y_ref[rows_c, :] = (_rms_scale(yz) * gn_ref[0]).astype(BF16)

    project_and_conv(0)
    staged = light_stage(0)
    for s in range(nseq):
        staged_next = None
        if s + 1 < nseq:
            project_and_conv(s + 1)
            staged_next = light_stage(s + 1)
        heavy_stage(s, staged)
        staged = staged_next


def _ssd(x2d, mods, row0, rows_per_mod, gain, w_all, cw, cb, dtb, alog, dskip, gn, seq_len, l,
         s0=None, prev_states=None):
    has_s0 = s0 is not None
    extra = []
    if has_s0:
        extra.append((s0, pl.BlockSpec((1, 1, 2, N_SSD, W_SSD), lambda i: (i, l, 0, 0, 0))))
    scratch = [pltpu.VMEM((SEQ_TILE, W_GRP_S), F32), pltpu.VMEM((SEQ_TILE, W_SSD), F32),
               pltpu.VMEM((SEQ_TILE, 2 * G_SSD * N_SSD), F32)]
    return _seq_call(functools.partial(_ssd_body, has_s0=has_s0), "ssd_lat" if has_s0 else "ssd_ctx",
                     x2d, mods, row0, rows_per_mod, gain, w_all, COL_S, W_SLOT_S, l, seq_len,
                     (cw, cb, dtb, alog, dskip, gn), extra, W_SSD,
                     None if has_s0 else (2, H_SSD, P_SSD, N_SSD), prev_states, scratch)


def _lru_body(*refs, seq_len, has_h0, emit_state, first_layer):
    it = iter(refs)
    x_ref, mod_ref, g_ref, w_ref = next(it), next(it), next(it), next(it)
    cw_ref = next(it)
    cb_ref = next(it)
    wg_ref = next(it)
    bg_ref = next(it)
    lam_ref = next(it)
    h0_ref = st_ref = None
    if has_h0:
        h0_ref = next(it)
    if emit_state and not first_layer:
        next(it)
    y_ref = next(it)
    if emit_state:
        st_ref = next(it)
    p_ref = next(it)

    T = seq_len
    n_rows = p_ref.shape[0]
    nseq = n_rows // T
    assert not has_h0 or nseq == 1
    assert T & (T - 1) == 0
    _project(x_ref, mod_ref, g_ref, w_ref, p_ref)
    row = lax.broadcasted_iota(jnp.int32, (n_rows, W_LRU), 0)
    pos = row & (T - 1)
    sub = row & (SUBLANES - 1)
    xc = _dwconv4(p_ref[:, 0:W_LRU], cw_ref, cb_ref, 0, pos, T)
    gates = _sigmoid(_dot(xc.astype(BF16), wg_ref[0]) + bg_ref[0])
    decay_rate = _softplus(-lam_ref[0])
    if emit_state and first_layer:
        _zero_later_layers(st_ref)
    nblk = T // SUBLANES
    total = None
    for d in range(2):
        r = gates[:, 2 * d * W_LRU:(2 * d + 1) * W_LRU]
        i = gates[:, (2 * d + 1) * W_LRU:(2 * d + 2) * W_LRU]
        log_a = (-LRU_C * r) * decay_rate[d:d + 1, :]
        a = jnp.exp(log_a)
        th = jnp.tanh(log_a)
        b = jnp.sqrt(-2.0 * th / (1.0 - th)) * (i * xc)
        step = 1
        while step < SUBLANES:
            if d == 0:
                ok = sub >= step
                shift = step
            else:
                ok = sub <= SUBLANES - 1 - step
                shift = n_rows - step
            a_prev = jnp.where(ok, pltpu.roll(a, shift, 0), 1.0)
            b_prev = jnp.where(ok, pltpu.roll(b, shift, 0), 0.0)
            b = a * b_prev + b
            a = a * a_prev
            step *= 2
        edge = SUBLANES - 1 if d == 0 else 0
        tiles = [None] * (nseq * nblk)
        for s in range(nseq):
            carry = h0_ref[0, 0, d:d + 1, :] if has_h0 else None
            for v in (range(nblk) if d == 0 else reversed(range(nblk))):
                t = s * nblk + v
                rows = slice(t * SUBLANES, (t + 1) * SUBLANES)
                hv = b[rows]
                if carry is not None:
                    hv = hv + a[rows] * carry
                tiles[t] = hv
                carry = hv[edge:edge + 1, :]
            if emit_state:
                st_ref[s, 0, d:d + 1, :] = carry
        hs = jnp.concatenate(tiles, axis=0)
        total = hs if total is None else total + hs
    y_ref[...] = (total * _gelu_tanh(p_ref[:, W_LRU:2 * W_LRU])).astype(BF16)


def _lru(x2d, mods, row0, rows_per_mod, gain, w_all, cw, cb, wg, bg, lam, seq_len, l, h0=None, prev_states=None):
    has_h0 = h0 is not None
    extra = []
    if has_h0:
        extra.append((h0, pl.BlockSpec((1, 1, 2, W_LRU), lambda i: (i, l, 0, 0))))
    scratch = [pltpu.VMEM((SEQ_TILE, W_GRP_L), F32)]
    return _seq_call(functools.partial(_lru_body, has_h0=has_h0), "lru_lat" if has_h0 else "lru_ctx",
                     x2d, mods, row0, rows_per_mod, gain, w_all, COL_L, W_GRP_L, l, seq_len,
                     (cw, cb, wg, bg, lam), extra, W_LRU,
                     None if has_h0 else (2, W_LRU), prev_states, scratch)


def _outproj_body(x_ref, yr_ref, ys_ref, yl_ref, mod_ref, g_ref, wo_ref, x1_ref, h2_ref):
    mix = _dot(yr_ref[...], wo_ref[0, 0:W_RET, :])
    mix = mix + _dot(ys_ref[...], wo_ref[0, W_RET:W_RET + W_SSD, :])
    mix = mix + _dot(yl_ref[...], wo_ref[0, W_RET + W_SSD:D_MODEL, :])
    x1 = x_ref[...] + mod_ref[0, 0, 2:3, :] * mix
    x1_ref[...] = x1
    y = _rms_scale(x1) * g_ref[0]
    h2_ref[...] = (y * (1.0 + mod_ref[0, 0, 4:5, :]) + mod_ref[0, 0, 3:4, :]).astype(BF16)


def _out_proj(x2d, y_ret, y_ssd, y_lru, mods, row0, rows_per_mod, gain, wo, l):
    n = x2d.shape[0]
    tm = ROW_TILE
    row_spec = lambda w: pl.BlockSpec((tm, w), lambda i: (i, 0))
    return pl.pallas_call(
        _outproj_body,
        grid=(n // tm,),
        in_specs=[row_spec(D_MODEL), row_spec(W_RET), row_spec(W_SSD), row_spec(W_LRU),
                  _mod_spec(row0, rows_per_mod, tm, l), _layer_spec(gain, l), _layer_spec(wo, l)],
        out_specs=[row_spec(D_MODEL), row_spec(D_MODEL)],
        out_shape=[jax.ShapeDtypeStruct((n, D_MODEL), F32), jax.ShapeDtypeStruct((n, D_MODEL), BF16)],
        compiler_params=_params(1),
        name="out_proj",
    )(x2d, y_ret, y_ssd, y_lru, mods, gain, wo)


def _ffn_core(h, wup_ref, cw_ref, cb_ref, wd_ref, act_ref, seq_len):
    tm = h.shape[0]
    nseq = tm // seq_len
    row8 = lax.broadcasted_iota(jnp.int32, (SUBLANES, FF_BLOCK), 0)
    first_row = row8 == 0
    last_row = row8 == SUBLANES - 1

    def conv3(u, col0):
        cols = slice(col0, col0 + FF_BLOCK)
        w_prev, w_mid, w_next = (cw_ref[0, t:t + 1, cols] for t in range(3))
        bias = cb_ref[0, 0:1, cols]
        outs = []
        for s in range(nseq):
            us = u[s * seq_len:(s + 1) * seq_len]
            prev = pltpu.roll(us, 1, 0)
            nxt = pltpu.roll(us, seq_len - 1, 0)
            prev = jnp.concatenate([jnp.where(first_row, 0.0, prev[:SUBLANES]), prev[SUBLANES:]], axis=0)
            nxt = jnp.concatenate([nxt[:seq_len - SUBLANES], jnp.where(last_row, 0.0, nxt[seq_len - SUBLANES:])],
                                  axis=0)
            outs.append(us * w_mid + bias + prev * w_prev + nxt * w_next)
        return outs[0] if nseq == 1 else jnp.concatenate(outs, axis=0)

    def up(j):
        return (_dot(h, wup_ref[0, :, j * FF_BLOCK:(j + 1) * FF_BLOCK]),
                _dot(h, wup_ref[0, :, D_FF + j * FF_BLOCK:D_FF + (j + 1) * FF_BLOCK]))

    for j in range(N_FF_BLOCKS):
        uv, ug = up(j)
        val = conv3(uv, j * FF_BLOCK)
        gate = conv3(ug, D_FF + j * FF_BLOCK)
        act_ref[:, j * FF_BLOCK:(j + 1) * FF_BLOCK] = (_silu(gate) * val).astype(BF16)
    return _dot(act_ref[...], wd_ref[0])


def _ffn_body(h_ref, x1_ref, mod_ref, wup_ref, cw_ref, cb_ref, wd_ref, fg_ref, o_ref, act_ref, *, seq_len,
              final_norm):
    down = _ffn_core(h_ref[...], wup_ref, cw_ref, cb_ref, wd_ref, act_ref, seq_len)
    out = x1_ref[...] + mod_ref[0, 0, 5:6, :] * down
    if final_norm:
        out = _rms_scale(out) * fg_ref[...]
    o_ref[...] = out


def _mix_ffn_body(x_ref, yr_ref, ys_ref, yl_ref, mod_ref, g_ref, wo_ref, wup_ref, cw_ref, cb_ref, wd_ref, fg_ref,
                  o_ref, act_ref, h_ref, *, seq_len, final_norm):
    mix = _dot(yr_ref[...], wo_ref[0, 0:W_RET, :])
    mix = mix + _dot(ys_ref[...], wo_ref[0, W_RET:W_RET + W_SSD, :])
    mix = mix + _dot(yl_ref[...], wo_ref[0, W_RET + W_SSD:D_MODEL, :])
    x1 = x_ref[...] + mod_ref[0, 0, 2:3, :] * mix
    o_ref[...] = x1
    y = _rms_scale(x1) * g_ref[0]
    h_ref[...] = (y * (1.0 + mod_ref[0, 0, 4:5, :]) + mod_ref[0, 0, 3:4, :]).astype(BF16)
    down = _ffn_core(h_ref[...], wup_ref, cw_ref, cb_ref, wd_ref, act_ref, seq_len)
    out = o_ref[...] + mod_ref[0, 0, 5:6, :] * down
    if final_norm:
        out = _rms_scale(out) * fg_ref[...]
    o_ref[...] = out


def _mix_ffn(x2d, y_ret, y_ssd, y_lru, mods, row0, rows_per_mod, gain, wo, wup, cw, cb, wd, fgain, seq_len,
             final_norm, l):
    n = x2d.shape[0]
    tm = ROW_TILE
    assert tm % seq_len == 0
    row_spec = lambda w: pl.BlockSpec((tm, w), lambda i: (i, 0))
    return pl.pallas_call(
        functools.partial(_mix_ffn_body, seq_len=seq_len, final_norm=final_norm),
        grid=(n // tm,),
        in_specs=[row_spec(D_MODEL), row_spec(W_RET), row_spec(W_SSD), row_spec(W_LRU),
                  _mod_spec(row0, rows_per_mod, tm, l), _layer_spec(gain, l), _layer_spec(wo, l),
                  _layer_spec(wup, l), _layer_spec(cw, l), _layer_spec(cb, l), _layer_spec(wd, l),
                  _const_spec(fgain.shape)],
        out_specs=row_spec(D_MODEL),
        out_shape=jax.ShapeDtypeStruct((n, D_MODEL), F32),
        scratch_shapes=[pltpu.VMEM((tm, D_FF), BF16), pltpu.VMEM((tm, D_MODEL), BF16)],
        compiler_params=_params(1),
        name="mix_ffn",
    )(x2d, y_ret, y_ssd, y_lru, mods, gain, wo, wup, cw, cb, wd, fgain)


def _ffn(h2, x1, mods, row0, rows_per_mod, wup, cw, cb, wd, fgain, seq_len, final_norm, l):
    n = h2.shape[0]
    tm = SEQ_TILE
    row_spec = pl.BlockSpec((tm, D_MODEL), lambda i: (i, 0))
    return pl.pallas_call(
        functools.partial(_ffn_body, seq_len=seq_len, final_norm=final_norm),
        grid=(n // tm,),
        in_specs=[row_spec, row_spec, _mod_spec(row0, rows_per_mod, tm, l),
                  _layer_spec(wup, l), _layer_spec(cw, l), _layer_spec(cb, l), _layer_spec(wd, l),
                  _const_spec(fgain.shape)],
        out_specs=row_spec,
        out_shape=jax.ShapeDtypeStruct((n, D_MODEL), F32),
        scratch_shapes=[pltpu.VMEM((tm, D_FF), BF16)],
        compiler_params=_params(1),
        name="ffn",
    )(h2, x1, mods, wup, cw, cb, wd, fgain)


def _rope_tables(t_len):
    tok = jnp.arange(t_len)
    row_pos = (tok // GRID_W).astype(F32)
    col_pos = (tok % GRID_W).astype(F32)
    lane = jnp.arange(HEAD_PAIR)
    d = lane % DK_RET
    use_row = d < (DK_RET // 2)
    half = DK_RET // 4
    freqs = ROPE_BASE ** (-(d % half).astype(F32) / half)
    ang = jnp.where(use_row[None, :], row_pos[:, None], col_pos[:, None]) * freqs[None, :]
    first = (d % (2 * half)) < half
    return jnp.cos(ang), jnp.where(first[None, :], -jnp.sin(ang), jnp.sin(ang))


def _prep_params(w_in, ret_decay, ret_norm_g, ssd_conv_b, ssd_dt_bias, ssd_a_log, ssd_d, ssd_norm_g,
                 lru_conv_b, lru_w_a, lru_b_a, lru_w_x, lru_b_x, w_out, ffn_w_up, ffn_conv_b, ffn_w_down):
    w_all = _regroup_w_in(w_in)

    def lane_place(pair):
        z = jnp.zeros((DEPTH, LANES), F32)
        return jnp.concatenate([pair[:, 0], z[:, :DT_BWD_LANE - H_SSD], pair[:, 1],
                                z[:, :LANES - DT_BWD_LANE - H_SSD]], axis=-1)[:, None, :]

    eye = jnp.eye(LRU_BLOCKS, dtype=F32)
    gates = jnp.stack([lru_w_a, lru_w_x], axis=2)
    wgate = jnp.einsum("lxgncd,nm->lncxgmd", gates, eye).reshape(DEPTH, W_LRU, 4 * W_LRU).astype(BF16)
    bgate = jnp.stack([lru_b_a, lru_b_x], axis=2).reshape(DEPTH, 1, 4 * W_LRU)
    return dict(
        w_all=w_all,
        dec_lane=jnp.repeat(ret_decay, DK_RET, axis=-1),
        dec_wide=jnp.broadcast_to(ret_decay.reshape(DEPTH, 2 * H_RET, 1), (DEPTH, 2 * H_RET, SCAN_CHUNK)),
        ret_gn=ret_norm_g[:, None, :],
        ssd_cb=ssd_conv_b[:, None, :],
        ssd_dtb=lane_place(ssd_dt_bias), ssd_alog=lane_place(ssd_a_log),
        ssd_dskip=jnp.repeat(ssd_d, P_SSD, axis=-1)[:, None, :], ssd_gn=ssd_norm_g[:, None, :],
        lru_cb=lru_conv_b[:, None, :], lru_wg=wgate, lru_bg=bgate,
        wo=w_out.astype(BF16), wup=ffn_w_up.astype(BF16), ffn_cb=ffn_conv_b[:, None, :],
        wd=ffn_w_down.astype(BF16),
    )


def kernel(x_prompt, x_sample, c, c_ctx, state_ret, state_ssd, state_lru, w_ada, b_ada, norm1_g, norm2_g, w_in, ret_decay, ret_norm_g, ssd_conv_w, ssd_conv_b, ssd_dt_bias, ssd_a_log, ssd_d, ssd_norm_g, lru_conv_w, lru_conv_b, lru_w_a, lru_b_a, lru_w_x, lru_b_x, lru_lambda, w_out, ffn_w_up, ffn_conv_w, ffn_conv_b, ffn_w_down, final_norm_g):
    bp, tp, _ = x_prompt.shape
    bs, ts, _ = x_sample.shape
    assert tp == SCAN_CHUNK and ts == SEQ_TILE and ts % GRID_W == 0 and (bp * tp) % SEQ_TILE == 0

    n_req = 8
    cvec = jnp.concatenate([c_ctx[None, :], c, jnp.zeros((n_req - 1 - bs, D_MODEL), F32)], axis=0)
    mods = _modulation(cvec, w_ada, b_ada).reshape(DEPTH, n_req, 6, D_MODEL)

    pp = _prep_params(w_in, ret_decay, ret_norm_g, ssd_conv_b, ssd_dt_bias, ssd_a_log, ssd_d, ssd_norm_g,
                      lru_conv_b, lru_w_a, lru_b_a, lru_w_x, lru_b_x, w_out, ffn_w_up, ffn_conv_b, ffn_w_down)
    n1 = norm1_g[:, None, :]
    n2 = norm2_g[:, None, :]
    rope_tabs = _rope_tables(ts)
    fgain = final_norm_g[None, :]

    npair = H_RET // 2
    s0_ret = state_ret.reshape(bs, DEPTH, 2, npair, 2, DK_RET, DK_RET).transpose(0, 1, 2, 3, 5, 4, 6).reshape(
        bs, DEPTH, 2, npair, DK_RET, HEAD_PAIR)
    s0_ssd = state_ssd.transpose(0, 1, 2, 4, 3, 5).reshape(bs, DEPTH, 2, N_SSD, W_SSD)

    def trunk_layer(x2d, l, seq_len, row0, rows_per_mod, latent, prev=(None, None, None)):
        common = (x2d, mods, row0, rows_per_mod, n1, pp["w_all"])
        ret = _retention(*common, pp["dec_lane"], pp["dec_wide"], pp["ret_gn"], seq_len, l,
                         rope_tabs if latent else None, s0_ret if latent else None, prev[0])
        ssd = _ssd(*common, ssd_conv_w, pp["ssd_cb"], pp["ssd_dtb"], pp["ssd_alog"], pp["ssd_dskip"],
                   pp["ssd_gn"], seq_len, l, s0_ssd if latent else None, prev[1])
        lru = _lru(*common, lru_conv_w, pp["lru_cb"], pp["lru_wg"], pp["lru_bg"], lru_lambda, seq_len, l,
                   state_lru if latent else None, prev[2])
        ffn_args = (pp["wup"], ffn_conv_w, pp["ffn_cb"], pp["wd"], fgain, seq_len, l == DEPTH - 1, l)
        if ROW_TILE % seq_len == 0:
            out = _mix_ffn(x2d, ret[0], ssd[0], lru[0], mods, row0, rows_per_mod, n2, pp["wo"], *ffn_args)
        else:
            x1, h2 = _out_proj(x2d, ret[0], ssd[0], lru[0], mods, row0, rows_per_mod, n2, pp["wo"], l)
            out = _ffn(h2, x1, mods, row0, rows_per_mod, *ffn_args)
        states = None if latent else (ret[1], ssd[1], lru[1])
        return out, states

    xp = x_prompt.reshape(bp * tp, D_MODEL)
    xs = x_sample.reshape(bs * ts, D_MODEL)
    states = (None, None, None)
    for l in range(DEPTH):
        xp, states = trunk_layer(xp, l, tp, 0, bp * tp, False, states)
        xs, _ = trunk_layer(xs, l, ts, 1, ts, True)

    return (xp.reshape(bp, tp, D_MODEL), xs.reshape(bs, ts, D_MODEL), states[0],
            jnp.swapaxes(states[1], -1, -2), states[2])
```

```python
import functools
import math

import jax
import jax.numpy as jnp
from jax import lax
from jax.experimental import pallas as pl
from jax.experimental.pallas import tpu as pltpu

F32 = jnp.float32
BF16 = jnp.bfloat16

D_MODEL = 1024
DEPTH = 2
GRID_W = 64
W_RET = 384
H_RET = 6
DK_RET = 64
W_SSD = 384
P_SSD = 64
H_SSD = 6
N_SSD = 128
G_SSD = 2
CONV_CH = W_SSD + 2 * G_SSD * N_SSD
W_LRU = 256
LRU_BLOCKS = 4
LRU_BW = 64
LRU_C = 8.0
D_FF = 2816
ROPE_BASE = 10000.0
EPS = 1e-6

LANES = 128
SUBLANES = 8
HEAD_PAIR = LANES
SCAN_CHUNK = 256
FF_BLOCK = 256
N_FF_BLOCKS = D_FF // FF_BLOCK
ROW_TILE = 512
SEQ_TILE = 1024
PROJ_COL_BLOCK = 512
VMEM_LIMIT = 56 * 1024 * 1024

W_GRP_R = 4 * W_RET
W_GRP_S = W_SSD + CONV_CH + LANES
W_GRP_L = 2 * W_LRU
W_SLOT_S = W_GRP_R
COL_R, COL_S, COL_L = 0, W_GRP_R, W_GRP_R + W_SLOT_S
W_IN_PAD = COL_L + W_GRP_L
DT_BWD_LANE = 8


def _dot(a, b):
    return jnp.dot(a, b, preferred_element_type=F32)


def _dot_nt(a, b):
    return lax.dot_general(a, b, (((1,), (1,)), ((), ())), preferred_element_type=F32)


def _sigmoid(x):
    return 1.0 / (1.0 + jnp.exp(-x))


def _silu(x):
    return x * _sigmoid(x)


def _softplus(x):
    return jnp.maximum(x, 0.0) + jnp.log1p(jnp.exp(-jnp.abs(x)))


def _log_sigmoid(x):
    return -_softplus(-x)


def _gelu_tanh(x):
    c = math.sqrt(2.0 / math.pi)
    return 0.5 * x * (1.0 + jnp.tanh(c * (x + 0.044715 * (x * x * x))))


def _rms_scale(x):
    return x * lax.rsqrt(jnp.mean(x * x, axis=-1, keepdims=True) + EPS)


def _const_spec(shape):
    zeros = (0,) * len(shape)
    return pl.BlockSpec(shape, lambda *_: zeros, pipeline_mode=pl.Buffered(1))


def _layer_spec(arr, l):
    rest = (0,) * (arr.ndim - 1)
    return pl.BlockSpec((1,) + arr.shape[1:], lambda *_: (l,) + rest, pipeline_mode=pl.Buffered(1))


def _w_in_spec(l, col0, width):
    assert col0 % width == 0
    return pl.BlockSpec((1, D_MODEL, width), lambda *_: (l, 0, col0 // width), pipeline_mode=pl.Buffered(1))


def _mod_spec(row0, rows_per_mod, tm, l):
    return pl.BlockSpec((1, 1, 6, D_MODEL), lambda i: (l, row0 + i * tm // rows_per_mod, 0, 0))


def _params(n_axes):
    return pltpu.CompilerParams(dimension_semantics=("arbitrary",) * n_axes,
                                vmem_limit_bytes=VMEM_LIMIT)


def _mod_body(c_ref, w_ref, b_ref, o_ref):
    s = _silu(c_ref[...]).astype(BF16)
    o_ref[0] = _dot(s, w_ref[0].astype(BF16)) + b_ref[0]


def _modulation(cvec, w_ada, b_ada):
    rows = cvec.shape[0]
    nblk = (6 * D_MODEL) // D_MODEL
    return pl.pallas_call(
        _mod_body,
        grid=(DEPTH, nblk),
        in_specs=[pl.BlockSpec((rows, D_MODEL), lambda l, j: (0, 0)),
                  pl.BlockSpec((1, D_MODEL, D_MODEL), lambda l, j: (l, 0, j)),
                  pl.BlockSpec((1, 1, D_MODEL), lambda l, j: (l, 0, j))],
        out_specs=pl.BlockSpec((1, rows, D_MODEL), lambda l, j: (l, 0, j)),
        out_shape=jax.ShapeDtypeStruct((DEPTH, rows, 6 * D_MODEL), F32),
        compiler_params=_params(2),
        name="adaln_mod",
    )(cvec, w_ada, b_ada.reshape(DEPTH, 1, 6 * D_MODEL))


IN_DIM = 4 * W_RET + W_SSD + CONV_CH + H_SSD + 2 * W_LRU
REGROUP_ROWS = 256


def _regroup_body(w_ref, o_ref):
    rows = w_ref.shape[1]
    o_dt = COL_S + W_SSD + CONV_CH
    o_ref[0, :, 0:W_RET] = (w_ref[0, :, 0:W_RET] * (DK_RET ** -0.5)).astype(BF16)
    o_ref[0, :, W_RET:o_dt] = w_ref[0, :, W_RET:o_dt].astype(BF16)
    tail = w_ref[0, :, o_dt:IN_DIM]
    first = tail[:, 0:LANES]
    lane = lax.broadcasted_iota(jnp.int32, (rows, LANES), 1)
    dt_fwd = jnp.where(lane < H_SSD, first, 0.0)
    dt_bwd = jnp.where((lane >= DT_BWD_LANE) & (lane < DT_BWD_LANE + H_SSD), pltpu.roll(first, DT_BWD_LANE, 1), 0.0)
    o_ref[0, :, o_dt:o_dt + LANES] = (dt_fwd + dt_bwd).astype(BF16)
    o_ref[0, :, o_dt + LANES:COL_L] = jnp.zeros((rows, COL_L - o_dt - LANES), BF16)
    o_ref[0, :, COL_L:W_IN_PAD] = tail[:, H_SSD:H_SSD + W_GRP_L].astype(BF16)


def _regroup_w_in(w_in):
    rb = REGROUP_ROWS
    return pl.pallas_call(
        _regroup_body,
        grid=(DEPTH, D_MODEL // rb),
        in_specs=[pl.BlockSpec((1, rb, IN_DIM), lambda l, i: (l, i, 0))],
        out_specs=pl.BlockSpec((1, rb, W_IN_PAD), lambda l, i: (l, i, 0)),
        out_shape=jax.ShapeDtypeStruct((DEPTH, D_MODEL, W_IN_PAD), BF16),
        compiler_params=_params(2),
        name="w_in_regroup",
    )(w_in)


def _modulated_norm(x, mod_ref, g_ref, shift_row):
    y = _rms_scale(x) * g_ref[0]
    return (y * (1.0 + mod_ref[0, 0, shift_row + 1:shift_row + 2, :])
            + mod_ref[0, 0, shift_row:shift_row + 1, :]).astype(BF16)


def _project(h_ref, w_ref, p_ref, rows=slice(None)):
    h = h_ref[rows, :]
    width = p_ref.shape[1]
    for j in range(0, width, PROJ_COL_BLOCK):
        jb = min(PROJ_COL_BLOCK, width - j)
        p_ref[rows, j:j + jb] = _dot(h, w_ref[0, :, j:j + jb])


def _shift_rows(x, k, pos, seq_len):
    n = x.shape[0]
    if k == 0:
        return x
    rolled = pltpu.roll(x, (-k) % n, 0)
    ok = (pos >= -k) if k < 0 else (pos <= seq_len - 1 - k)
    return jnp.where(ok, rolled, 0.0)


def _dwconv4(x, w_ref, b_ref, col0, pos, seq_len):
    width = x.shape[1]
    cols = slice(col0, col0 + width)
    acc = x * w_ref[0, 1:2, cols] + b_ref[0, 0:1, cols]
    acc = acc + _shift_rows(x, -1, pos, seq_len) * w_ref[0, 0:1, cols]
    acc = acc + _shift_rows(x, 1, pos, seq_len) * w_ref[0, 2:3, cols]
    acc = acc + _shift_rows(x, 2, pos, seq_len) * w_ref[0, 3:4, cols]
    return acc


def _zero_later_layers(st_ref):
    st_ref[:, 1:] = jnp.zeros((st_ref.shape[0], DEPTH - 1) + tuple(st_ref.shape[2:]), F32)


def _state_output(nseq_total, nseq_tile, tail, l, prev, n_inputs):
    zeros = (0,) * len(tail)
    shape = jax.ShapeDtypeStruct((nseq_total, DEPTH) + tail, F32)
    if l == 0:
        return pl.BlockSpec((nseq_tile, DEPTH) + tail, lambda i: (i, 0) + zeros), shape, [], [], {}
    spec = pl.BlockSpec((nseq_tile, 1) + tail, lambda i: (i, l) + zeros)
    return spec, shape, [pl.BlockSpec(memory_space=pl.ANY)], [prev], {n_inputs: 1}


def _head_lane_expand(x, lane0):
    rows = x.shape[0]
    low = lax.broadcasted_iota(jnp.int32, (rows, HEAD_PAIR), 1) < DK_RET
    blocks = []
    for p in range(H_RET // 2):
        a = jnp.broadcast_to(x[:, lane0 + 2 * p:lane0 + 2 * p + 1], (rows, HEAD_PAIR))
        b = jnp.broadcast_to(x[:, lane0 + 2 * p + 1:lane0 + 2 * p + 2], (rows, HEAD_PAIR))
        blocks.append(jnp.where(low, a, b))
    return jnp.concatenate(blocks, axis=1)


def _pair_tile_rows(seq_len):
    return max(SEQ_TILE, 2 * seq_len)


def _seq_call(body, name, h2d, w_all, col0, width, l, seq_len, params, extra_in, y_width, state_tail,
              prev_states, scratch, tm=SEQ_TILE):
    n = h2d.shape[0]
    assert n % tm == 0 and tm % seq_len == 0
    nseq_tile = tm // seq_len
    emit_state = state_tail is not None
    in_specs = [pl.BlockSpec((tm, D_MODEL), lambda i: (i, 0)), _w_in_spec(l, col0, width)]
    in_specs += [_layer_spec(a, l) for a in params]
    in_specs += [spec for _, spec in extra_in]
    args = [h2d, w_all] + list(params) + [a for a, _ in extra_in]
    out_specs = [pl.BlockSpec((tm, y_width), lambda i: (i, 0))]
    out_shape = [jax.ShapeDtypeStruct((n, y_width), BF16)]
    aliases = {}
    if emit_state:
        spec, shape, extra_specs, extra_args, aliases = _state_output(
            n // seq_len, nseq_tile, state_tail, l, prev_states, len(args))
        out_specs.append(spec)
        out_shape.append(shape)
        in_specs += extra_specs
        args += extra_args
    return pl.pallas_call(
        functools.partial(body, seq_len=seq_len, emit_state=emit_state, first_layer=l == 0),
        grid=(n // tm,), in_specs=in_specs, out_specs=out_specs, out_shape=out_shape,
        input_output_aliases=aliases, scratch_shapes=scratch, compiler_params=_params(1), name=name,
    )(*args)


def _ret_body(*refs, seq_len, rope, has_s0, emit_state, first_layer):
    it = iter(refs)
    h_ref, w_ref = next(it), next(it)
    dec_lane_ref = next(it)
    dec_wide_ref = next(it)
    gn_ref = next(it)
    cos_ref = sin_ref = s0_ref = st_ref = qk_ref = None
    if rope:
        cos_ref = next(it)
        sin_ref = next(it)
    if has_s0:
        s0_ref = next(it)
    if emit_state and not first_layer:
        next(it)
    y_ref = next(it)
    if emit_state:
        st_ref = next(it)
    p_ref = next(it)
    dmat_ref = next(it)
    wts_ref = next(it)
    if rope:
        qk_ref = next(it)

    C = SCAN_CHUNK
    nch = seq_len // C
    nseq = p_ref.shape[0] // seq_len
    npair = H_RET // 2
    lgl = _log_sigmoid(dec_lane_ref[0])

    @pl.when(pl.program_id(0) == 0)
    def _():
        lgw = _log_sigmoid(dec_wide_ref[0])
        ii = lax.broadcasted_iota(jnp.int32, (C, C), 0)
        mm = lax.broadcasted_iota(jnp.int32, (C, C), 1)
        dif = (ii - mm).astype(F32)
        neg_inf = -jnp.inf
        for h in range(H_RET):
            df = jnp.exp(jnp.where(ii >= mm, dif * lgw[h:h + 1, :], neg_inf))
            db = jnp.exp(jnp.where(mm >= ii, (-dif) * lgw[H_RET + h:H_RET + h + 1, :], neg_inf))
            dmat_ref[h] = df + db
        ri = lax.broadcasted_iota(jnp.int32, (C, W_RET), 0).astype(F32)
        wts_ref[0] = jnp.exp((C - 1.0 - ri) * lgl[0:1, :])
        wts_ref[1] = jnp.exp(ri * lgl[1:2, :])
        wts_ref[2] = jnp.exp((ri + 1.0) * lgl[0:1, :])
        wts_ref[3] = jnp.exp((C - ri) * lgl[1:2, :])

    lane = lax.broadcasted_iota(jnp.int32, (C, HEAD_PAIR), 1)
    low = lane < DK_RET
    r128 = lax.broadcasted_iota(jnp.int32, (HEAD_PAIR, HEAD_PAIR), 0) >= DK_RET
    c128 = lax.broadcasted_iota(jnp.int32, (HEAD_PAIR, HEAD_PAIR), 1) >= DK_RET
    same_head = r128 == c128
    avg = jnp.where(same_head, 1.0 / DK_RET, 0.0).astype(BF16)

    need_states = emit_state or (has_s0 and nch > 1)
    if has_s0:
        dec_f = jnp.exp(float(C) * lgl[0:1, :])
        dec_b = jnp.exp(float(C) * lgl[1:2, :])

    if rope:
        swap_low = (lax.broadcasted_iota(jnp.int32, (seq_len, HEAD_PAIR), 1) % 32) < 16

    def cols(p):
        return slice(p * HEAD_PAIR, (p + 1) * HEAD_PAIR)

    if emit_state and first_layer:
        _zero_later_layers(st_ref)

    def project_seq(s):
        seq_rows = slice(s * seq_len, (s + 1) * seq_len)
        _project(h_ref, w_ref, p_ref, seq_rows)
        if rope:
            for blk in range(2 * npair):
                x = p_ref[seq_rows, cols(blk)]
                swapped = jnp.where(swap_low, pltpu.roll(x, HEAD_PAIR - 16, 1), pltpu.roll(x, 16, 1))
                qk_ref[seq_rows, cols(blk)] = x * cos_ref[...] + swapped * sin_ref[...]

    project_seq(0)
    for s in range(nseq):
        if s + 1 < nseq:
            project_seq(s + 1)

        def rows(c, s=s):
            return slice(s * seq_len + c * C, s * seq_len + (c + 1) * C)

        def get_q(c, p):
            src = qk_ref if rope else p_ref
            return src[rows(c), p * HEAD_PAIR:(p + 1) * HEAD_PAIR]

        def get_k(c, p):
            src = qk_ref if rope else p_ref
            return src[rows(c), W_RET + p * HEAD_PAIR:W_RET + (p + 1) * HEAD_PAIR]

        def get_v(c, p):
            return p_ref[rows(c), 2 * W_RET + p * HEAD_PAIR:2 * W_RET + (p + 1) * HEAD_PAIR]

        ds_f = [[None] * npair for _ in range(nch)]
        ds_b = [[None] * npair for _ in range(nch)]
        if need_states:
            for c in range(nch):
                for p in range(npair):
                    k = get_k(c, p)
                    vb = get_v(c, p).astype(BF16)
                    kf = jnp.transpose(k * wts_ref[0, :, cols(p)]).astype(BF16)
                    kb = jnp.transpose(k * wts_ref[1, :, cols(p)]).astype(BF16)
                    ds_f[c][p] = _dot(kf, vb)
                    ds_b[c][p] = _dot(kb, vb)

        if emit_state:
            for p in range(npair):
                for a in range(2):
                    blk = slice(a * DK_RET, (a + 1) * DK_RET)
                    st_ref[s, 0, 0, 2 * p + a] = ds_f[0][p][blk, blk]
                    st_ref[s, 0, 1, 2 * p + a] = ds_b[0][p][blk, blk]

        sf_in = [[None] * npair for _ in range(nch)]
        sb_in = [[None] * npair for _ in range(nch)]
        if has_s0:
            low_half = lax.broadcasted_iota(jnp.int32, (DK_RET, HEAD_PAIR), 1) < DK_RET

            def pair_state(d, p):
                u = s0_ref[s, 0, d, p]
                return jnp.concatenate([jnp.where(low_half, u, 0.0), jnp.where(low_half, 0.0, u)], axis=0)

            for p in range(npair):
                st = pair_state(0, p)
                for c in range(nch):
                    sf_in[c][p] = st
                    if c + 1 < nch:
                        st = st * dec_f[:, cols(p)] + jnp.where(same_head, ds_f[c][p], 0.0)
                st = pair_state(1, p)
                for c in reversed(range(nch)):
                    sb_in[c][p] = st
                    if c > 0:
                        st = st * dec_b[:, cols(p)] + jnp.where(same_head, ds_b[c][p], 0.0)

        for c in range(nch):
            pairs = range(npair)
            q = [get_q(c, p) for p in pairs]
            kb = [get_k(c, p).astype(BF16) for p in pairs]
            vb = [get_v(c, p).astype(BF16) for p in pairs]
            scores = []
            for p in pairs:
                scores.append(_dot_nt(jnp.where(low, q[p], 0.0).astype(BF16), kb[p]))
                scores.append(_dot_nt(jnp.where(low, 0.0, q[p]).astype(BF16), kb[p]))
            inter = [None] * npair
            if has_s0:
                for p in pairs:
                    inter[p] = (_dot((q[p] * wts_ref[2, :, cols(p)]).astype(BF16), sf_in[c][p].astype(BF16))
                                + _dot((q[p] * wts_ref[3, :, cols(p)]).astype(BF16), sb_in[c][p].astype(BF16)))
            o = []
            for p in pairs:
                o0 = _dot((scores[2 * p] * dmat_ref[2 * p]).astype(BF16), vb[p])
                o1 = _dot((scores[2 * p + 1] * dmat_ref[2 * p + 1]).astype(BF16), vb[p])
                op = jnp.where(low, o0, o1)
                o.append(op + inter[p] if has_s0 else op)
            mu = [_dot(o[p].astype(BF16), avg) for p in pairs]
            d = [o[p] - mu[p] for p in pairs]
            var = [_dot((d[p] * d[p]).astype(BF16), avg) for p in pairs]
            for p in pairs:
                on = d[p] * lax.rsqrt(var[p] + EPS)
                g = p_ref[rows(c), 3 * W_RET + p * HEAD_PAIR:3 * W_RET + (p + 1) * HEAD_PAIR]
                y_ref[rows(c), cols(p)] = (on * gn_ref[0, 0:1, cols(p)] * _silu(g)).astype(BF16)


def _retention(h2d, w_all, dec_lane, dec_wide, gn, seq_len, l, rope_tabs=None, s0=None, prev_states=None):
    rope = rope_tabs is not None
    has_s0 = s0 is not None
    npair = H_RET // 2
    tm = _pair_tile_rows(seq_len)
    extra = []
    if rope:
        extra += [(t, _const_spec(t.shape)) for t in rope_tabs]
    if has_s0:
        extra.append((s0, pl.BlockSpec((tm // seq_len, 1, 2, npair, DK_RET, HEAD_PAIR),
                                       lambda i: (i, l, 0, 0, 0, 0))))
    scratch = [pltpu.VMEM((tm, W_GRP_R), F32),
               pltpu.VMEM((H_RET, SCAN_CHUNK, SCAN_CHUNK), F32), pltpu.VMEM((4, SCAN_CHUNK, W_RET), F32)]
    if rope:
        scratch.append(pltpu.VMEM((tm, 2 * W_RET), F32))
    return _seq_call(functools.partial(_ret_body, rope=rope, has_s0=has_s0),
                     "retention_lat" if rope else "retention_ctx",
                     h2d, w_all, COL_R, W_GRP_R, l, seq_len,
                     (dec_lane, dec_wide, gn), extra, W_RET,
                     None if has_s0 else (2, H_RET, DK_RET, DK_RET), prev_states, scratch, tm)


def _ssd_body(*refs, seq_len, has_s0, emit_state, first_layer):
    it = iter(refs)
    h_ref, w_ref = next(it), next(it)
    cw_ref = next(it)
    cb_ref = next(it)
    dtb_ref = next(it)
    alog_ref = next(it)
    dskip_ref = next(it)
    gn_ref = next(it)
    s0_ref = st_ref = None
    if has_s0:
        s0_ref = next(it)
    if emit_state and not first_layer:
        next(it)
    y_ref = next(it)
    if emit_state:
        st_ref = next(it)
    p_ref = next(it)
    xs_ref = next(it)
    bc_ref = next(it)

    T = seq_len
    C = SCAN_CHUNK
    nch = T // C
    n_rows = p_ref.shape[0]
    nseq = n_rows // T
    assert T & (T - 1) == 0
    pos = lax.broadcasted_iota(jnp.int32, (T, LANES), 0)

    def project_and_conv(s):
        seq_rows = slice(s * T, (s + 1) * T)
        _project(h_ref, w_ref, p_ref, seq_rows)
        for blk in range(CONV_CH // LANES):
            x = p_ref[seq_rows, W_SSD + blk * LANES:W_SSD + (blk + 1) * LANES]
            a = _silu(_dwconv4(x, cw_ref, cb_ref, blk * LANES, pos, T))
            if blk < W_SSD // LANES:
                xs_ref[seq_rows, blk * LANES:(blk + 1) * LANES] = a
            else:
                o = blk * LANES - W_SSD
                bc_ref[seq_rows, o:o + LANES] = a

    a_neg = -jnp.exp(alog_ref[0])
    ii = lax.broadcasted_iota(jnp.int32, (C, C), 0)
    mm = lax.broadcasted_iota(jnp.int32, (C, C), 1)
    lower = ii >= mm
    upper = mm >= ii
    tri_l = jnp.where(lower, 1.0, 0.0).astype(BF16)
    tri_u = jnp.where(upper, 1.0, 0.0).astype(BF16)
    neg_inf = -jnp.inf
    grp0 = lax.broadcasted_iota(jnp.int32, (C, W_SSD), 1) < (W_SSD // G_SSD)
    grp0_s = lax.broadcasted_iota(jnp.int32, (N_SSD, W_SSD), 1) < (W_SSD // G_SSD)
    grp0_t = lax.broadcasted_iota(jnp.int32, (W_SSD, N_SSD), 0) < (W_SSD // G_SSD)
    low = lax.broadcasted_iota(jnp.int32, (C, HEAD_PAIR), 1) < P_SSD

    if emit_state and first_layer:
        _zero_later_layers(st_ref)

    scalars = {}

    def chunk_scalars(s, c):
        if (s, c) not in scalars:
            rows = slice(s * T + c * C, s * T + (c + 1) * C)
            dt = _softplus(p_ref[rows, W_SSD + CONV_CH:W_SSD + CONV_CH + LANES] + dtb_ref[0])
            la = dt * a_neg
            a1 = la.astype(BF16)
            r1 = la - a1.astype(F32)
            a2 = r1.astype(BF16)
            a3 = (r1 - a2.astype(F32)).astype(BF16)
            pre = _dot(tri_l, a1) + _dot(tri_l, a2) + _dot(tri_l, a3)
            suf = _dot(tri_u, a1) + _dot(tri_u, a2) + _dot(tri_u, a3)
            scalars[(s, c)] = (dt, pre, suf)
        return scalars[(s, c)]

    def light_stage(s):
        def rows(c):
            return slice(s * T + c * C, s * T + (c + 1) * C)

        def state_weights(c):
            dt, pre, suf = chunk_scalars(s, c)
            wf = jnp.exp(pre[C - 1:C, :] - pre) * dt
            wb = jnp.exp(suf[0:1, :] - suf) * dt
            xs = xs_ref[rows(c), :]
            return (xs * _head_lane_expand(wf, 0), xs * _head_lane_expand(wb, DT_BWD_LANE), pre, suf)

        if emit_state:
            xf, xb, _, _ = state_weights(0)
            bm = [bc_ref[rows(0), g * N_SSD:(g + 1) * N_SSD].astype(BF16) for g in range(G_SSD)]
            for d, xw in enumerate((xf, xb)):
                xt = jnp.transpose(xw).astype(BF16)
                st = jnp.where(grp0_t, _dot(xt, bm[0]), _dot(xt, bm[1]))
                for h in range(H_SSD):
                    st_ref[s, 0, d, h] = st[h * P_SSD:(h + 1) * P_SSD, :]

        ds_f = [None] * nch
        ds_b = [None] * nch
        dec_f = [None] * nch
        dec_b = [None] * nch
        if has_s0 and nch > 1:
            for c in range(nch):
                xf, xb, pre, suf = state_weights(c)
                bt0 = jnp.transpose(bc_ref[rows(c), 0:N_SSD]).astype(BF16)
                bt1 = jnp.transpose(bc_ref[rows(c), N_SSD:2 * N_SSD]).astype(BF16)
                ds_f[c] = jnp.where(grp0_s, _dot(bt0, xf.astype(BF16)), _dot(bt1, xf.astype(BF16)))
                ds_b[c] = jnp.where(grp0_s, _dot(bt0, xb.astype(BF16)), _dot(bt1, xb.astype(BF16)))
                dec_f[c] = _head_lane_expand(jnp.exp(pre[C - 1:C, :]), 0)
                dec_b[c] = _head_lane_expand(jnp.exp(suf[0:1, :]), DT_BWD_LANE)

        sf_in = [None] * nch
        sb_in = [None] * nch
        if has_s0:
            st = s0_ref[s, 0, 0]
            for c in range(nch):
                sf_in[c] = st
                if c + 1 < nch:
                    st = st * dec_f[c] + ds_f[c]
            st = s0_ref[s, 0, 1]
            for c in reversed(range(nch)):
                sb_in[c] = st
                if c > 0:
                    st = st * dec_b[c] + ds_b[c]

        staged = []
        for c in range(nch):
            dt, pre, suf = chunk_scalars(s, c)
            cm = [bc_ref[rows(c), 2 * N_SSD + g * N_SSD:2 * N_SSD + (g + 1) * N_SSD].astype(BF16)
                  for g in range(G_SSD)]
            bm = [bc_ref[rows(c), g * N_SSD:(g + 1) * N_SSD].astype(BF16) for g in range(G_SSD)]
            gram = [_dot_nt(cm[g], bm[g]) for g in range(G_SSD)]
            inter = None
            if has_s0:
                sf = sf_in[c].astype(BF16)
                sb = sb_in[c].astype(BF16)
                inter = (jnp.where(grp0, _dot(cm[0], sf), _dot(cm[1], sf)),
                         jnp.where(grp0, _dot(cm[0], sb), _dot(cm[1], sb)))
            staged.append((dt, pre, suf, jnp.transpose(pre), jnp.transpose(suf), jnp.transpose(dt), gram, inter))
        return staged

    def heavy_stage(s, staged):
        for c in range(nch):
            rows_c = slice(s * T + c * C, s * T + (c + 1) * C)
            dt, pre, suf, pre_t, suf_t, dt_t, gram, inter = staged[c]
            xs = xs_ref[rows_c, :]
            xs_b = xs.astype(BF16)
            heads = []
            for h in range(H_SSD):
                g = h // (H_SSD // G_SSD)
                hb = DT_BWD_LANE + h
                df = jnp.exp(jnp.where(lower, pre[:, h:h + 1] - pre_t[h:h + 1, :], neg_inf))
                db = jnp.exp(jnp.where(upper, suf[:, hb:hb + 1] - suf_t[hb:hb + 1, :], neg_inf))
                w = gram[g] * (df * dt_t[h:h + 1, :] + db * dt_t[hb:hb + 1, :])
                p = h // 2
                heads.append(_dot(w.astype(BF16), xs_b[:, p * HEAD_PAIR:(p + 1) * HEAD_PAIR]))
            o = jnp.concatenate([jnp.where(low, heads[2 * p], heads[2 * p + 1]) for p in range(H_SSD // 2)],
                                axis=1)
            if has_s0:
                o = o + _head_lane_expand(jnp.exp(pre), 0) * inter[0]
                o = o + _head_lane_expand(jnp.exp(suf), DT_BWD_LANE) * inter[1]
            y = o + dskip_ref[0] * xs
            yz = y * _silu(p_ref[rows_c, 0:W_SSD])
            y_ref[rows_c, :] = (_rms_scale(yz) * gn_ref[0]).astype(BF16)

    project_and_conv(0)
    staged = light_stage(0)
    for s in range(nseq):
        staged_next = None
        if s + 1 < nseq:
            project_and_conv(s + 1)
            staged_next = light_stage(s + 1)
        heavy_stage(s, staged)
        staged = staged_next


def _ssd(h2d, w_all, cw, cb, dtb, alog, dskip, gn, seq_len, l,
         s0=None, prev_states=None):
    has_s0 = s0 is not None
    tm = _pair_tile_rows(seq_len)
    extra = []
    if has_s0:
        extra.append((s0, pl.BlockSpec((tm // seq_len, 1, 2, N_SSD, W_SSD), lambda i: (i, l, 0, 0, 0))))
    scratch = [pltpu.VMEM((tm, W_GRP_S), F32), pltpu.VMEM((tm, W_SSD), F32),
               pltpu.VMEM((tm, 2 * G_SSD * N_SSD), F32)]
    return _seq_call(functools.partial(_ssd_body, has_s0=has_s0), "ssd_lat" if has_s0 else "ssd_ctx",
                     h2d, w_all, COL_S, W_SLOT_S, l, seq_len,
                     (cw, cb, dtb, alog, dskip, gn), extra, W_SSD,
                     None if has_s0 else (2, H_SSD, P_SSD, N_SSD), prev_states, scratch, tm)


def _lru_body(*refs, seq_len, has_h0, emit_state, first_layer):
    it = iter(refs)
    h_ref, w_ref = next(it), next(it)
    cw_ref = next(it)
    cb_ref = next(it)
    wg_ref = next(it)
    bg_ref = next(it)
    lam_ref = next(it)
    h0_ref = st_ref = None
    if has_h0:
        h0_ref = next(it)
    if emit_state and not first_layer:
        next(it)
    y_ref = next(it)
    if emit_state:
        st_ref = next(it)
    p_ref = next(it)

    T = seq_len
    n_rows = p_ref.shape[0]
    nseq = n_rows // T
    assert not has_h0 or nseq == 1
    assert T & (T - 1) == 0
    _project(h_ref, w_ref, p_ref)
    row = lax.broadcasted_iota(jnp.int32, (n_rows, W_LRU), 0)
    pos = row & (T - 1)
    sub = row & (SUBLANES - 1)
    xc = _dwconv4(p_ref[:, 0:W_LRU], cw_ref, cb_ref, 0, pos, T)
    gates = _sigmoid(_dot(xc.astype(BF16), wg_ref[0]) + bg_ref[0])
    decay_rate = _softplus(-lam_ref[0])
    if emit_state and first_layer:
        _zero_later_layers(st_ref)
    nblk = T // SUBLANES
    total = None
    for d in range(2):
        r = gates[:, 2 * d * W_LRU:(2 * d + 1) * W_LRU]
        i = gates[:, (2 * d + 1) * W_LRU:(2 * d + 2) * W_LRU]
        log_a = (-LRU_C * r) * decay_rate[d:d + 1, :]
        a = jnp.exp(log_a)
        th = jnp.tanh(log_a)
        b = jnp.sqrt(-2.0 * th / (1.0 - th)) * (i * xc)
        step = 1
        while step < SUBLANES:
            if d == 0:
                ok = sub >= step
                shift = step
            else:
                ok = sub <= SUBLANES - 1 - step
                shift = n_rows - step
            a_prev = jnp.where(ok, pltpu.roll(a, shift, 0), 1.0)
            b_prev = jnp.where(ok, pltpu.roll(b, shift, 0), 0.0)
            b = a * b_prev + b
            a = a * a_prev
            step *= 2
        edge = SUBLANES - 1 if d == 0 else 0
        tiles = [None] * (nseq * nblk)
        for s in range(nseq):
            carry = h0_ref[0, 0, d:d + 1, :] if has_h0 else None
            for v in (range(nblk) if d == 0 else reversed(range(nblk))):
                t = s * nblk + v
                rows = slice(t * SUBLANES, (t + 1) * SUBLANES)
                hv = b[rows]
                if carry is not None:
                    hv = hv + a[rows] * carry
                tiles[t] = hv
                carry = hv[edge:edge + 1, :]
            if emit_state:
                st_ref[s, 0, d:d + 1, :] = carry
        hs = jnp.concatenate(tiles, axis=0)
        total = hs if total is None else total + hs
    y_ref[...] = (total * _gelu_tanh(p_ref[:, W_LRU:2 * W_LRU])).astype(BF16)


def _lru(h2d, w_all, cw, cb, wg, bg, lam, seq_len, l, h0=None, prev_states=None):
    has_h0 = h0 is not None
    extra = []
    if has_h0:
        extra.append((h0, pl.BlockSpec((1, 1, 2, W_LRU), lambda i: (i, l, 0, 0))))
    scratch = [pltpu.VMEM((SEQ_TILE, W_GRP_L), F32)]
    return _seq_call(functools.partial(_lru_body, has_h0=has_h0), "lru_lat" if has_h0 else "lru_ctx",
                     h2d, w_all, COL_L, W_GRP_L, l, seq_len,
                     (cw, cb, wg, bg, lam), extra, W_LRU,
                     None if has_h0 else (2, W_LRU), prev_states, scratch)


def _outproj_body(x_ref, yr_ref, ys_ref, yl_ref, mod_ref, g_ref, wo_ref, x1_ref, h2_ref):
    mix = _dot(yr_ref[...], wo_ref[0, 0:W_RET, :])
    mix = mix + _dot(ys_ref[...], wo_ref[0, W_RET:W_RET + W_SSD, :])
    mix = mix + _dot(yl_ref[...], wo_ref[0, W_RET + W_SSD:D_MODEL, :])
    x1 = x_ref[...] + mod_ref[0, 0, 2:3, :] * mix
    x1_ref[...] = x1
    h2_ref[...] = _modulated_norm(x1, mod_ref, g_ref, 3)


def _out_proj(x2d, y_ret, y_ssd, y_lru, mods, row0, rows_per_mod, gain, wo, l):
    n = x2d.shape[0]
    tm = ROW_TILE
    row_spec = lambda w: pl.BlockSpec((tm, w), lambda i: (i, 0))
    return pl.pallas_call(
        _outproj_body,
        grid=(n // tm,),
        in_specs=[row_spec(D_MODEL), row_spec(W_RET), row_spec(W_SSD), row_spec(W_LRU),
                  _mod_spec(row0, rows_per_mod, tm, l), _layer_spec(gain, l), _layer_spec(wo, l)],
        out_specs=[row_spec(D_MODEL), row_spec(D_MODEL)],
        out_shape=[jax.ShapeDtypeStruct((n, D_MODEL), F32), jax.ShapeDtypeStruct((n, D_MODEL), BF16)],
        compiler_params=_params(1),
        name="out_proj",
    )(x2d, y_ret, y_ssd, y_lru, mods, gain, wo)


def _ffn_core(h, wup_ref, cw_ref, cb_ref, wd_ref, act_ref, seq_len):
    tm = h.shape[0]
    nseq = tm // seq_len
    row8 = lax.broadcasted_iota(jnp.int32, (SUBLANES, FF_BLOCK), 0)
    first_row = row8 == 0
    last_row = row8 == SUBLANES - 1

    def conv3(u, col0):
        cols = slice(col0, col0 + FF_BLOCK)
        w_prev, w_mid, w_next = (cw_ref[0, t:t + 1, cols] for t in range(3))
        bias = cb_ref[0, 0:1, cols]
        outs = []
        for s in range(nseq):
            us = u[s * seq_len:(s + 1) * seq_len]
            prev = pltpu.roll(us, 1, 0)
            nxt = pltpu.roll(us, seq_len - 1, 0)
            prev = jnp.concatenate([jnp.where(first_row, 0.0, prev[:SUBLANES]), prev[SUBLANES:]], axis=0)
            nxt = jnp.concatenate([nxt[:seq_len - SUBLANES], jnp.where(last_row, 0.0, nxt[seq_len - SUBLANES:])],
                                  axis=0)
            outs.append(us * w_mid + bias + prev * w_prev + nxt * w_next)
        return outs[0] if nseq == 1 else jnp.concatenate(outs, axis=0)

    def up(j):
        return (_dot(h, wup_ref[0, :, j * FF_BLOCK:(j + 1) * FF_BLOCK]),
                _dot(h, wup_ref[0, :, D_FF + j * FF_BLOCK:D_FF + (j + 1) * FF_BLOCK]))

    for j in range(N_FF_BLOCKS):
        uv, ug = up(j)
        val = conv3(uv, j * FF_BLOCK)
        gate = conv3(ug, D_FF + j * FF_BLOCK)
        act_ref[:, j * FF_BLOCK:(j + 1) * FF_BLOCK] = (_silu(gate) * val).astype(BF16)
    return _dot(act_ref[...], wd_ref[0])


def _finish_layer(out, tail_refs, o_ref, final_norm):
    if final_norm:
        (fg_ref,) = tail_refs
        o_ref[...] = _rms_scale(out) * fg_ref[...]
    else:
        mod_next_ref, g_next_ref, h_next_ref = tail_refs
        o_ref[...] = out
        h_next_ref[...] = _modulated_norm(out, mod_next_ref, g_next_ref, 0)


def _ffn_body(*refs, seq_len, final_norm):
    h_ref, x1_ref, mod_ref, wup_ref, cw_ref, cb_ref, wd_ref = refs[:7]
    n_tail_in = 1 if final_norm else 2
    tail_in = refs[7:7 + n_tail_in]
    o_ref = refs[7 + n_tail_in]
    tail_out = refs[8 + n_tail_in:-1]
    act_ref = refs[-1]
    down = _ffn_core(h_ref[...], wup_ref, cw_ref, cb_ref, wd_ref, act_ref, seq_len)
    out = x1_ref[...] + mod_ref[0, 0, 5:6, :] * down
    _finish_layer(out, tuple(tail_in) + tuple(tail_out), o_ref, final_norm)


def _mix_ffn_body(*refs, seq_len, final_norm):
    x_ref, yr_ref, ys_ref, yl_ref, mod_ref, g_ref, wo_ref, wup_ref, cw_ref, cb_ref, wd_ref = refs[:11]
    n_tail_in = 1 if final_norm else 2
    tail_in = refs[11:11 + n_tail_in]
    o_ref = refs[11 + n_tail_in]
    tail_out = refs[12 + n_tail_in:-2]
    act_ref, h_ref = refs[-2:]
    mix = _dot(yr_ref[...], wo_ref[0, 0:W_RET, :])
    mix = mix + _dot(ys_ref[...], wo_ref[0, W_RET:W_RET + W_SSD, :])
    mix = mix + _dot(yl_ref[...], wo_ref[0, W_RET + W_SSD:D_MODEL, :])
    x1 = x_ref[...] + mod_ref[0, 0, 2:3, :] * mix
    o_ref[...] = x1
    h_ref[...] = _modulated_norm(x1, mod_ref, g_ref, 3)
    down = _ffn_core(h_ref[...], wup_ref, cw_ref, cb_ref, wd_ref, act_ref, seq_len)
    out = o_ref[...] + mod_ref[0, 0, 5:6, :] * down
    _finish_layer(out, tuple(tail_in) + tuple(tail_out), o_ref, final_norm)


def _layer_tail(n, tm, mods, row0, rows_per_mod, gain1, fgain, final_norm, l):
    row_spec = pl.BlockSpec((tm, D_MODEL), lambda i: (i, 0))
    out_specs = [row_spec]
    out_shape = [jax.ShapeDtypeStruct((n, D_MODEL), F32)]
    if final_norm:
        return [_const_spec(fgain.shape)], [fgain], out_specs, out_shape
    out_specs.append(row_spec)
    out_shape.append(jax.ShapeDtypeStruct((n, D_MODEL), BF16))
    return ([_mod_spec(row0, rows_per_mod, tm, l + 1), _layer_spec(gain1, l + 1)], [mods, gain1],
            out_specs, out_shape)


def _mix_ffn(x2d, y_ret, y_ssd, y_lru, mods, row0, rows_per_mod, gain, wo, wup, cw, cb, wd, gain1, fgain, seq_len,
             final_norm, l):
    n = x2d.shape[0]
    tm = ROW_TILE
    assert tm % seq_len == 0
    row_spec = lambda w: pl.BlockSpec((tm, w), lambda i: (i, 0))
    tail_specs, tail_args, out_specs, out_shape = _layer_tail(n, tm, mods, row0, rows_per_mod, gain1, fgain,
                                                               final_norm, l)
    return pl.pallas_call(
        functools.partial(_mix_ffn_body, seq_len=seq_len, final_norm=final_norm),
        grid=(n // tm,),
        in_specs=[row_spec(D_MODEL), row_spec(W_RET), row_spec(W_SSD), row_spec(W_LRU),
                  _mod_spec(row0, rows_per_mod, tm, l), _layer_spec(gain, l), _layer_spec(wo, l),
                  _layer_spec(wup, l), _layer_spec(cw, l), _layer_spec(cb, l), _layer_spec(wd, l)] + tail_specs,
        out_specs=out_specs, out_shape=out_shape,
        scratch_shapes=[pltpu.VMEM((tm, D_FF), BF16), pltpu.VMEM((tm, D_MODEL), BF16)],
        compiler_params=_params(1),
        name="mix_ffn",
    )(x2d, y_ret, y_ssd, y_lru, mods, gain, wo, wup, cw, cb, wd, *tail_args)


def _ffn(h2, x1, mods, row0, rows_per_mod, wup, cw, cb, wd, gain1, fgain, seq_len, final_norm, l):
    n = h2.shape[0]
    tm = SEQ_TILE
    row_spec = pl.BlockSpec((tm, D_MODEL), lambda i: (i, 0))
    tail_specs, tail_args, out_specs, out_shape = _layer_tail(n, tm, mods, row0, rows_per_mod, gain1, fgain,
                                                               final_norm, l)
    return pl.pallas_call(
        functools.partial(_ffn_body, seq_len=seq_len, final_norm=final_norm),
        grid=(n // tm,),
        in_specs=[row_spec, row_spec, _mod_spec(row0, rows_per_mod, tm, l),
                  _layer_spec(wup, l), _layer_spec(cw, l), _layer_spec(cb, l), _layer_spec(wd, l)] + tail_specs,
        out_specs=out_specs, out_shape=out_shape,
        scratch_shapes=[pltpu.VMEM((tm, D_FF), BF16)],
        compiler_params=_params(1),
        name="ffn",
    )(h2, x1, mods, wup, cw, cb, wd, *tail_args)


def _norm_body(x_ref, mod_ref, g_ref, h_ref):
    h_ref[...] = _modulated_norm(x_ref[...], mod_ref, g_ref, 0)


def _input_norm(x2d, mods, row0, rows_per_mod, gain1, l):
    n = x2d.shape[0]
    tm = SEQ_TILE
    row_spec = pl.BlockSpec((tm, D_MODEL), lambda i: (i, 0))
    return pl.pallas_call(
        _norm_body,
        grid=(n // tm,),
        in_specs=[row_spec, _mod_spec(row0, rows_per_mod, tm, l), _layer_spec(gain1, l)],
        out_specs=row_spec,
        out_shape=jax.ShapeDtypeStruct((n, D_MODEL), BF16),
        compiler_params=_params(1),
        name="input_norm",
    )(x2d, mods, gain1)


def _rope_tables(t_len):
    tok = jnp.arange(t_len)
    row_pos = (tok // GRID_W).astype(F32)
    col_pos = (tok % GRID_W).astype(F32)
    lane = jnp.arange(HEAD_PAIR)
    d = lane % DK_RET
    use_row = d < (DK_RET // 2)
    half = DK_RET // 4
    freqs = ROPE_BASE ** (-(d % half).astype(F32) / half)
    ang = jnp.where(use_row[None, :], row_pos[:, None], col_pos[:, None]) * freqs[None, :]
    first = (d % (2 * half)) < half
    return jnp.cos(ang), jnp.where(first[None, :], -jnp.sin(ang), jnp.sin(ang))


def _prep_params(w_in, ret_decay, ret_norm_g, ssd_conv_b, ssd_dt_bias, ssd_a_log, ssd_d, ssd_norm_g,
                 lru_conv_b, lru_w_a, lru_b_a, lru_w_x, lru_b_x, w_out, ffn_w_up, ffn_conv_b, ffn_w_down):
    w_all = _regroup_w_in(w_in)

    def lane_place(pair):
        z = jnp.zeros((DEPTH, LANES), F32)
        return jnp.concatenate([pair[:, 0], z[:, :DT_BWD_LANE - H_SSD], pair[:, 1],
                                z[:, :LANES - DT_BWD_LANE - H_SSD]], axis=-1)[:, None, :]

    eye = jnp.eye(LRU_BLOCKS, dtype=F32)
    gates = jnp.stack([lru_w_a, lru_w_x], axis=2)
    wgate = jnp.einsum("lxgncd,nm->lncxgmd", gates, eye).reshape(DEPTH, W_LRU, 4 * W_LRU).astype(BF16)
    bgate = jnp.stack([lru_b_a, lru_b_x], axis=2).reshape(DEPTH, 1, 4 * W_LRU)
    return dict(
        w_all=w_all,
        dec_lane=jnp.repeat(ret_decay, DK_RET, axis=-1),
        dec_wide=jnp.broadcast_to(ret_decay.reshape(DEPTH, 2 * H_RET, 1), (DEPTH, 2 * H_RET, SCAN_CHUNK)),
        ret_gn=ret_norm_g[:, None, :],
        ssd_cb=ssd_conv_b[:, None, :],
        ssd_dtb=lane_place(ssd_dt_bias), ssd_alog=lane_place(ssd_a_log),
        ssd_dskip=jnp.repeat(ssd_d, P_SSD, axis=-1)[:, None, :], ssd_gn=ssd_norm_g[:, None, :],
        lru_cb=lru_conv_b[:, None, :], lru_wg=wgate, lru_bg=bgate,
        wo=w_out.astype(BF16), wup=ffn_w_up.astype(BF16), ffn_cb=ffn_conv_b[:, None, :],
        wd=ffn_w_down.astype(BF16),
    )


def kernel(x_prompt, x_sample, c, c_ctx, state_ret, state_ssd, state_lru, w_ada, b_ada, norm1_g, norm2_g, w_in, ret_decay, ret_norm_g, ssd_conv_w, ssd_conv_b, ssd_dt_bias, ssd_a_log, ssd_d, ssd_norm_g, lru_conv_w, lru_conv_b, lru_w_a, lru_b_a, lru_w_x, lru_b_x, lru_lambda, w_out, ffn_w_up, ffn_conv_w, ffn_conv_b, ffn_w_down, final_norm_g):
    bp, tp, _ = x_prompt.shape
    bs, ts, _ = x_sample.shape
    assert tp == SCAN_CHUNK and ts == SEQ_TILE and ts % GRID_W == 0 and (bp * tp) % SEQ_TILE == 0

    n_req = 8
    cvec = jnp.concatenate([c_ctx[None, :], c, jnp.zeros((n_req - 1 - bs, D_MODEL), F32)], axis=0)
    mods = _modulation(cvec, w_ada, b_ada).reshape(DEPTH, n_req, 6, D_MODEL)

    pp = _prep_params(w_in, ret_decay, ret_norm_g, ssd_conv_b, ssd_dt_bias, ssd_a_log, ssd_d, ssd_norm_g,
                      lru_conv_b, lru_w_a, lru_b_a, lru_w_x, lru_b_x, w_out, ffn_w_up, ffn_conv_b, ffn_w_down)
    n1 = norm1_g[:, None, :]
    n2 = norm2_g[:, None, :]
    rope_tabs = _rope_tables(ts)
    fgain = final_norm_g[None, :]

    npair = H_RET // 2
    s0_ret = state_ret.reshape(bs, DEPTH, 2, npair, 2, DK_RET, DK_RET).transpose(0, 1, 2, 3, 5, 4, 6).reshape(
        bs, DEPTH, 2, npair, DK_RET, HEAD_PAIR)
    s0_ssd = state_ssd.transpose(0, 1, 2, 4, 3, 5).reshape(bs, DEPTH, 2, N_SSD, W_SSD)

    def trunk_layer(x2d, h2d, l, seq_len, row0, rows_per_mod, latent, prev=(None, None, None)):
        common = (h2d, pp["w_all"])
        ret = _retention(*common, pp["dec_lane"], pp["dec_wide"], pp["ret_gn"], seq_len, l,
                         rope_tabs if latent else None, s0_ret if latent else None, prev[0])
        ssd = _ssd(*common, ssd_conv_w, pp["ssd_cb"], pp["ssd_dtb"], pp["ssd_alog"], pp["ssd_dskip"],
                   pp["ssd_gn"], seq_len, l, s0_ssd if latent else None, prev[1])
        lru = _lru(*common, lru_conv_w, pp["lru_cb"], pp["lru_wg"], pp["lru_bg"], lru_lambda, seq_len, l,
                   state_lru if latent else None, prev[2])
        last = l == DEPTH - 1
        ffn_args = (pp["wup"], ffn_conv_w, pp["ffn_cb"], pp["wd"], n1, fgain, seq_len, last, l)
        if ROW_TILE % seq_len == 0:
            out = _mix_ffn(x2d, ret[0], ssd[0], lru[0], mods, row0, rows_per_mod, n2, pp["wo"], *ffn_args)
        else:
            x1, h2 = _out_proj(x2d, ret[0], ssd[0], lru[0], mods, row0, rows_per_mod, n2, pp["wo"], l)
            out = _ffn(h2, x1, mods, row0, rows_per_mod, *ffn_args)
        x_next, h_next = (out[0], None) if last else out
        states = None if latent else (ret[1], ssd[1], lru[1])
        return x_next, h_next, states

    xp = x_prompt.reshape(bp * tp, D_MODEL)
    xs = x_sample.reshape(bs * ts, D_MODEL)
    hp = _input_norm(xp, mods, 0, bp * tp, n1, 0)
    hs = _input_norm(xs, mods, 1, ts, n1, 0)
    states = (None, None, None)
    for l in range(DEPTH):
        xp, hp, states = trunk_layer(xp, hp, l, tp, 0, bp * tp, False, states)
        xs, hs, _ = trunk_layer(xs, hs, l, ts, 1, ts, True)

    return (xp.reshape(bp, tp, D_MODEL), xs.reshape(bs, ts, D_MODEL), states[0],
            jnp.swapaxes(states[1], -1, -2), states[2])
```

```python
import functools
import math

import jax
import jax.numpy as jnp
from jax import lax
from jax.experimental import pallas as pl
from jax.experimental.pallas import tpu as pltpu

F32 = jnp.float32
BF16 = jnp.bfloat16

D_MODEL = 1024
DEPTH = 2
GRID_W = 64
W_RET = 384
H_RET = 6
DK_RET = 64
W_SSD = 384
P_SSD = 64
H_SSD = 6
N_SSD = 128
G_SSD = 2
CONV_CH = W_SSD + 2 * G_SSD * N_SSD
W_LRU = 256
LRU_BLOCKS = 4
LRU_BW = 64
LRU_C = 8.0
D_FF = 2816
ROPE_BASE = 10000.0
EPS = 1e-6

LANES = 128
SUBLANES = 8
HEAD_PAIR = LANES
SCAN_CHUNK = 256
FF_BLOCK = 256
N_FF_BLOCKS = D_FF // FF_BLOCK
ROW_TILE = 512
SEQ_TILE = 1024
PROJ_COL_BLOCK = 512
VMEM_LIMIT = 56 * 1024 * 1024

W_GRP_R = 4 * W_RET
W_GRP_S = W_SSD + CONV_CH + LANES
W_GRP_L = 2 * W_LRU
W_SLOT_S = W_GRP_R
COL_R, COL_S, COL_L = 0, W_GRP_R, W_GRP_R + W_SLOT_S
W_IN_PAD = COL_L + W_GRP_L
DT_BWD_LANE = 8


def _dot(a, b):
    return jnp.dot(a, b, preferred_element_type=F32)


def _dot_nt(a, b):
    return lax.dot_general(a, b, (((1,), (1,)), ((), ())), preferred_element_type=F32)


def _sigmoid(x):
    return 1.0 / (1.0 + jnp.exp(-x))


def _silu(x):
    return x * _sigmoid(x)


def _softplus(x):
    return jnp.maximum(x, 0.0) + jnp.log1p(jnp.exp(-jnp.abs(x)))


def _log_sigmoid(x):
    return -_softplus(-x)


def _gelu_tanh(x):
    c = math.sqrt(2.0 / math.pi)
    return 0.5 * x * (1.0 + jnp.tanh(c * (x + 0.044715 * (x * x * x))))


def _rms_scale(x):
    return x * lax.rsqrt(jnp.mean(x * x, axis=-1, keepdims=True) + EPS)


def _const_spec(shape):
    zeros = (0,) * len(shape)
    return pl.BlockSpec(shape, lambda *_: zeros, pipeline_mode=pl.Buffered(1))


def _layer_spec(arr, l):
    rest = (0,) * (arr.ndim - 1)
    return pl.BlockSpec((1,) + arr.shape[1:], lambda *_: (l,) + rest, pipeline_mode=pl.Buffered(1))


def _w_in_spec(l, col0, width):
    assert col0 % width == 0
    return pl.BlockSpec((1, D_MODEL, width), lambda *_: (l, 0, col0 // width), pipeline_mode=pl.Buffered(1))


def _mod_spec(row0, rows_per_mod, tm, l):
    return pl.BlockSpec((1, 1, 6, D_MODEL), lambda i: (l, row0 + i * tm // rows_per_mod, 0, 0))


def _params(n_axes):
    return pltpu.CompilerParams(dimension_semantics=("arbitrary",) * n_axes,
                                vmem_limit_bytes=VMEM_LIMIT)


def _mod_body(c_ref, w_ref, b_ref, o_ref):
    s = _silu(c_ref[...]).astype(BF16)
    o_ref[0] = _dot(s, w_ref[0].astype(BF16)) + b_ref[0]


def _modulation(cvec, w_ada, b_ada):
    rows = cvec.shape[0]
    nblk = (6 * D_MODEL) // D_MODEL
    return pl.pallas_call(
        _mod_body,
        grid=(DEPTH, nblk),
        in_specs=[pl.BlockSpec((rows, D_MODEL), lambda l, j: (0, 0)),
                  pl.BlockSpec((1, D_MODEL, D_MODEL), lambda l, j: (l, 0, j)),
                  pl.BlockSpec((1, 1, D_MODEL), lambda l, j: (l, 0, j))],
        out_specs=pl.BlockSpec((1, rows, D_MODEL), lambda l, j: (l, 0, j)),
        out_shape=jax.ShapeDtypeStruct((DEPTH, rows, 6 * D_MODEL), F32),
        compiler_params=_params(2),
        name="adaln_mod",
    )(cvec, w_ada, b_ada.reshape(DEPTH, 1, 6 * D_MODEL))


IN_DIM = 4 * W_RET + W_SSD + CONV_CH + H_SSD + 2 * W_LRU
REGROUP_ROWS = 256


def _regroup_body(w_ref, o_ref):
    rows = w_ref.shape[1]
    o_dt = COL_S + W_SSD + CONV_CH
    o_ref[0, :, 0:W_RET] = (w_ref[0, :, 0:W_RET] * (DK_RET ** -0.5)).astype(BF16)
    o_ref[0, :, W_RET:o_dt] = w_ref[0, :, W_RET:o_dt].astype(BF16)
    tail = w_ref[0, :, o_dt:IN_DIM]
    first = tail[:, 0:LANES]
    lane = lax.broadcasted_iota(jnp.int32, (rows, LANES), 1)
    dt_fwd = jnp.where(lane < H_SSD, first, 0.0)
    dt_bwd = jnp.where((lane >= DT_BWD_LANE) & (lane < DT_BWD_LANE + H_SSD), pltpu.roll(first, DT_BWD_LANE, 1), 0.0)
    o_ref[0, :, o_dt:o_dt + LANES] = (dt_fwd + dt_bwd).astype(BF16)
    o_ref[0, :, o_dt + LANES:COL_L] = jnp.zeros((rows, COL_L - o_dt - LANES), BF16)
    o_ref[0, :, COL_L:W_IN_PAD] = tail[:, H_SSD:H_SSD + W_GRP_L].astype(BF16)


def _regroup_w_in(w_in):
    rb = REGROUP_ROWS
    return pl.pallas_call(
        _regroup_body,
        grid=(DEPTH, D_MODEL // rb),
        in_specs=[pl.BlockSpec((1, rb, IN_DIM), lambda l, i: (l, i, 0))],
        out_specs=pl.BlockSpec((1, rb, W_IN_PAD), lambda l, i: (l, i, 0)),
        out_shape=jax.ShapeDtypeStruct((DEPTH, D_MODEL, W_IN_PAD), BF16),
        compiler_params=_params(2),
        name="w_in_regroup",
    )(w_in)


def _project(x_ref, mod_ref, g_ref, w_ref, p_ref, rows=slice(None)):
    y = _rms_scale(x_ref[rows, :]) * g_ref[0]
    h = (y * (1.0 + mod_ref[0, 0, 1:2, :]) + mod_ref[0, 0, 0:1, :]).astype(BF16)
    width = p_ref.shape[1]
    for j in range(0, width, PROJ_COL_BLOCK):
        jb = min(PROJ_COL_BLOCK, width - j)
        p_ref[rows, j:j + jb] = _dot(h, w_ref[0, :, j:j + jb])


def _shift_rows(x, k, pos, seq_len):
    n = x.shape[0]
    if k == 0:
        return x
    rolled = pltpu.roll(x, (-k) % n, 0)
    ok = (pos >= -k) if k < 0 else (pos <= seq_len - 1 - k)
    return jnp.where(ok, rolled, 0.0)


def _dwconv4(x, w_ref, b_ref, col0, pos, seq_len):
    width = x.shape[1]
    cols = slice(col0, col0 + width)
    acc = x * w_ref[0, 1:2, cols] + b_ref[0, 0:1, cols]
    acc = acc + _shift_rows(x, -1, pos, seq_len) * w_ref[0, 0:1, cols]
    acc = acc + _shift_rows(x, 1, pos, seq_len) * w_ref[0, 2:3, cols]
    acc = acc + _shift_rows(x, 2, pos, seq_len) * w_ref[0, 3:4, cols]
    return acc


def _zero_later_layers(st_ref):
    st_ref[:, 1:] = jnp.zeros((st_ref.shape[0], DEPTH - 1) + tuple(st_ref.shape[2:]), F32)


def _state_output(nseq_total, nseq_tile, tail, l, prev, n_inputs):
    zeros = (0,) * len(tail)
    shape = jax.ShapeDtypeStruct((nseq_total, DEPTH) + tail, F32)
    if l == 0:
        return pl.BlockSpec((nseq_tile, DEPTH) + tail, lambda i: (i, 0) + zeros), shape, [], [], {}
    spec = pl.BlockSpec((nseq_tile, 1) + tail, lambda i: (i, l) + zeros)
    return spec, shape, [pl.BlockSpec(memory_space=pl.ANY)], [prev], {n_inputs: 1}


def _head_lane_expand(x, lane0):
    rows = x.shape[0]
    low = lax.broadcasted_iota(jnp.int32, (rows, HEAD_PAIR), 1) < DK_RET
    blocks = []
    for p in range(H_RET // 2):
        a = jnp.broadcast_to(x[:, lane0 + 2 * p:lane0 + 2 * p + 1], (rows, HEAD_PAIR))
        b = jnp.broadcast_to(x[:, lane0 + 2 * p + 1:lane0 + 2 * p + 2], (rows, HEAD_PAIR))
        blocks.append(jnp.where(low, a, b))
    return jnp.concatenate(blocks, axis=1)


def _seq_call(body, name, x2d, mods, row0, rows_per_mod, gain, w_all, col0, width, l, seq_len, params,
              extra_in, y_width, state_tail, prev_states, scratch):
    n = x2d.shape[0]
    tm = SEQ_TILE
    nseq_tile = tm // seq_len
    emit_state = state_tail is not None
    in_specs = [pl.BlockSpec((tm, D_MODEL), lambda i: (i, 0)), _mod_spec(row0, rows_per_mod, tm, l),
                _layer_spec(gain, l), _w_in_spec(l, col0, width)]
    in_specs += [_layer_spec(a, l) for a in params]
    in_specs += [spec for _, spec in extra_in]
    args = [x2d, mods, gain, w_all] + list(params) + [a for a, _ in extra_in]
    out_specs = [pl.BlockSpec((tm, y_width), lambda i: (i, 0))]
    out_shape = [jax.ShapeDtypeStruct((n, y_width), BF16)]
    aliases = {}
    if emit_state:
        spec, shape, extra_specs, extra_args, aliases = _state_output(
            n // seq_len, nseq_tile, state_tail, l, prev_states, len(args))
        out_specs.append(spec)
        out_shape.append(shape)
        in_specs += extra_specs
        args += extra_args
    return pl.pallas_call(
        functools.partial(body, seq_len=seq_len, emit_state=emit_state, first_layer=l == 0),
        grid=(n // tm,), in_specs=in_specs, out_specs=out_specs, out_shape=out_shape,
        input_output_aliases=aliases, scratch_shapes=scratch, compiler_params=_params(1), name=name,
    )(*args)


def _ret_body(*refs, seq_len, rope, has_s0, emit_state, first_layer):
    it = iter(refs)
    x_ref, mod_ref, g_ref, w_ref = next(it), next(it), next(it), next(it)
    dec_lane_ref = next(it)
    dec_wide_ref = next(it)
    gn_ref = next(it)
    cos_ref = sin_ref = s0_ref = st_ref = qk_ref = None
    if rope:
        cos_ref = next(it)
        sin_ref = next(it)
    if has_s0:
        s0_ref = next(it)
    if emit_state and not first_layer:
        next(it)
    y_ref = next(it)
    if emit_state:
        st_ref = next(it)
    p_ref = next(it)
    dmat_ref = next(it)
    wts_ref = next(it)
    if rope:
        qk_ref = next(it)

    C = SCAN_CHUNK
    nch = seq_len // C
    nseq = p_ref.shape[0] // seq_len
    npair = H_RET // 2
    assert not (has_s0 or rope) or nseq == 1
    if nseq == 1:
        _project(x_ref, mod_ref, g_ref, w_ref, p_ref)
    lgl = _log_sigmoid(dec_lane_ref[0])

    @pl.when(pl.program_id(0) == 0)
    def _():
        lgw = _log_sigmoid(dec_wide_ref[0])
        ii = lax.broadcasted_iota(jnp.int32, (C, C), 0)
        mm = lax.broadcasted_iota(jnp.int32, (C, C), 1)
        dif = (ii - mm).astype(F32)
        neg_inf = -jnp.inf
        for h in range(H_RET):
            df = jnp.exp(jnp.where(ii >= mm, dif * lgw[h:h + 1, :], neg_inf))
            db = jnp.exp(jnp.where(mm >= ii, (-dif) * lgw[H_RET + h:H_RET + h + 1, :], neg_inf))
            dmat_ref[h] = df + db
        ri = lax.broadcasted_iota(jnp.int32, (C, W_RET), 0).astype(F32)
        wts_ref[0] = jnp.exp((C - 1.0 - ri) * lgl[0:1, :])
        wts_ref[1] = jnp.exp(ri * lgl[1:2, :])
        wts_ref[2] = jnp.exp((ri + 1.0) * lgl[0:1, :])
        wts_ref[3] = jnp.exp((C - ri) * lgl[1:2, :])

    lane = lax.broadcasted_iota(jnp.int32, (C, HEAD_PAIR), 1)
    low = lane < DK_RET
    r128 = lax.broadcasted_iota(jnp.int32, (HEAD_PAIR, HEAD_PAIR), 0) >= DK_RET
    c128 = lax.broadcasted_iota(jnp.int32, (HEAD_PAIR, HEAD_PAIR), 1) >= DK_RET
    same_head = r128 == c128
    avg = jnp.where(same_head, 1.0 / DK_RET, 0.0).astype(BF16)

    need_states = emit_state or (has_s0 and nch > 1)
    if has_s0:
        dec_f = jnp.exp(float(C) * lgl[0:1, :])
        dec_b = jnp.exp(float(C) * lgl[1:2, :])

    if rope:
        swap_low = (lax.broadcasted_iota(jnp.int32, (seq_len, HEAD_PAIR), 1) % 32) < 16
        cos = cos_ref[...]
        sin = sin_ref[...]
        for blk in range(2 * npair):
            cols_b = slice(blk * HEAD_PAIR, (blk + 1) * HEAD_PAIR)
            x = p_ref[:, cols_b]
            swapped = jnp.where(swap_low, pltpu.roll(x, HEAD_PAIR - 16, 1), pltpu.roll(x, 16, 1))
            qk_ref[:, cols_b] = x * cos + swapped * sin

    def cols(p):
        return slice(p * HEAD_PAIR, (p + 1) * HEAD_PAIR)

    if emit_state and first_layer:
        _zero_later_layers(st_ref)

    def project_seq(s):
        _project(x_ref, mod_ref, g_ref, w_ref, p_ref, slice(s * seq_len, (s + 1) * seq_len))

    if nseq > 1:
        project_seq(0)
    for s in range(nseq):
        if s + 1 < nseq:
            project_seq(s + 1)

        def rows(c, s=s):
            return slice(s * seq_len + c * C, s * seq_len + (c + 1) * C)

        def get_q(c, p):
            src = qk_ref if rope else p_ref
            return src[rows(c), p * HEAD_PAIR:(p + 1) * HEAD_PAIR]

        def get_k(c, p):
            src = qk_ref if rope else p_ref
            return src[rows(c), W_RET + p * HEAD_PAIR:W_RET + (p + 1) * HEAD_PAIR]

        def get_v(c, p):
            return p_ref[rows(c), 2 * W_RET + p * HEAD_PAIR:2 * W_RET + (p + 1) * HEAD_PAIR]

        ds_f = [[None] * npair for _ in range(nch)]
        ds_b = [[None] * npair for _ in range(nch)]
        if need_states:
            for c in range(nch):
                for p in range(npair):
                    k = get_k(c, p)
                    vb = get_v(c, p).astype(BF16)
                    kf = jnp.transpose(k * wts_ref[0, :, cols(p)]).astype(BF16)
                    kb = jnp.transpose(k * wts_ref[1, :, cols(p)]).astype(BF16)
                    ds_f[c][p] = _dot(kf, vb)
                    ds_b[c][p] = _dot(kb, vb)

        if emit_state:
            for p in range(npair):
                for a in range(2):
                    blk = slice(a * DK_RET, (a + 1) * DK_RET)
                    st_ref[s, 0, 0, 2 * p + a] = ds_f[0][p][blk, blk]
                    st_ref[s, 0, 1, 2 * p + a] = ds_b[0][p][blk, blk]

        sf_in = [[None] * npair for _ in range(nch)]
        sb_in = [[None] * npair for _ in range(nch)]
        if has_s0:
            low_half = lax.broadcasted_iota(jnp.int32, (DK_RET, HEAD_PAIR), 1) < DK_RET

            def pair_state(d, p):
                u = s0_ref[0, 0, d, p]
                return jnp.concatenate([jnp.where(low_half, u, 0.0), jnp.where(low_half, 0.0, u)], axis=0)

            for p in range(npair):
                st = pair_state(0, p)
                for c in range(nch):
                    sf_in[c][p] = st
                    if c + 1 < nch:
                        st = st * dec_f[:, cols(p)] + jnp.where(same_head, ds_f[c][p], 0.0)
                st = pair_state(1, p)
                for c in reversed(range(nch)):
                    sb_in[c][p] = st
                    if c > 0:
                        st = st * dec_b[:, cols(p)] + jnp.where(same_head, ds_b[c][p], 0.0)

        for c in range(nch):
            pairs = range(npair)
            q = [get_q(c, p) for p in pairs]
            kb = [get_k(c, p).astype(BF16) for p in pairs]
            vb = [get_v(c, p).astype(BF16) for p in pairs]
            scores = []
            for p in pairs:
                scores.append(_dot_nt(jnp.where(low, q[p], 0.0).astype(BF16), kb[p]))
                scores.append(_dot_nt(jnp.where(low, 0.0, q[p]).astype(BF16), kb[p]))
            inter = [None] * npair
            if has_s0:
                for p in pairs:
                    inter[p] = (_dot((q[p] * wts_ref[2, :, cols(p)]).astype(BF16), sf_in[c][p].astype(BF16))
                                + _dot((q[p] * wts_ref[3, :, cols(p)]).astype(BF16), sb_in[c][p].astype(BF16)))
            o = []
            for p in pairs:
                o0 = _dot((scores[2 * p] * dmat_ref[2 * p]).astype(BF16), vb[p])
                o1 = _dot((scores[2 * p + 1] * dmat_ref[2 * p + 1]).astype(BF16), vb[p])
                op = jnp.where(low, o0, o1)
                o.append(op + inter[p] if has_s0 else op)
            mu = [_dot(o[p].astype(BF16), avg) for p in pairs]
            d = [o[p] - mu[p] for p in pairs]
            var = [_dot((d[p] * d[p]).astype(BF16), avg) for p in pairs]
            for p in pairs:
                on = d[p] * lax.rsqrt(var[p] + EPS)
                g = p_ref[rows(c), 3 * W_RET + p * HEAD_PAIR:3 * W_RET + (p + 1) * HEAD_PAIR]
                y_ref[rows(c), cols(p)] = (on * gn_ref[0, 0:1, cols(p)] * _silu(g)).astype(BF16)


def _retention(x2d, mods, row0, rows_per_mod, gain, w_all, dec_lane, dec_wide, gn, seq_len, l,
               rope_tabs=None, s0=None, prev_states=None):
    rope = rope_tabs is not None
    has_s0 = s0 is not None
    npair = H_RET // 2
    extra = []
    if rope:
        extra += [(t, _const_spec(t.shape)) for t in rope_tabs]
    if has_s0:
        extra.append((s0, pl.BlockSpec((1, 1, 2, npair, DK_RET, HEAD_PAIR), lambda i: (i, l, 0, 0, 0, 0))))
    scratch = [pltpu.VMEM((SEQ_TILE, W_GRP_R), F32),
               pltpu.VMEM((H_RET, SCAN_CHUNK, SCAN_CHUNK), F32), pltpu.VMEM((4, SCAN_CHUNK, W_RET), F32)]
    if rope:
        scratch.append(pltpu.VMEM((seq_len, 2 * W_RET), F32))
    return _seq_call(functools.partial(_ret_body, rope=rope, has_s0=has_s0),
                     "retention_lat" if rope else "retention_ctx",
                     x2d, mods, row0, rows_per_mod, gain, w_all, COL_R, W_GRP_R, l, seq_len,
                     (dec_lane, dec_wide, gn), extra, W_RET,
                     None if has_s0 else (2, H_RET, DK_RET, DK_RET), prev_states, scratch)


def _ssd_body(*refs, seq_len, has_s0, emit_state, first_layer):
    it = iter(refs)
    x_ref, mod_ref, g_ref, w_ref = next(it), next(it), next(it), next(it)
    cw_ref = next(it)
    cb_ref = next(it)
    dtb_ref = next(it)
    alog_ref = next(it)
    dskip_ref = next(it)
    gn_ref = next(it)
    s0_ref = st_ref = None
    if has_s0:
        s0_ref = next(it)
    if emit_state and not first_layer:
        next(it)
    y_ref = next(it)
    if emit_state:
        st_ref = next(it)
    p_ref = next(it)
    xs_ref = next(it)
    bc_ref = next(it)

    T = seq_len
    C = SCAN_CHUNK
    nch = T // C
    n_rows = p_ref.shape[0]
    nseq = n_rows // T
    assert not has_s0 or nseq == 1
    assert T & (T - 1) == 0
    pos = lax.broadcasted_iota(jnp.int32, (T, LANES), 0)

    def project_and_conv(s):
        seq_rows = slice(s * T, (s + 1) * T)
        _project(x_ref, mod_ref, g_ref, w_ref, p_ref, seq_rows)
        for blk in range(CONV_CH // LANES):
            x = p_ref[seq_rows, W_SSD + blk * LANES:W_SSD + (blk + 1) * LANES]
            a = _silu(_dwconv4(x, cw_ref, cb_ref, blk * LANES, pos, T))
            if blk < W_SSD // LANES:
                xs_ref[seq_rows, blk * LANES:(blk + 1) * LANES] = a
            else:
                o = blk * LANES - W_SSD
                bc_ref[seq_rows, o:o + LANES] = a

    a_neg = -jnp.exp(alog_ref[0])
    ii = lax.broadcasted_iota(jnp.int32, (C, C), 0)
    mm = lax.broadcasted_iota(jnp.int32, (C, C), 1)
    lower = ii >= mm
    upper = mm >= ii
    tri_l = jnp.where(lower, 1.0, 0.0).astype(BF16)
    tri_u = jnp.where(upper, 1.0, 0.0).astype(BF16)
    neg_inf = -jnp.inf
    grp0 = lax.broadcasted_iota(jnp.int32, (C, W_SSD), 1) < (W_SSD // G_SSD)
    grp0_s = lax.broadcasted_iota(jnp.int32, (N_SSD, W_SSD), 1) < (W_SSD // G_SSD)
    grp0_t = lax.broadcasted_iota(jnp.int32, (W_SSD, N_SSD), 0) < (W_SSD // G_SSD)
    low = lax.broadcasted_iota(jnp.int32, (C, HEAD_PAIR), 1) < P_SSD

    if emit_state and first_layer:
        _zero_later_layers(st_ref)

    scalars = {}

    def chunk_scalars(s, c):
        if (s, c) not in scalars:
            rows = slice(s * T + c * C, s * T + (c + 1) * C)
            dt = _softplus(p_ref[rows, W_SSD + CONV_CH:W_SSD + CONV_CH + LANES] + dtb_ref[0])
            la = dt * a_neg
            a1 = la.astype(BF16)
            r1 = la - a1.astype(F32)
            a2 = r1.astype(BF16)
            a3 = (r1 - a2.astype(F32)).astype(BF16)
            pre = _dot(tri_l, a1) + _dot(tri_l, a2) + _dot(tri_l, a3)
            suf = _dot(tri_u, a1) + _dot(tri_u, a2) + _dot(tri_u, a3)
            scalars[(s, c)] = (dt, pre, suf)
        return scalars[(s, c)]

    def light_stage(s):
        def rows(c):
            return slice(s * T + c * C, s * T + (c + 1) * C)

        def state_weights(c):
            dt, pre, suf = chunk_scalars(s, c)
            wf = jnp.exp(pre[C - 1:C, :] - pre) * dt
            wb = jnp.exp(suf[0:1, :] - suf) * dt
            xs = xs_ref[rows(c), :]
            return (xs * _head_lane_expand(wf, 0), xs * _head_lane_expand(wb, DT_BWD_LANE), pre, suf)

        if emit_state:
            xf, xb, _, _ = state_weights(0)
            bm = [bc_ref[rows(0), g * N_SSD:(g + 1) * N_SSD].astype(BF16) for g in range(G_SSD)]
            for d, xw in enumerate((xf, xb)):
                xt = jnp.transpose(xw).astype(BF16)
                st = jnp.where(grp0_t, _dot(xt, bm[0]), _dot(xt, bm[1]))
                for h in range(H_SSD):
                    st_ref[s, 0, d, h] = st[h * P_SSD:(h + 1) * P_SSD, :]

        ds_f = [None] * nch
        ds_b = [None] * nch
        dec_f = [None] * nch
        dec_b = [None] * nch
        if has_s0 and nch > 1:
            for c in range(nch):
                xf, xb, pre, suf = state_weights(c)
                bt0 = jnp.transpose(bc_ref[rows(c), 0:N_SSD]).astype(BF16)
                bt1 = jnp.transpose(bc_ref[rows(c), N_SSD:2 * N_SSD]).astype(BF16)
                ds_f[c] = jnp.where(grp0_s, _dot(bt0, xf.astype(BF16)), _dot(bt1, xf.astype(BF16)))
                ds_b[c] = jnp.where(grp0_s, _dot(bt0, xb.astype(BF16)), _dot(bt1, xb.astype(BF16)))
                dec_f[c] = _head_lane_expand(jnp.exp(pre[C - 1:C, :]), 0)
                dec_b[c] = _head_lane_expand(jnp.exp(suf[0:1, :]), DT_BWD_LANE)

        sf_in = [None] * nch
        sb_in = [None] * nch
        if has_s0:
            st = s0_ref[0, 0, 0]
            for c in range(nch):
                sf_in[c] = st
                if c + 1 < nch:
                    st = st * dec_f[c] + ds_f[c]
            st = s0_ref[0, 0, 1]
            for c in reversed(range(nch)):
                sb_in[c] = st
                if c > 0:
                    st = st * dec_b[c] + ds_b[c]

        staged = []
        for c in range(nch):
            dt, pre, suf = chunk_scalars(s, c)
            cm = [bc_ref[rows(c), 2 * N_SSD + g * N_SSD:2 * N_SSD + (g + 1) * N_SSD].astype(BF16)
                  for g in range(G_SSD)]
            bm = [bc_ref[rows(c), g * N_SSD:(g + 1) * N_SSD].astype(BF16) for g in range(G_SSD)]
            gram = [_dot_nt(cm[g], bm[g]) for g in range(G_SSD)]
            inter = None
            if has_s0:
                sf = sf_in[c].astype(BF16)
                sb = sb_in[c].astype(BF16)
                inter = (jnp.where(grp0, _dot(cm[0], sf), _dot(cm[1], sf)),
                         jnp.where(grp0, _dot(cm[0], sb), _dot(cm[1], sb)))
            log_dt = jnp.log(dt)
            staged.append((pre, suf, jnp.transpose(pre - log_dt), jnp.transpose(suf - log_dt), gram, inter))
        return staged

    def heavy_stage(s, staged):
        for c in range(nch):
            rows_c = slice(s * T + c * C, s * T + (c + 1) * C)
            pre, suf, pre_t, suf_t, gram, inter = staged[c]
            xs = xs_ref[rows_c, :]
            xs_b = xs.astype(BF16)
            heads = []
            for h in range(H_SSD):
                g = h // (H_SSD // G_SSD)
                hb = DT_BWD_LANE + h
                df = jnp.exp(jnp.where(lower, pre[:, h:h + 1] - pre_t[h:h + 1, :], neg_inf))
                db = jnp.exp(jnp.where(upper, suf[:, hb:hb + 1] - suf_t[hb:hb + 1, :], neg_inf))
                w = gram[g] * (df + db)
                p = h // 2
                heads.append(_dot(w.astype(BF16), xs_b[:, p * HEAD_PAIR:(p + 1) * HEAD_PAIR]))
            o = jnp.concatenate([jnp.where(low, heads[2 * p], heads[2 * p + 1]) for p in range(H_SSD // 2)],
                                axis=1)
            if has_s0:
                o = o + _head_lane_expand(jnp.exp(pre), 0) * inter[0]
                o = o + _head_lane_expand(jnp.exp(suf), DT_BWD_LANE) * inter[1]
            y = o + dskip_ref[0] * xs
            yz = y * _silu(p_ref[rows_c, 0:W_SSD])
            y_ref[rows_c, :] = (_rms_scale(yz) * gn_ref[0]).astype(BF16)

    project_and_conv(0)
    staged = light_stage(0)
    for s in range(nseq):
        staged_next = None
        if s + 1 < nseq:
            project_and_conv(s + 1)
            staged_next = light_stage(s + 1)
        heavy_stage(s, staged)
        staged = staged_next


def _ssd(x2d, mods, row0, rows_per_mod, gain, w_all, cw, cb, dtb, alog, dskip, gn, seq_len, l,
         s0=None, prev_states=None):
    has_s0 = s0 is not None
    extra = []
    if has_s0:
        extra.append((s0, pl.BlockSpec((1, 1, 2, N_SSD, W_SSD), lambda i: (i, l, 0, 0, 0))))
    scratch = [pltpu.VMEM((SEQ_TILE, W_GRP_S), F32), pltpu.VMEM((SEQ_TILE, W_SSD), F32),
               pltpu.VMEM((SEQ_TILE, 2 * G_SSD * N_SSD), F32)]
    return _seq_call(functools.partial(_ssd_body, has_s0=has_s0), "ssd_lat" if has_s0 else "ssd_ctx",
                     x2d, mods, row0, rows_per_mod, gain, w_all, COL_S, W_SLOT_S, l, seq_len,
                     (cw, cb, dtb, alog, dskip, gn), extra, W_SSD,
                     None if has_s0 else (2, H_SSD, P_SSD, N_SSD), prev_states, scratch)


def _lru_body(*refs, seq_len, has_h0, emit_state, first_layer):
    it = iter(refs)
    x_ref, mod_ref, g_ref, w_ref = next(it), next(it), next(it), next(it)
    cw_ref = next(it)
    cb_ref = next(it)
    wg_ref = next(it)
    bg_ref = next(it)
    lam_ref = next(it)
    h0_ref = st_ref = None
    if has_h0:
        h0_ref = next(it)
    if emit_state and not first_layer:
        next(it)
    y_ref = next(it)
    if emit_state:
        st_ref = next(it)
    p_ref = next(it)

    T = seq_len
    n_rows = p_ref.shape[0]
    nseq = n_rows // T
    assert not has_h0 or nseq == 1
    assert T & (T - 1) == 0
    _project(x_ref, mod_ref, g_ref, w_ref, p_ref)
    row = lax.broadcasted_iota(jnp.int32, (n_rows, W_LRU), 0)
    pos = row & (T - 1)
    sub = row & (SUBLANES - 1)
    xc = _dwconv4(p_ref[:, 0:W_LRU], cw_ref, cb_ref, 0, pos, T)
    gates = _sigmoid(_dot(xc.astype(BF16), wg_ref[0]) + bg_ref[0])
    decay_rate = _softplus(-lam_ref[0])
    if emit_state and first_layer:
        _zero_later_layers(st_ref)
    nblk = T // SUBLANES
    total = None
    for d in range(2):
        r = gates[:, 2 * d * W_LRU:(2 * d + 1) * W_LRU]
        i = gates[:, (2 * d + 1) * W_LRU:(2 * d + 2) * W_LRU]
        log_a = (-LRU_C * r) * decay_rate[d:d + 1, :]
        a = jnp.exp(log_a)
        th = jnp.tanh(log_a)
        b = jnp.sqrt(-2.0 * th / (1.0 - th)) * (i * xc)
        step = 1
        while step < SUBLANES:
            if d == 0:
                ok = sub >= step
                shift = step
            else:
                ok = sub <= SUBLANES - 1 - step
                shift = n_rows - step
            a_prev = jnp.where(ok, pltpu.roll(a, shift, 0), 1.0)
            b_prev = jnp.where(ok, pltpu.roll(b, shift, 0), 0.0)
            b = a * b_prev + b
            a = a * a_prev
            step *= 2
        edge = SUBLANES - 1 if d == 0 else 0
        tiles = [None] * (nseq * nblk)
        for s in range(nseq):
            carry = h0_ref[0, 0, d:d + 1, :] if has_h0 else None
            for v in (range(nblk) if d == 0 else reversed(range(nblk))):
                t = s * nblk + v
                rows = slice(t * SUBLANES, (t + 1) * SUBLANES)
                hv = b[rows]
                if carry is not None:
                    hv = hv + a[rows] * carry
                tiles[t] = hv
                carry = hv[edge:edge + 1, :]
            if emit_state:
                st_ref[s, 0, d:d + 1, :] = carry
        hs = jnp.concatenate(tiles, axis=0)
        total = hs if total is None else total + hs
    y_ref[...] = (total * _gelu_tanh(p_ref[:, W_LRU:2 * W_LRU])).astype(BF16)


def _lru(x2d, mods, row0, rows_per_mod, gain, w_all, cw, cb, wg, bg, lam, seq_len, l, h0=None, prev_states=None):
    has_h0 = h0 is not None
    extra = []
    if has_h0:
        extra.append((h0, pl.BlockSpec((1, 1, 2, W_LRU), lambda i: (i, l, 0, 0))))
    scratch = [pltpu.VMEM((SEQ_TILE, W_GRP_L), F32)]
    return _seq_call(functools.partial(_lru_body, has_h0=has_h0), "lru_lat" if has_h0 else "lru_ctx",
                     x2d, mods, row0, rows_per_mod, gain, w_all, COL_L, W_GRP_L, l, seq_len,
                     (cw, cb, wg, bg, lam), extra, W_LRU,
                     None if has_h0 else (2, W_LRU), prev_states, scratch)


def _outproj_body(x_ref, yr_ref, ys_ref, yl_ref, mod_ref, g_ref, wo_ref, x1_ref, h2_ref):
    mix = _dot(yr_ref[...], wo_ref[0, 0:W_RET, :])
    mix = mix + _dot(ys_ref[...], wo_ref[0, W_RET:W_RET + W_SSD, :])
    mix = mix + _dot(yl_ref[...], wo_ref[0, W_RET + W_SSD:D_MODEL, :])
    x1 = x_ref[...] + mod_ref[0, 0, 2:3, :] * mix
    x1_ref[...] = x1
    y = _rms_scale(x1) * g_ref[0]
    h2_ref[...] = (y * (1.0 + mod_ref[0, 0, 4:5, :]) + mod_ref[0, 0, 3:4, :]).astype(BF16)


def _out_proj(x2d, y_ret, y_ssd, y_lru, mods, row0, rows_per_mod, gain, wo, l):
    n = x2d.shape[0]
    tm = ROW_TILE
    row_spec = lambda w: pl.BlockSpec((tm, w), lambda i: (i, 0))
    return pl.pallas_call(
        _outproj_body,
        grid=(n // tm,),
        in_specs=[row_spec(D_MODEL), row_spec(W_RET), row_spec(W_SSD), row_spec(W_LRU),
                  _mod_spec(row0, rows_per_mod, tm, l), _layer_spec(gain, l), _layer_spec(wo, l)],
        out_specs=[row_spec(D_MODEL), row_spec(D_MODEL)],
        out_shape=[jax.ShapeDtypeStruct((n, D_MODEL), F32), jax.ShapeDtypeStruct((n, D_MODEL), BF16)],
        compiler_params=_params(1),
        name="out_proj",
    )(x2d, y_ret, y_ssd, y_lru, mods, gain, wo)


def _ffn_core(h, wup_ref, cw_ref, cb_ref, wd_ref, act_ref, seq_len):
    tm = h.shape[0]
    nseq = tm // seq_len
    row8 = lax.broadcasted_iota(jnp.int32, (SUBLANES, FF_BLOCK), 0)
    first_row = row8 == 0
    last_row = row8 == SUBLANES - 1

    def conv3(u, col0):
        cols = slice(col0, col0 + FF_BLOCK)
        w_prev, w_mid, w_next = (cw_ref[0, t:t + 1, cols] for t in range(3))
        bias = cb_ref[0, 0:1, cols]
        outs = []
        for s in range(nseq):
            us = u[s * seq_len:(s + 1) * seq_len]
            prev = pltpu.roll(us, 1, 0)
            nxt = pltpu.roll(us, seq_len - 1, 0)
            prev = jnp.concatenate([jnp.where(first_row, 0.0, prev[:SUBLANES]), prev[SUBLANES:]], axis=0)
            nxt = jnp.concatenate([nxt[:seq_len - SUBLANES], jnp.where(last_row, 0.0, nxt[seq_len - SUBLANES:])],
                                  axis=0)
            outs.append(us * w_mid + bias + prev * w_prev + nxt * w_next)
        return outs[0] if nseq == 1 else jnp.concatenate(outs, axis=0)

    def up(j):
        return (_dot(h, wup_ref[0, :, j * FF_BLOCK:(j + 1) * FF_BLOCK]),
                _dot(h, wup_ref[0, :, D_FF + j * FF_BLOCK:D_FF + (j + 1) * FF_BLOCK]))

    for j in range(N_FF_BLOCKS):
        uv, ug = up(j)
        val = conv3(uv, j * FF_BLOCK)
        gate = conv3(ug, D_FF + j * FF_BLOCK)
        act_ref[:, j * FF_BLOCK:(j + 1) * FF_BLOCK] = (_silu(gate) * val).astype(BF16)
    return _dot(act_ref[...], wd_ref[0])


def _ffn_body(h_ref, x1_ref, mod_ref, wup_ref, cw_ref, cb_ref, wd_ref, fg_ref, o_ref, act_ref, *, seq_len,
              final_norm):
    down = _ffn_core(h_ref[...], wup_ref, cw_ref, cb_ref, wd_ref, act_ref, seq_len)
    out = x1_ref[...] + mod_ref[0, 0, 5:6, :] * down
    if final_norm:
        out = _rms_scale(out) * fg_ref[...]
    o_ref[...] = out


def _mix_ffn_body(x_ref, yr_ref, ys_ref, yl_ref, mod_ref, g_ref, wo_ref, wup_ref, cw_ref, cb_ref, wd_ref, fg_ref,
                  o_ref, act_ref, h_ref, *, seq_len, final_norm):
    mix = _dot(yr_ref[...], wo_ref[0, 0:W_RET, :])
    mix = mix + _dot(ys_ref[...], wo_ref[0, W_RET:W_RET + W_SSD, :])
    mix = mix + _dot(yl_ref[...], wo_ref[0, W_RET + W_SSD:D_MODEL, :])
    x1 = x_ref[...] + mod_ref[0, 0, 2:3, :] * mix
    o_ref[...] = x1
    y = _rms_scale(x1) * g_ref[0]
    h_ref[...] = (y * (1.0 + mod_ref[0, 0, 4:5, :]) + mod_ref[0, 0, 3:4, :]).astype(BF16)
    down = _ffn_core(h_ref[...], wup_ref, cw_ref, cb_ref, wd_ref, act_ref, seq_len)
    out = o_ref[...] + mod_ref[0, 0, 5:6, :] * down
    if final_norm:
        out = _rms_scale(out) * fg_ref[...]
    o_ref[...] = out


def _mix_ffn(x2d, y_ret, y_ssd, y_lru, mods, row0, rows_per_mod, gain, wo, wup, cw, cb, wd, fgain, seq_len,
             final_norm, l):
    n = x2d.shape[0]
    tm = ROW_TILE
    assert tm % seq_len == 0
    row_spec = lambda w: pl.BlockSpec((tm, w), lambda i: (i, 0))
    return pl.pallas_call(
        functools.partial(_mix_ffn_body, seq_len=seq_len, final_norm=final_norm),
        grid=(n // tm,),
        in_specs=[row_spec(D_MODEL), row_spec(W_RET), row_spec(W_SSD), row_spec(W_LRU),
                  _mod_spec(row0, rows_per_mod, tm, l), _layer_spec(gain, l), _layer_spec(wo, l),
                  _layer_spec(wup, l), _layer_spec(cw, l), _layer_spec(cb, l), _layer_spec(wd, l),
                  _const_spec(fgain.shape)],
        out_specs=row_spec(D_MODEL),
        out_shape=jax.ShapeDtypeStruct((n, D_MODEL), F32),
        scratch_shapes=[pltpu.VMEM((tm, D_FF), BF16), pltpu.VMEM((tm, D_MODEL), BF16)],
        compiler_params=_params(1),
        name="mix_ffn",
    )(x2d, y_ret, y_ssd, y_lru, mods, gain, wo, wup, cw, cb, wd, fgain)


def _ffn(h2, x1, mods, row0, rows_per_mod, wup, cw, cb, wd, fgain, seq_len, final_norm, l):
    n = h2.shape[0]
    tm = SEQ_TILE
    row_spec = pl.BlockSpec((tm, D_MODEL), lambda i: (i, 0))
    return pl.pallas_call(
        functools.partial(_ffn_body, seq_len=seq_len, final_norm=final_norm),
        grid=(n // tm,),
        in_specs=[row_spec, row_spec, _mod_spec(row0, rows_per_mod, tm, l),
                  _layer_spec(wup, l), _layer_spec(cw, l), _layer_spec(cb, l), _layer_spec(wd, l),
                  _const_spec(fgain.shape)],
        out_specs=row_spec,
        out_shape=jax.ShapeDtypeStruct((n, D_MODEL), F32),
        scratch_shapes=[pltpu.VMEM((tm, D_FF), BF16)],
        compiler_params=_params(1),
        name="ffn",
    )(h2, x1, mods, wup, cw, cb, wd, fgain)


def _rope_tables(t_len):
    tok = jnp.arange(t_len)
    row_pos = (tok // GRID_W).astype(F32)
    col_pos = (tok % GRID_W).astype(F32)
    lane = jnp.arange(HEAD_PAIR)
    d = lane % DK_RET
    use_row = d < (DK_RET // 2)
    half = DK_RET // 4
    freqs = ROPE_BASE ** (-(d % half).astype(F32) / half)
    ang = jnp.where(use_row[None, :], row_pos[:, None], col_pos[:, None]) * freqs[None, :]
    first = (d % (2 * half)) < half
    return jnp.cos(ang), jnp.where(first[None, :], -jnp.sin(ang), jnp.sin(ang))


def _prep_params(w_in, ret_decay, ret_norm_g, ssd_conv_b, ssd_dt_bias, ssd_a_log, ssd_d, ssd_norm_g,
                 lru_conv_b, lru_w_a, lru_b_a, lru_w_x, lru_b_x, w_out, ffn_w_up, ffn_conv_b, ffn_w_down):
    w_all = _regroup_w_in(w_in)

    def lane_place(pair):
        z = jnp.zeros((DEPTH, LANES), F32)
        return jnp.concatenate([pair[:, 0], z[:, :DT_BWD_LANE - H_SSD], pair[:, 1],
                                z[:, :LANES - DT_BWD_LANE - H_SSD]], axis=-1)[:, None, :]

    eye = jnp.eye(LRU_BLOCKS, dtype=F32)
    gates = jnp.stack([lru_w_a, lru_w_x], axis=2)
    wgate = jnp.einsum("lxgncd,nm->lncxgmd", gates, eye).reshape(DEPTH, W_LRU, 4 * W_LRU).astype(BF16)
    bgate = jnp.stack([lru_b_a, lru_b_x], axis=2).reshape(DEPTH, 1, 4 * W_LRU)
    return dict(
        w_all=w_all,
        dec_lane=jnp.repeat(ret_decay, DK_RET, axis=-1),
        dec_wide=jnp.broadcast_to(ret_decay.reshape(DEPTH, 2 * H_RET, 1), (DEPTH, 2 * H_RET, SCAN_CHUNK)),
        ret_gn=ret_norm_g[:, None, :],
        ssd_cb=ssd_conv_b[:, None, :],
        ssd_dtb=lane_place(ssd_dt_bias), ssd_alog=lane_place(ssd_a_log),
        ssd_dskip=jnp.repeat(ssd_d, P_SSD, axis=-1)[:, None, :], ssd_gn=ssd_norm_g[:, None, :],
        lru_cb=lru_conv_b[:, None, :], lru_wg=wgate, lru_bg=bgate,
        wo=w_out.astype(BF16), wup=ffn_w_up.astype(BF16), ffn_cb=ffn_conv_b[:, None, :],
        wd=ffn_w_down.astype(BF16),
    )


def kernel(x_prompt, x_sample, c, c_ctx, state_ret, state_ssd, state_lru, w_ada, b_ada, norm1_g, norm2_g, w_in, ret_decay, ret_norm_g, ssd_conv_w, ssd_conv_b, ssd_dt_bias, ssd_a_log, ssd_d, ssd_norm_g, lru_conv_w, lru_conv_b, lru_w_a, lru_b_a, lru_w_x, lru_b_x, lru_lambda, w_out, ffn_w_up, ffn_conv_w, ffn_conv_b, ffn_w_down, final_norm_g):
    bp, tp, _ = x_prompt.shape
    bs, ts, _ = x_sample.shape
    assert tp == SCAN_CHUNK and ts == SEQ_TILE and ts % GRID_W == 0 and (bp * tp) % SEQ_TILE == 0

    n_req = 8
    cvec = jnp.concatenate([c_ctx[None, :], c, jnp.zeros((n_req - 1 - bs, D_MODEL), F32)], axis=0)
    mods = _modulation(cvec, w_ada, b_ada).reshape(DEPTH, n_req, 6, D_MODEL)

    pp = _prep_params(w_in, ret_decay, ret_norm_g, ssd_conv_b, ssd_dt_bias, ssd_a_log, ssd_d, ssd_norm_g,
                      lru_conv_b, lru_w_a, lru_b_a, lru_w_x, lru_b_x, w_out, ffn_w_up, ffn_conv_b, ffn_w_down)
    n1 = norm1_g[:, None, :]
    n2 = norm2_g[:, None, :]
    rope_tabs = _rope_tables(ts)
    fgain = final_norm_g[None, :]

    npair = H_RET // 2
    s0_ret = state_ret.reshape(bs, DEPTH, 2, npair, 2, DK_RET, DK_RET).transpose(0, 1, 2, 3, 5, 4, 6).reshape(
        bs, DEPTH, 2, npair, DK_RET, HEAD_PAIR)
    s0_ssd = state_ssd.transpose(0, 1, 2, 4, 3, 5).reshape(bs, DEPTH, 2, N_SSD, W_SSD)

    def trunk_layer(x2d, l, seq_len, row0, rows_per_mod, latent, prev=(None, None, None)):
        common = (x2d, mods, row0, rows_per_mod, n1, pp["w_all"])
        ret = _retention(*common, pp["dec_lane"], pp["dec_wide"], pp["ret_gn"], seq_len, l,
                         rope_tabs if latent else None, s0_ret if latent else None, prev[0])
        ssd = _ssd(*common, ssd_conv_w, pp["ssd_cb"], pp["ssd_dtb"], pp["ssd_alog"], pp["ssd_dskip"],
                   pp["ssd_gn"], seq_len, l, s0_ssd if latent else None, prev[1])
        lru = _lru(*common, lru_conv_w, pp["lru_cb"], pp["lru_wg"], pp["lru_bg"], lru_lambda, seq_len, l,
                   state_lru if latent else None, prev[2])
        ffn_args = (pp["wup"], ffn_conv_w, pp["ffn_cb"], pp["wd"], fgain, seq_len, l == DEPTH - 1, l)
        if ROW_TILE % seq_len == 0:
            out = _mix_ffn(x2d, ret[0], ssd[0], lru[0], mods, row0, rows_per_mod, n2, pp["wo"], *ffn_args)
        else:
            x1, h2 = _out_proj(x2d, ret[0], ssd[0], lru[0], mods, row0, rows_per_mod, n2, pp["wo"], l)
            out = _ffn(h2, x1, mods, row0, rows_per_mod, *ffn_args)
        states = None if latent else (ret[1], ssd[1], lru[1])
        return out, states

    xp = x_prompt.reshape(bp * tp, D_MODEL)
    xs = x_sample.reshape(bs * ts, D_MODEL)
    states = (None, None, None)
    for l in range(DEPTH):
        xp, states = trunk_layer(xp, l, tp, 0, bp * tp, False, states)
        xs, _ = trunk_layer(xs, l, ts, 1, ts, True)

    return (xp.reshape(bp, tp, D_MODEL), xs.reshape(bs, ts, D_MODEL), states[0],
            jnp.swapaxes(states[1], -1, -2), states[2])
```

```python
import functools
import math

import jax
import jax.numpy as jnp
from jax import lax
from jax.experimental import pallas as pl
from jax.experimental.pallas import tpu as pltpu

F32 = jnp.float32
BF16 = jnp.bfloat16

D_MODEL = 1024
DEPTH = 2
GRID_W = 64
W_RET = 384
H_RET = 6
DK_RET = 64
W_SSD = 384
P_SSD = 64
H_SSD = 6
N_SSD = 128
G_SSD = 2
CONV_CH = W_SSD + 2 * G_SSD * N_SSD
W_LRU = 256
LRU_BLOCKS = 4
LRU_BW = 64
LRU_C = 8.0
D_FF = 2816
ROPE_BASE = 10000.0
EPS = 1e-6

LANES = 128
SUBLANES = 8
HEAD_PAIR = LANES
SCAN_CHUNK = 256
FF_BLOCK = 256
N_FF_BLOCKS = D_FF // FF_BLOCK
ROW_TILE = 512
SEQ_TILE = 1024
PROJ_COL_BLOCK = 512
VMEM_LIMIT = 56 * 1024 * 1024

W_GRP_R = 4 * W_RET
W_GRP_S = W_SSD + CONV_CH + LANES
W_GRP_L = 2 * W_LRU
W_SLOT_S = W_GRP_R
COL_R, COL_S, COL_L = 0, W_GRP_R, W_GRP_R + W_SLOT_S
W_IN_PAD = COL_L + W_GRP_L
DT_BWD_LANE = 8


def _dot(a, b):
    return jnp.dot(a, b, preferred_element_type=F32)


def _dot_nt(a, b):
    return lax.dot_general(a, b, (((1,), (1,)), ((), ())), preferred_element_type=F32)


def _sigmoid(x):
    return 1.0 / (1.0 + jnp.exp(-x))


def _silu(x):
    return x * _sigmoid(x)


def _softplus(x):
    return jnp.maximum(x, 0.0) + jnp.log1p(jnp.exp(-jnp.abs(x)))


def _log_sigmoid(x):
    return -_softplus(-x)


def _gelu_tanh(x):
    c = math.sqrt(2.0 / math.pi)
    return 0.5 * x * (1.0 + jnp.tanh(c * (x + 0.044715 * (x * x * x))))


def _rms_scale(x):
    return x * lax.rsqrt(jnp.mean(x * x, axis=-1, keepdims=True) + EPS)


def _const_spec(shape):
    zeros = (0,) * len(shape)
    return pl.BlockSpec(shape, lambda *_: zeros, pipeline_mode=pl.Buffered(1))


def _layer_spec(arr, l):
    rest = (0,) * (arr.ndim - 1)
    return pl.BlockSpec((1,) + arr.shape[1:], lambda *_: (l,) + rest, pipeline_mode=pl.Buffered(1))


def _w_in_spec(l, col0, width):
    assert col0 % width == 0
    return pl.BlockSpec((1, D_MODEL, width), lambda *_: (l, 0, col0 // width), pipeline_mode=pl.Buffered(1))


def _mod_spec(row0, rows_per_mod, tm, l):
    return pl.BlockSpec((1, 1, 6, D_MODEL), lambda i: (l, row0 + i * tm // rows_per_mod, 0, 0))


def _params(n_axes):
    return pltpu.CompilerParams(dimension_semantics=("arbitrary",) * n_axes,
                                vmem_limit_bytes=VMEM_LIMIT)


def _mod_body(c_ref, w_ref, b_ref, o_ref):
    s = _silu(c_ref[...]).astype(BF16)
    o_ref[0] = _dot(s, w_ref[0].astype(BF16)) + b_ref[0]


def _modulation(cvec, w_ada, b_ada):
    rows = cvec.shape[0]
    nblk = (6 * D_MODEL) // D_MODEL
    return pl.pallas_call(
        _mod_body,
        grid=(DEPTH, nblk),
        in_specs=[pl.BlockSpec((rows, D_MODEL), lambda l, j: (0, 0)),
                  pl.BlockSpec((1, D_MODEL, D_MODEL), lambda l, j: (l, 0, j)),
                  pl.BlockSpec((1, 1, D_MODEL), lambda l, j: (l, 0, j))],
        out_specs=pl.BlockSpec((1, rows, D_MODEL), lambda l, j: (l, 0, j)),
        out_shape=jax.ShapeDtypeStruct((DEPTH, rows, 6 * D_MODEL), F32),
        compiler_params=_params(2),
        name="adaln_mod",
    )(cvec, w_ada, b_ada.reshape(DEPTH, 1, 6 * D_MODEL))


IN_DIM = 4 * W_RET + W_SSD + CONV_CH + H_SSD + 2 * W_LRU
REGROUP_ROWS = 256


def _regroup_body(w_ref, o_ref):
    rows = w_ref.shape[1]
    o_dt = COL_S + W_SSD + CONV_CH
    o_ref[0, :, 0:W_RET] = (w_ref[0, :, 0:W_RET] * (DK_RET ** -0.5)).astype(BF16)
    o_ref[0, :, W_RET:o_dt] = w_ref[0, :, W_RET:o_dt].astype(BF16)
    tail = w_ref[0, :, o_dt:IN_DIM]
    first = tail[:, 0:LANES]
    lane = lax.broadcasted_iota(jnp.int32, (rows, LANES), 1)
    dt_fwd = jnp.where(lane < H_SSD, first, 0.0)
    dt_bwd = jnp.where((lane >= DT_BWD_LANE) & (lane < DT_BWD_LANE + H_SSD), pltpu.roll(first, DT_BWD_LANE, 1), 0.0)
    o_ref[0, :, o_dt:o_dt + LANES] = (dt_fwd + dt_bwd).astype(BF16)
    o_ref[0, :, o_dt + LANES:COL_L] = jnp.zeros((rows, COL_L - o_dt - LANES), BF16)
    o_ref[0, :, COL_L:W_IN_PAD] = tail[:, H_SSD:H_SSD + W_GRP_L].astype(BF16)


def _regroup_w_in(w_in):
    rb = REGROUP_ROWS
    return pl.pallas_call(
        _regroup_body,
        grid=(DEPTH, D_MODEL // rb),
        in_specs=[pl.BlockSpec((1, rb, IN_DIM), lambda l, i: (l, i, 0))],
        out_specs=pl.BlockSpec((1, rb, W_IN_PAD), lambda l, i: (l, i, 0)),
        out_shape=jax.ShapeDtypeStruct((DEPTH, D_MODEL, W_IN_PAD), BF16),
        compiler_params=_params(2),
        name="w_in_regroup",
    )(w_in)


def _project(x_ref, mod_ref, g_ref, w_ref, p_ref, rows=slice(None)):
    gain = g_ref[0] * (1.0 + mod_ref[0, 0, 1:2, :])
    h = (_rms_scale(x_ref[rows, :]) * gain + mod_ref[0, 0, 0:1, :]).astype(BF16)
    width = p_ref.shape[1]
    for j in range(0, width, PROJ_COL_BLOCK):
        jb = min(PROJ_COL_BLOCK, width - j)
        p_ref[rows, j:j + jb] = _dot(h, w_ref[0, :, j:j + jb])


def _shift_rows(x, k, pos, seq_len):
    n = x.shape[0]
    if k == 0:
        return x
    rolled = pltpu.roll(x, (-k) % n, 0)
    ok = (pos >= -k) if k < 0 else (pos <= seq_len - 1 - k)
    return jnp.where(ok, rolled, 0.0)


def _dwconv4(x, w_ref, b_ref, col0, pos, seq_len):
    width = x.shape[1]
    cols = slice(col0, col0 + width)
    acc = x * w_ref[0, 1:2, cols] + b_ref[0, 0:1, cols]
    acc = acc + _shift_rows(x, -1, pos, seq_len) * w_ref[0, 0:1, cols]
    acc = acc + _shift_rows(x, 1, pos, seq_len) * w_ref[0, 2:3, cols]
    acc = acc + _shift_rows(x, 2, pos, seq_len) * w_ref[0, 3:4, cols]
    return acc


def _zero_later_layers(st_ref):
    st_ref[:, 1:] = jnp.zeros((st_ref.shape[0], DEPTH - 1) + tuple(st_ref.shape[2:]), F32)


def _state_output(nseq_total, nseq_tile, tail, l, prev, n_inputs):
    zeros = (0,) * len(tail)
    shape = jax.ShapeDtypeStruct((nseq_total, DEPTH) + tail, F32)
    if l == 0:
        return pl.BlockSpec((nseq_tile, DEPTH) + tail, lambda i: (i, 0) + zeros), shape, [], [], {}
    spec = pl.BlockSpec((nseq_tile, 1) + tail, lambda i: (i, l) + zeros)
    return spec, shape, [pl.BlockSpec(memory_space=pl.ANY)], [prev], {n_inputs: 1}


def _head_lane_expand(x, lane0):
    rows = x.shape[0]
    low = lax.broadcasted_iota(jnp.int32, (rows, HEAD_PAIR), 1) < DK_RET
    blocks = []
    for p in range(H_RET // 2):
        a = jnp.broadcast_to(x[:, lane0 + 2 * p:lane0 + 2 * p + 1], (rows, HEAD_PAIR))
        b = jnp.broadcast_to(x[:, lane0 + 2 * p + 1:lane0 + 2 * p + 2], (rows, HEAD_PAIR))
        blocks.append(jnp.where(low, a, b))
    return jnp.concatenate(blocks, axis=1)


def _seq_call(body, name, x2d, mods, row0, rows_per_mod, gain, w_all, col0, width, l, seq_len, params,
              extra_in, y_width, state_tail, prev_states, scratch):
    n = x2d.shape[0]
    tm = SEQ_TILE
    nseq_tile = tm // seq_len
    emit_state = state_tail is not None
    in_specs = [pl.BlockSpec((tm, D_MODEL), lambda i: (i, 0)), _mod_spec(row0, rows_per_mod, tm, l),
                _layer_spec(gain, l), _w_in_spec(l, col0, width)]
    in_specs += [_layer_spec(a, l) for a in params]
    in_specs += [spec for _, spec in extra_in]
    args = [x2d, mods, gain, w_all] + list(params) + [a for a, _ in extra_in]
    out_specs = [pl.BlockSpec((tm, y_width), lambda i: (i, 0))]
    out_shape = [jax.ShapeDtypeStruct((n, y_width), BF16)]
    aliases = {}
    if emit_state:
        spec, shape, extra_specs, extra_args, aliases = _state_output(
            n // seq_len, nseq_tile, state_tail, l, prev_states, len(args))
        out_specs.append(spec)
        out_shape.append(shape)
        in_specs += extra_specs
        args += extra_args
    return pl.pallas_call(
        functools.partial(body, seq_len=seq_len, emit_state=emit_state, first_layer=l == 0),
        grid=(n // tm,), in_specs=in_specs, out_specs=out_specs, out_shape=out_shape,
        input_output_aliases=aliases, scratch_shapes=scratch, compiler_params=_params(1), name=name,
    )(*args)


def _ret_body(*refs, seq_len, rope, has_s0, emit_state, first_layer):
    it = iter(refs)
    x_ref, mod_ref, g_ref, w_ref = next(it), next(it), next(it), next(it)
    dec_lane_ref = next(it)
    dec_wide_ref = next(it)
    gn_ref = next(it)
    cos_ref = sin_ref = s0_ref = st_ref = qk_ref = None
    if rope:
        cos_ref = next(it)
        sin_ref = next(it)
    if has_s0:
        s0_ref = next(it)
    if emit_state and not first_layer:
        next(it)
    y_ref = next(it)
    if emit_state:
        st_ref = next(it)
    p_ref = next(it)
    dmat_ref = next(it)
    wts_ref = next(it)
    if rope:
        qk_ref = next(it)

    C = SCAN_CHUNK
    nch = seq_len // C
    nseq = p_ref.shape[0] // seq_len
    npair = H_RET // 2
    assert not (has_s0 or rope) or nseq == 1
    if nseq == 1:
        _project(x_ref, mod_ref, g_ref, w_ref, p_ref)
    lgl = _log_sigmoid(dec_lane_ref[0])

    @pl.when(pl.program_id(0) == 0)
    def _():
        lgw = _log_sigmoid(dec_wide_ref[0])
        ii = lax.broadcasted_iota(jnp.int32, (C, C), 0)
        mm = lax.broadcasted_iota(jnp.int32, (C, C), 1)
        dif = (ii - mm).astype(F32)
        neg_inf = -jnp.inf
        for h in range(H_RET):
            df = jnp.exp(jnp.where(ii >= mm, dif * lgw[h:h + 1, :], neg_inf))
            db = jnp.exp(jnp.where(mm >= ii, (-dif) * lgw[H_RET + h:H_RET + h + 1, :], neg_inf))
            dmat_ref[h] = df + db
        ri = lax.broadcasted_iota(jnp.int32, (C, W_RET), 0).astype(F32)
        wts_ref[0] = jnp.exp((C - 1.0 - ri) * lgl[0:1, :])
        wts_ref[1] = jnp.exp(ri * lgl[1:2, :])
        wts_ref[2] = jnp.exp((ri + 1.0) * lgl[0:1, :])
        wts_ref[3] = jnp.exp((C - ri) * lgl[1:2, :])

    lane = lax.broadcasted_iota(jnp.int32, (C, HEAD_PAIR), 1)
    low = lane < DK_RET
    r128 = lax.broadcasted_iota(jnp.int32, (HEAD_PAIR, HEAD_PAIR), 0) >= DK_RET
    c128 = lax.broadcasted_iota(jnp.int32, (HEAD_PAIR, HEAD_PAIR), 1) >= DK_RET
    same_head = r128 == c128
    avg = jnp.where(same_head, 1.0 / DK_RET, 0.0).astype(BF16)

    need_states = emit_state or (has_s0 and nch > 1)
    if has_s0:
        dec_f = jnp.exp(float(C) * lgl[0:1, :])
        dec_b = jnp.exp(float(C) * lgl[1:2, :])

    if rope:
        swap_low = (lax.broadcasted_iota(jnp.int32, (seq_len, HEAD_PAIR), 1) % 32) < 16
        cos = cos_ref[...]
        sin = sin_ref[...]
        for blk in range(2 * npair):
            cols_b = slice(blk * HEAD_PAIR, (blk + 1) * HEAD_PAIR)
            x = p_ref[:, cols_b]
            swapped = jnp.where(swap_low, pltpu.roll(x, HEAD_PAIR - 16, 1), pltpu.roll(x, 16, 1))
            qk_ref[:, cols_b] = x * cos + swapped * sin

    def cols(p):
        return slice(p * HEAD_PAIR, (p + 1) * HEAD_PAIR)

    if emit_state and first_layer:
        _zero_later_layers(st_ref)

    def project_seq(s):
        _project(x_ref, mod_ref, g_ref, w_ref, p_ref, slice(s * seq_len, (s + 1) * seq_len))

    def rows(s, c):
        return slice(s * seq_len + c * C, s * seq_len + (c + 1) * C)

    def get_q(s, c, p):
        src = qk_ref if rope else p_ref
        return src[rows(s, c), p * HEAD_PAIR:(p + 1) * HEAD_PAIR]

    def get_k(s, c, p):
        src = qk_ref if rope else p_ref
        return src[rows(s, c), W_RET + p * HEAD_PAIR:W_RET + (p + 1) * HEAD_PAIR]

    def get_v(s, c, p):
        return p_ref[rows(s, c), 2 * W_RET + p * HEAD_PAIR:2 * W_RET + (p + 1) * HEAD_PAIR]

    entering = {}

    def state_stage(s):
        ds_f = [[None] * npair for _ in range(nch)]
        ds_b = [[None] * npair for _ in range(nch)]
        if need_states:
            for c in range(nch):
                for p in range(npair):
                    k = get_k(s, c, p)
                    vb = get_v(s, c, p).astype(BF16)
                    kf = jnp.transpose(k * wts_ref[0, :, cols(p)]).astype(BF16)
                    kb = jnp.transpose(k * wts_ref[1, :, cols(p)]).astype(BF16)
                    ds_f[c][p] = _dot(kf, vb)
                    ds_b[c][p] = _dot(kb, vb)

        if emit_state:
            for p in range(npair):
                for a in range(2):
                    blk = slice(a * DK_RET, (a + 1) * DK_RET)
                    st_ref[s, 0, 0, 2 * p + a] = ds_f[0][p][blk, blk]
                    st_ref[s, 0, 1, 2 * p + a] = ds_b[0][p][blk, blk]

        if has_s0:
            low_half = lax.broadcasted_iota(jnp.int32, (DK_RET, HEAD_PAIR), 1) < DK_RET

            def pair_state(d, p):
                u = s0_ref[0, 0, d, p]
                return jnp.concatenate([jnp.where(low_half, u, 0.0), jnp.where(low_half, 0.0, u)], axis=0)

            for p in range(npair):
                fwd = [None] * nch
                bwd = [None] * nch
                st = pair_state(0, p)
                for c in range(nch):
                    fwd[c] = st
                    if c + 1 < nch:
                        st = st * dec_f[:, cols(p)] + jnp.where(same_head, ds_f[c][p], 0.0)
                st = pair_state(1, p)
                for c in reversed(range(nch)):
                    bwd[c] = st
                    if c > 0:
                        st = st * dec_b[:, cols(p)] + jnp.where(same_head, ds_b[c][p], 0.0)
                for c in range(nch):
                    entering[(s, c, p)] = (fwd[c], bwd[c])

    def output_stage(chunks):
        units = [(s, c, p) for s, c in chunks for p in range(npair)]
        q = {u: get_q(*u) for u in units}
        kb = {u: get_k(*u).astype(BF16) for u in units}
        vb = {u: get_v(*u).astype(BF16) for u in units}
        scores = {}
        for u in units:
            scores[u] = (_dot_nt(jnp.where(low, q[u], 0.0).astype(BF16), kb[u]),
                         _dot_nt(jnp.where(low, 0.0, q[u]).astype(BF16), kb[u]))
        inter = {}
        if has_s0:
            for u in units:
                sf, sb = entering[u]
                inter[u] = (_dot((q[u] * wts_ref[2, :, cols(u[2])]).astype(BF16), sf.astype(BF16))
                            + _dot((q[u] * wts_ref[3, :, cols(u[2])]).astype(BF16), sb.astype(BF16)))
        o = {}
        for u in units:
            p = u[2]
            o0 = _dot((scores[u][0] * dmat_ref[2 * p]).astype(BF16), vb[u])
            o1 = _dot((scores[u][1] * dmat_ref[2 * p + 1]).astype(BF16), vb[u])
            op = jnp.where(low, o0, o1)
            o[u] = op + inter[u] if has_s0 else op
        mu = {u: _dot(o[u].astype(BF16), avg) for u in units}
        d = {u: o[u] - mu[u] for u in units}
        var = {u: _dot((d[u] * d[u]).astype(BF16), avg) for u in units}
        for u in units:
            s, c, p = u
            on = d[u] * lax.rsqrt(var[u] + EPS)
            g = p_ref[rows(s, c), 3 * W_RET + p * HEAD_PAIR:3 * W_RET + (p + 1) * HEAD_PAIR]
            y_ref[rows(s, c), cols(p)] = (on * gn_ref[0, 0:1, cols(p)] * _silu(g)).astype(BF16)

    group = 2
    chunks = [(s, c) for s in range(nseq) for c in range(nch)]
    groups = [chunks[i:i + group] for i in range(0, len(chunks), group)]
    projected = {0} if nseq == 1 else set()
    prepared = set()

    def prepare(chunk_group, with_states):
        for s in sorted({s for s, _ in chunk_group}):
            if s not in projected:
                project_seq(s)
                projected.add(s)
            if with_states and s not in prepared:
                state_stage(s)
                prepared.add(s)

    prepare(groups[0], False)
    for gi, grp in enumerate(groups):
        if gi + 1 < len(groups):
            prepare(groups[gi + 1], False)
        prepare(grp, True)
        output_stage(grp)


def _retention(x2d, mods, row0, rows_per_mod, gain, w_all, dec_lane, dec_wide, gn, seq_len, l,
               rope_tabs=None, s0=None, prev_states=None):
    rope = rope_tabs is not None
    has_s0 = s0 is not None
    npair = H_RET // 2
    extra = []
    if rope:
        extra += [(t, _const_spec(t.shape)) for t in rope_tabs]
    if has_s0:
        extra.append((s0, pl.BlockSpec((1, 1, 2, npair, DK_RET, HEAD_PAIR), lambda i: (i, l, 0, 0, 0, 0))))
    scratch = [pltpu.VMEM((SEQ_TILE, W_GRP_R), F32),
               pltpu.VMEM((H_RET, SCAN_CHUNK, SCAN_CHUNK), F32), pltpu.VMEM((4, SCAN_CHUNK, W_RET), F32)]
    if rope:
        scratch.append(pltpu.VMEM((seq_len, 2 * W_RET), F32))
    return _seq_call(functools.partial(_ret_body, rope=rope, has_s0=has_s0),
                     "retention_lat" if rope else "retention_ctx",
                     x2d, mods, row0, rows_per_mod, gain, w_all, COL_R, W_GRP_R, l, seq_len,
                     (dec_lane, dec_wide, gn), extra, W_RET,
                     None if has_s0 else (2, H_RET, DK_RET, DK_RET), prev_states, scratch)


def _ssd_body(*refs, seq_len, has_s0, emit_state, first_layer):
    it = iter(refs)
    x_ref, mod_ref, g_ref, w_ref = next(it), next(it), next(it), next(it)
    cw_ref = next(it)
    cb_ref = next(it)
    dtb_ref = next(it)
    alog_ref = next(it)
    dskip_ref = next(it)
    gn_ref = next(it)
    s0_ref = st_ref = None
    if has_s0:
        s0_ref = next(it)
    if emit_state and not first_layer:
        next(it)
    y_ref = next(it)
    if emit_state:
        st_ref = next(it)
    p_ref = next(it)
    xs_ref = next(it)
    bc_ref = next(it)

    T = seq_len
    C = SCAN_CHUNK
    nch = T // C
    n_rows = p_ref.shape[0]
    nseq = n_rows // T
    assert not has_s0 or nseq == 1
    assert T & (T - 1) == 0
    pos = lax.broadcasted_iota(jnp.int32, (T, LANES), 0)

    def project_and_conv(s):
        seq_rows = slice(s * T, (s + 1) * T)
        _project(x_ref, mod_ref, g_ref, w_ref, p_ref, seq_rows)
        for blk in range(CONV_CH // LANES):
            x = p_ref[seq_rows, W_SSD + blk * LANES:W_SSD + (blk + 1) * LANES]
            a = _silu(_dwconv4(x, cw_ref, cb_ref, blk * LANES, pos, T))
            if blk < W_SSD // LANES:
                xs_ref[seq_rows, blk * LANES:(blk + 1) * LANES] = a
            else:
                o = blk * LANES - W_SSD
                bc_ref[seq_rows, o:o + LANES] = a

    a_neg = -jnp.exp(alog_ref[0])
    ii = lax.broadcasted_iota(jnp.int32, (C, C), 0)
    mm = lax.broadcasted_iota(jnp.int32, (C, C), 1)
    lower = ii >= mm
    upper = mm >= ii
    tri_l = jnp.where(lower, 1.0, 0.0).astype(BF16)
    tri_u = jnp.where(upper, 1.0, 0.0).astype(BF16)
    neg_inf = -jnp.inf
    grp0 = lax.broadcasted_iota(jnp.int32, (C, W_SSD), 1) < (W_SSD // G_SSD)
    grp0_s = lax.broadcasted_iota(jnp.int32, (N_SSD, W_SSD), 1) < (W_SSD // G_SSD)
    grp0_t = lax.broadcasted_iota(jnp.int32, (W_SSD, N_SSD), 0) < (W_SSD // G_SSD)
    low = lax.broadcasted_iota(jnp.int32, (C, HEAD_PAIR), 1) < P_SSD

    if emit_state and first_layer:
        _zero_later_layers(st_ref)

    scalars = {}

    def chunk_scalars(s, c):
        if (s, c) not in scalars:
            rows = slice(s * T + c * C, s * T + (c + 1) * C)
            dt = _softplus(p_ref[rows, W_SSD + CONV_CH:W_SSD + CONV_CH + LANES] + dtb_ref[0])
            la = dt * a_neg
            a1 = la.astype(BF16)
            r1 = la - a1.astype(F32)
            a2 = r1.astype(BF16)
            a3 = (r1 - a2.astype(F32)).astype(BF16)
            pre = _dot(tri_l, a1) + _dot(tri_l, a2) + _dot(tri_l, a3)
            suf = _dot(tri_u, a1) + _dot(tri_u, a2) + _dot(tri_u, a3)
            scalars[(s, c)] = (dt, pre, suf)
        return scalars[(s, c)]

    def light_stage(s):
        def rows(c):
            return slice(s * T + c * C, s * T + (c + 1) * C)

        def state_weights(c):
            dt, pre, suf = chunk_scalars(s, c)
            wf = jnp.exp(pre[C - 1:C, :] - pre) * dt
            wb = jnp.exp(suf[0:1, :] - suf) * dt
            xs = xs_ref[rows(c), :]
            return (xs * _head_lane_expand(wf, 0), xs * _head_lane_expand(wb, DT_BWD_LANE), pre, suf)

        if emit_state:
            xf, xb, _, _ = state_weights(0)
            bm = [bc_ref[rows(0), g * N_SSD:(g + 1) * N_SSD].astype(BF16) for g in range(G_SSD)]
            for d, xw in enumerate((xf, xb)):
                xt = jnp.transpose(xw).astype(BF16)
                st = jnp.where(grp0_t, _dot(xt, bm[0]), _dot(xt, bm[1]))
                for h in range(H_SSD):
                    st_ref[s, 0, d, h] = st[h * P_SSD:(h + 1) * P_SSD, :]

        ds_f = [None] * nch
        ds_b = [None] * nch
        dec_f = [None] * nch
        dec_b = [None] * nch
        if has_s0 and nch > 1:
            for c in range(nch):
                xf, xb, pre, suf = state_weights(c)
                bt0 = jnp.transpose(bc_ref[rows(c), 0:N_SSD]).astype(BF16)
                bt1 = jnp.transpose(bc_ref[rows(c), N_SSD:2 * N_SSD]).astype(BF16)
                ds_f[c] = jnp.where(grp0_s, _dot(bt0, xf.astype(BF16)), _dot(bt1, xf.astype(BF16)))
                ds_b[c] = jnp.where(grp0_s, _dot(bt0, xb.astype(BF16)), _dot(bt1, xb.astype(BF16)))
                dec_f[c] = _head_lane_expand(jnp.exp(pre[C - 1:C, :]), 0)
                dec_b[c] = _head_lane_expand(jnp.exp(suf[0:1, :]), DT_BWD_LANE)

        sf_in = [None] * nch
        sb_in = [None] * nch
        if has_s0:
            st = s0_ref[0, 0, 0]
            for c in range(nch):
                sf_in[c] = st
                if c + 1 < nch:
                    st = st * dec_f[c] + ds_f[c]
            st = s0_ref[0, 0, 1]
            for c in reversed(range(nch)):
                sb_in[c] = st
                if c > 0:
                    st = st * dec_b[c] + ds_b[c]

        staged = []
        for c in range(nch):
            dt, pre, suf = chunk_scalars(s, c)
            cm = [bc_ref[rows(c), 2 * N_SSD + g * N_SSD:2 * N_SSD + (g + 1) * N_SSD].astype(BF16)
                  for g in range(G_SSD)]
            bm = [bc_ref[rows(c), g * N_SSD:(g + 1) * N_SSD].astype(BF16) for g in range(G_SSD)]
            gram = [_dot_nt(cm[g], bm[g]) for g in range(G_SSD)]
            inter = None
            if has_s0:
                sf = sf_in[c].astype(BF16)
                sb = sb_in[c].astype(BF16)
                inter = (jnp.where(grp0, _dot(cm[0], sf), _dot(cm[1], sf)),
                         jnp.where(grp0, _dot(cm[0], sb), _dot(cm[1], sb)))
            log_dt = jnp.log(dt)
            staged.append((pre, suf, jnp.transpose(pre - log_dt), jnp.transpose(suf - log_dt), gram, inter))
        return staged

    def heavy_stage(s, staged):
        for c in range(nch):
            rows_c = slice(s * T + c * C, s * T + (c + 1) * C)
            pre, suf, pre_t, suf_t, gram, inter = staged[c]
            xs = xs_ref[rows_c, :]
            xs_b = xs.astype(BF16)
            heads = []
            for h in range(H_SSD):
                g = h // (H_SSD // G_SSD)
                hb = DT_BWD_LANE + h
                df = jnp.exp(jnp.where(lower, pre[:, h:h + 1] - pre_t[h:h + 1, :], neg_inf))
                db = jnp.exp(jnp.where(upper, suf[:, hb:hb + 1] - suf_t[hb:hb + 1, :], neg_inf))
                w = gram[g] * (df + db)
                p = h // 2
                heads.append(_dot(w.astype(BF16), xs_b[:, p * HEAD_PAIR:(p + 1) * HEAD_PAIR]))
            o = jnp.concatenate([jnp.where(low, heads[2 * p], heads[2 * p + 1]) for p in range(H_SSD // 2)],
                                axis=1)
            if has_s0:
                o = o + _head_lane_expand(jnp.exp(pre), 0) * inter[0]
                o = o + _head_lane_expand(jnp.exp(suf), DT_BWD_LANE) * inter[1]
            y = o + dskip_ref[0] * xs
            yz = y * _silu(p_ref[rows_c, 0:W_SSD])
            y_ref[rows_c, :] = (_rms_scale(yz) * gn_ref[0]).astype(BF16)

    project_and_conv(0)
    staged = light_stage(0)
    for s in range(nseq):
        staged_next = None
        if s + 1 < nseq:
            project_and_conv(s + 1)
            staged_next = light_stage(s + 1)
        heavy_stage(s, staged)
        staged = staged_next


def _ssd(x2d, mods, row0, rows_per_mod, gain, w_all, cw, cb, dtb, alog, dskip, gn, seq_len, l,
         s0=None, prev_states=None):
    has_s0 = s0 is not None
    extra = []
    if has_s0:
        extra.append((s0, pl.BlockSpec((1, 1, 2, N_SSD, W_SSD), lambda i: (i, l, 0, 0, 0))))
    scratch = [pltpu.VMEM((SEQ_TILE, W_GRP_S), F32), pltpu.VMEM((SEQ_TILE, W_SSD), F32),
               pltpu.VMEM((SEQ_TILE, 2 * G_SSD * N_SSD), F32)]
    return _seq_call(functools.partial(_ssd_body, has_s0=has_s0), "ssd_lat" if has_s0 else "ssd_ctx",
                     x2d, mods, row0, rows_per_mod, gain, w_all, COL_S, W_SLOT_S, l, seq_len,
                     (cw, cb, dtb, alog, dskip, gn), extra, W_SSD,
                     None if has_s0 else (2, H_SSD, P_SSD, N_SSD), prev_states, scratch)


def _lru_body(*refs, seq_len, has_h0, emit_state, first_layer):
    it = iter(refs)
    x_ref, mod_ref, g_ref, w_ref = next(it), next(it), next(it), next(it)
    cw_ref = next(it)
    cb_ref = next(it)
    wg_ref = next(it)
    bg_ref = next(it)
    lam_ref = next(it)
    h0_ref = st_ref = None
    if has_h0:
        h0_ref = next(it)
    if emit_state and not first_layer:
        next(it)
    y_ref = next(it)
    if emit_state:
        st_ref = next(it)
    p_ref = next(it)

    T = seq_len
    n_rows = p_ref.shape[0]
    nseq = n_rows // T
    assert not has_h0 or nseq == 1
    assert T & (T - 1) == 0
    _project(x_ref, mod_ref, g_ref, w_ref, p_ref)
    row = lax.broadcasted_iota(jnp.int32, (n_rows, W_LRU), 0)
    pos = row & (T - 1)
    sub = row & (SUBLANES - 1)
    xc = _dwconv4(p_ref[:, 0:W_LRU], cw_ref, cb_ref, 0, pos, T)
    gates = _sigmoid(_dot(xc.astype(BF16), wg_ref[0]) + bg_ref[0])
    decay_rate = _softplus(-lam_ref[0])
    if emit_state and first_layer:
        _zero_later_layers(st_ref)
    nblk = T // SUBLANES
    total = None
    for d in range(2):
        r = gates[:, 2 * d * W_LRU:(2 * d + 1) * W_LRU]
        i = gates[:, (2 * d + 1) * W_LRU:(2 * d + 2) * W_LRU]
        log_a = (-LRU_C * r) * decay_rate[d:d + 1, :]
        a = jnp.exp(log_a)
        th = jnp.tanh(log_a)
        b = jnp.sqrt(-2.0 * th / (1.0 - th)) * (i * xc)
        step = 1
        while step < SUBLANES:
            if d == 0:
                ok = sub >= step
                shift = step
            else:
                ok = sub <= SUBLANES - 1 - step
                shift = n_rows - step
            a_prev = jnp.where(ok, pltpu.roll(a, shift, 0), 1.0)
            b_prev = jnp.where(ok, pltpu.roll(b, shift, 0), 0.0)
            b = a * b_prev + b
            a = a * a_prev
            step *= 2
        edge = SUBLANES - 1 if d == 0 else 0
        tiles = [None] * (nseq * nblk)
        for s in range(nseq):
            carry = h0_ref[0, 0, d:d + 1, :] if has_h0 else None
            for v in (range(nblk) if d == 0 else reversed(range(nblk))):
                t = s * nblk + v
                rows = slice(t * SUBLANES, (t + 1) * SUBLANES)
                hv = b[rows]
                if carry is not None:
                    hv = hv + a[rows] * carry
                tiles[t] = hv
                carry = hv[edge:edge + 1, :]
            if emit_state:
                st_ref[s, 0, d:d + 1, :] = carry
        hs = jnp.concatenate(tiles, axis=0)
        total = hs if total is None else total + hs
    y_ref[...] = (total * _gelu_tanh(p_ref[:, W_LRU:2 * W_LRU])).astype(BF16)


def _lru(x2d, mods, row0, rows_per_mod, gain, w_all, cw, cb, wg, bg, lam, seq_len, l, h0=None, prev_states=None):
    has_h0 = h0 is not None
    extra = []
    if has_h0:
        extra.append((h0, pl.BlockSpec((1, 1, 2, W_LRU), lambda i: (i, l, 0, 0))))
    scratch = [pltpu.VMEM((SEQ_TILE, W_GRP_L), F32)]
    return _seq_call(functools.partial(_lru_body, has_h0=has_h0), "lru_lat" if has_h0 else "lru_ctx",
                     x2d, mods, row0, rows_per_mod, gain, w_all, COL_L, W_GRP_L, l, seq_len,
                     (cw, cb, wg, bg, lam), extra, W_LRU,
                     None if has_h0 else (2, W_LRU), prev_states, scratch)


def _outproj_body(x_ref, yr_ref, ys_ref, yl_ref, mod_ref, g_ref, wo_ref, x1_ref, h2_ref):
    mix = _dot(yr_ref[...], wo_ref[0, 0:W_RET, :])
    mix = mix + _dot(ys_ref[...], wo_ref[0, W_RET:W_RET + W_SSD, :])
    mix = mix + _dot(yl_ref[...], wo_ref[0, W_RET + W_SSD:D_MODEL, :])
    x1 = x_ref[...] + mod_ref[0, 0, 2:3, :] * mix
    x1_ref[...] = x1
    y = _rms_scale(x1) * g_ref[0]
    h2_ref[...] = (y * (1.0 + mod_ref[0, 0, 4:5, :]) + mod_ref[0, 0, 3:4, :]).astype(BF16)


def _out_proj(x2d, y_ret, y_ssd, y_lru, mods, row0, rows_per_mod, gain, wo, l):
    n = x2d.shape[0]
    tm = ROW_TILE
    row_spec = lambda w: pl.BlockSpec((tm, w), lambda i: (i, 0))
    return pl.pallas_call(
        _outproj_body,
        grid=(n // tm,),
        in_specs=[row_spec(D_MODEL), row_spec(W_RET), row_spec(W_SSD), row_spec(W_LRU),
                  _mod_spec(row0, rows_per_mod, tm, l), _layer_spec(gain, l), _layer_spec(wo, l)],
        out_specs=[row_spec(D_MODEL), row_spec(D_MODEL)],
        out_shape=[jax.ShapeDtypeStruct((n, D_MODEL), F32), jax.ShapeDtypeStruct((n, D_MODEL), BF16)],
        compiler_params=_params(1),
        name="out_proj",
    )(x2d, y_ret, y_ssd, y_lru, mods, gain, wo)


def _ffn_core(h, wup_ref, cw_ref, cb_ref, wd_ref, act_ref, seq_len):
    tm = h.shape[0]
    nseq = tm // seq_len
    row8 = lax.broadcasted_iota(jnp.int32, (SUBLANES, FF_BLOCK), 0)
    first_row = row8 == 0
    last_row = row8 == SUBLANES - 1

    def conv3(u, col0):
        cols = slice(col0, col0 + FF_BLOCK)
        w_prev, w_mid, w_next = (cw_ref[0, t:t + 1, cols] for t in range(3))
        bias = cb_ref[0, 0:1, cols]
        outs = []
        for s in range(nseq):
            us = u[s * seq_len:(s + 1) * seq_len]
            prev = pltpu.roll(us, 1, 0)
            nxt = pltpu.roll(us, seq_len - 1, 0)
            prev = jnp.concatenate([jnp.where(first_row, 0.0, prev[:SUBLANES]), prev[SUBLANES:]], axis=0)
            nxt = jnp.concatenate([nxt[:seq_len - SUBLANES], jnp.where(last_row, 0.0, nxt[seq_len - SUBLANES:])],
                                  axis=0)
            outs.append(us * w_mid + bias + prev * w_prev + nxt * w_next)
        return outs[0] if nseq == 1 else jnp.concatenate(outs, axis=0)

    def up(j):
        return (_dot(h, wup_ref[0, :, j * FF_BLOCK:(j + 1) * FF_BLOCK]),
                _dot(h, wup_ref[0, :, D_FF + j * FF_BLOCK:D_FF + (j + 1) * FF_BLOCK]))

    for j in range(N_FF_BLOCKS):
        uv, ug = up(j)
        val = conv3(uv, j * FF_BLOCK)
        gate = conv3(ug, D_FF + j * FF_BLOCK)
        act_ref[:, j * FF_BLOCK:(j + 1) * FF_BLOCK] = (_silu(gate) * val).astype(BF16)
    return _dot(act_ref[...], wd_ref[0])


def _ffn_body(h_ref, x1_ref, mod_ref, wup_ref, cw_ref, cb_ref, wd_ref, fg_ref, o_ref, act_ref, *, seq_len,
              final_norm):
    down = _ffn_core(h_ref[...], wup_ref, cw_ref, cb_ref, wd_ref, act_ref, seq_len)
    out = x1_ref[...] + mod_ref[0, 0, 5:6, :] * down
    if final_norm:
        out = _rms_scale(out) * fg_ref[...]
    o_ref[...] = out


def _mix_ffn_body(x_ref, yr_ref, ys_ref, yl_ref, mod_ref, g_ref, wo_ref, wup_ref, cw_ref, cb_ref, wd_ref, fg_ref,
                  o_ref, act_ref, h_ref, *, seq_len, final_norm):
    mix = _dot(yr_ref[...], wo_ref[0, 0:W_RET, :])
    mix = mix + _dot(ys_ref[...], wo_ref[0, W_RET:W_RET + W_SSD, :])
    mix = mix + _dot(yl_ref[...], wo_ref[0, W_RET + W_SSD:D_MODEL, :])
    x1 = x_ref[...] + mod_ref[0, 0, 2:3, :] * mix
    o_ref[...] = x1
    y = _rms_scale(x1) * g_ref[0]
    h_ref[...] = (y * (1.0 + mod_ref[0, 0, 4:5, :]) + mod_ref[0, 0, 3:4, :]).astype(BF16)
    down = _ffn_core(h_ref[...], wup_ref, cw_ref, cb_ref, wd_ref, act_ref, seq_len)
    out = o_ref[...] + mod_ref[0, 0, 5:6, :] * down
    if final_norm:
        out = _rms_scale(out) * fg_ref[...]
    o_ref[...] = out


def _mix_ffn(x2d, y_ret, y_ssd, y_lru, mods, row0, rows_per_mod, gain, wo, wup, cw, cb, wd, fgain, seq_len,
             final_norm, l):
    n = x2d.shape[0]
    tm = ROW_TILE
    assert tm % seq_len == 0
    row_spec = lambda w: pl.BlockSpec((tm, w), lambda i: (i, 0))
    return pl.pallas_call(
        functools.partial(_mix_ffn_body, seq_len=seq_len, final_norm=final_norm),
        grid=(n // tm,),
        in_specs=[row_spec(D_MODEL), row_spec(W_RET), row_spec(W_SSD), row_spec(W_LRU),
                  _mod_spec(row0, rows_per_mod, tm, l), _layer_spec(gain, l), _layer_spec(wo, l),
                  _layer_spec(wup, l), _layer_spec(cw, l), _layer_spec(cb, l), _layer_spec(wd, l),
                  _const_spec(fgain.shape)],
        out_specs=row_spec(D_MODEL),
        out_shape=jax.ShapeDtypeStruct((n, D_MODEL), F32),
        scratch_shapes=[pltpu.VMEM((tm, D_FF), BF16), pltpu.VMEM((tm, D_MODEL), BF16)],
        compiler_params=_params(1),
        name="mix_ffn",
    )(x2d, y_ret, y_ssd, y_lru, mods, gain, wo, wup, cw, cb, wd, fgain)


def _ffn(h2, x1, mods, row0, rows_per_mod, wup, cw, cb, wd, fgain, seq_len, final_norm, l):
    n = h2.shape[0]
    tm = SEQ_TILE
    row_spec = pl.BlockSpec((tm, D_MODEL), lambda i: (i, 0))
    return pl.pallas_call(
        functools.partial(_ffn_body, seq_len=seq_len, final_norm=final_norm),
        grid=(n // tm,),
        in_specs=[row_spec, row_spec, _mod_spec(row0, rows_per_mod, tm, l),
                  _layer_spec(wup, l), _layer_spec(cw, l), _layer_spec(cb, l), _layer_spec(wd, l),
                  _const_spec(fgain.shape)],
        out_specs=row_spec,
        out_shape=jax.ShapeDtypeStruct((n, D_MODEL), F32),
        scratch_shapes=[pltpu.VMEM((tm, D_FF), BF16)],
        compiler_params=_params(1),
        name="ffn",
    )(h2, x1, mods, wup, cw, cb, wd, fgain)


def _rope_tables(t_len):
    tok = jnp.arange(t_len)
    row_pos = (tok // GRID_W).astype(F32)
    col_pos = (tok % GRID_W).astype(F32)
    lane = jnp.arange(HEAD_PAIR)
    d = lane % DK_RET
    use_row = d < (DK_RET // 2)
    half = DK_RET // 4
    freqs = ROPE_BASE ** (-(d % half).astype(F32) / half)
    ang = jnp.where(use_row[None, :], row_pos[:, None], col_pos[:, None]) * freqs[None, :]
    first = (d % (2 * half)) < half
    return jnp.cos(ang), jnp.where(first[None, :], -jnp.sin(ang), jnp.sin(ang))


def _prep_params(w_in, ret_decay, ret_norm_g, ssd_conv_b, ssd_dt_bias, ssd_a_log, ssd_d, ssd_norm_g,
                 lru_conv_b, lru_w_a, lru_b_a, lru_w_x, lru_b_x, w_out, ffn_w_up, ffn_conv_b, ffn_w_down):
    w_all = _regroup_w_in(w_in)

    def lane_place(pair):
        z = jnp.zeros((DEPTH, LANES), F32)
        return jnp.concatenate([pair[:, 0], z[:, :DT_BWD_LANE - H_SSD], pair[:, 1],
                                z[:, :LANES - DT_BWD_LANE - H_SSD]], axis=-1)[:, None, :]

    eye = jnp.eye(LRU_BLOCKS, dtype=F32)
    gates = jnp.stack([lru_w_a, lru_w_x], axis=2)
    wgate = jnp.einsum("lxgncd,nm->lncxgmd", gates, eye).reshape(DEPTH, W_LRU, 4 * W_LRU).astype(BF16)
    bgate = jnp.stack([lru_b_a, lru_b_x], axis=2).reshape(DEPTH, 1, 4 * W_LRU)
    return dict(
        w_all=w_all,
        dec_lane=jnp.repeat(ret_decay, DK_RET, axis=-1),
        dec_wide=jnp.broadcast_to(ret_decay.reshape(DEPTH, 2 * H_RET, 1), (DEPTH, 2 * H_RET, SCAN_CHUNK)),
        ret_gn=ret_norm_g[:, None, :],
        ssd_cb=ssd_conv_b[:, None, :],
        ssd_dtb=lane_place(ssd_dt_bias), ssd_alog=lane_place(ssd_a_log),
        ssd_dskip=jnp.repeat(ssd_d, P_SSD, axis=-1)[:, None, :], ssd_gn=ssd_norm_g[:, None, :],
        lru_cb=lru_conv_b[:, None, :], lru_wg=wgate, lru_bg=bgate,
        wo=w_out.astype(BF16), wup=ffn_w_up.astype(BF16), ffn_cb=ffn_conv_b[:, None, :],
        wd=ffn_w_down.astype(BF16),
    )


def kernel(x_prompt, x_sample, c, c_ctx, state_ret, state_ssd, state_lru, w_ada, b_ada, norm1_g, norm2_g, w_in, ret_decay, ret_norm_g, ssd_conv_w, ssd_conv_b, ssd_dt_bias, ssd_a_log, ssd_d, ssd_norm_g, lru_conv_w, lru_conv_b, lru_w_a, lru_b_a, lru_w_x, lru_b_x, lru_lambda, w_out, ffn_w_up, ffn_conv_w, ffn_conv_b, ffn_w_down, final_norm_g):
    bp, tp, _ = x_prompt.shape
    bs, ts, _ = x_sample.shape
    assert tp == SCAN_CHUNK and ts == SEQ_TILE and ts % GRID_W == 0 and (bp * tp) % SEQ_TILE == 0

    n_req = 8
    cvec = jnp.concatenate([c_ctx[None, :], c, jnp.zeros((n_req - 1 - bs, D_MODEL), F32)], axis=0)
    mods = _modulation(cvec, w_ada, b_ada).reshape(DEPTH, n_req, 6, D_MODEL)

    pp = _prep_params(w_in, ret_decay, ret_norm_g, ssd_conv_b, ssd_dt_bias, ssd_a_log, ssd_d, ssd_norm_g,
                      lru_conv_b, lru_w_a, lru_b_a, lru_w_x, lru_b_x, w_out, ffn_w_up, ffn_conv_b, ffn_w_down)
    n1 = norm1_g[:, None, :]
    n2 = norm2_g[:, None, :]
    rope_tabs = _rope_tables(ts)
    fgain = final_norm_g[None, :]

    npair = H_RET // 2
    s0_ret = state_ret.reshape(bs, DEPTH, 2, npair, 2, DK_RET, DK_RET).transpose(0, 1, 2, 3, 5, 4, 6).reshape(
        bs, DEPTH, 2, npair, DK_RET, HEAD_PAIR)
    s0_ssd = state_ssd.transpose(0, 1, 2, 4, 3, 5).reshape(bs, DEPTH, 2, N_SSD, W_SSD)

    def trunk_layer(x2d, l, seq_len, row0, rows_per_mod, latent, prev=(None, None, None)):
        common = (x2d, mods, row0, rows_per_mod, n1, pp["w_all"])
        ret = _retention(*common, pp["dec_lane"], pp["dec_wide"], pp["ret_gn"], seq_len, l,
                         rope_tabs if latent else None, s0_ret if latent else None, prev[0])
        ssd = _ssd(*common, ssd_conv_w, pp["ssd_cb"], pp["ssd_dtb"], pp["ssd_alog"], pp["ssd_dskip"],
                   pp["ssd_gn"], seq_len, l, s0_ssd if latent else None, prev[1])
        lru = _lru(*common, lru_conv_w, pp["lru_cb"], pp["lru_wg"], pp["lru_bg"], lru_lambda, seq_len, l,
                   state_lru if latent else None, prev[2])
        ffn_args = (pp["wup"], ffn_conv_w, pp["ffn_cb"], pp["wd"], fgain, seq_len, l == DEPTH - 1, l)
        if ROW_TILE % seq_len == 0:
            out = _mix_ffn(x2d, ret[0], ssd[0], lru[0], mods, row0, rows_per_mod, n2, pp["wo"], *ffn_args)
        else:
            x1, h2 = _out_proj(x2d, ret[0], ssd[0], lru[0], mods, row0, rows_per_mod, n2, pp["wo"], l)
            out = _ffn(h2, x1, mods, row0, rows_per_mod, *ffn_args)
        states = None if latent else (ret[1], ssd[1], lru[1])
        return out, states

    xp = x_prompt.reshape(bp * tp, D_MODEL)
    xs = x_sample.reshape(bs * ts, D_MODEL)
    states = (None, None, None)
    for l in range(DEPTH):
        xp, states = trunk_layer(xp, l, tp, 0, bp * tp, False, states)
        xs, _ = trunk_layer(xs, l, ts, 1, ts, True)

    return (xp.reshape(bp, tp, D_MODEL), xs.reshape(bs, ts, D_MODEL), states[0],
            jnp.swapaxes(states[1], -1, -2), states[2])
```

```python
import functools
import math

import jax
import jax.numpy as jnp
from jax import lax
from jax.experimental import pallas as pl
from jax.experimental.pallas import tpu as pltpu

F32 = jnp.float32
BF16 = jnp.bfloat16

D_MODEL = 1024
DEPTH = 2
GRID_W = 64
W_RET = 384
H_RET = 6
DK_RET = 64
W_SSD = 384
P_SSD = 64
H_SSD = 6
N_SSD = 128
G_SSD = 2
CONV_CH = W_SSD + 2 * G_SSD * N_SSD
W_LRU = 256
LRU_BLOCKS = 4
LRU_BW = 64
LRU_C = 8.0
D_FF = 2816
ROPE_BASE = 10000.0
EPS = 1e-6

LANES = 128
SUBLANES = 8
HEAD_PAIR = LANES
SCAN_CHUNK = 256
FF_BLOCK = 256
N_FF_BLOCKS = D_FF // FF_BLOCK
ROW_TILE = 512
SEQ_TILE = 1024
PROJ_COL_BLOCK = 512
VMEM_LIMIT = 56 * 1024 * 1024

W_GRP_R = 4 * W_RET
W_GRP_S = W_SSD + CONV_CH + LANES
W_GRP_L = 2 * W_LRU
W_SLOT_S = W_GRP_R
COL_R, COL_S, COL_L = 0, W_GRP_R, W_GRP_R + W_SLOT_S
W_IN_PAD = COL_L + W_GRP_L
DT_BWD_LANE = 8


def _dot(a, b):
    return jnp.dot(a, b, preferred_element_type=F32)


def _dot_nt(a, b):
    return lax.dot_general(a, b, (((1,), (1,)), ((), ())), preferred_element_type=F32)


def _sigmoid(x):
    return 1.0 / (1.0 + jnp.exp(-x))


def _silu(x):
    return x * _sigmoid(x)


def _softplus(x):
    return jnp.maximum(x, 0.0) + jnp.log1p(jnp.exp(-jnp.abs(x)))


def _log_sigmoid(x):
    return -_softplus(-x)


def _gelu_tanh(x):
    c = math.sqrt(2.0 / math.pi)
    return 0.5 * x * (1.0 + jnp.tanh(c * (x + 0.044715 * (x * x * x))))


def _rms_scale(x):
    return x * lax.rsqrt(jnp.mean(x * x, axis=-1, keepdims=True) + EPS)


def _const_spec(shape):
    zeros = (0,) * len(shape)
    return pl.BlockSpec(shape, lambda *_: zeros, pipeline_mode=pl.Buffered(1))


def _layer_spec(arr, l):
    rest = (0,) * (arr.ndim - 1)
    return pl.BlockSpec((1,) + arr.shape[1:], lambda *_: (l,) + rest, pipeline_mode=pl.Buffered(1))


def _w_in_spec(l, col0, width):
    assert col0 % width == 0
    return pl.BlockSpec((1, D_MODEL, width), lambda *_: (l, 0, col0 // width), pipeline_mode=pl.Buffered(1))


def _mod_spec(row0, rows_per_mod, tm, l):
    return pl.BlockSpec((1, 1, 6, D_MODEL), lambda i: (l, row0 + i * tm // rows_per_mod, 0, 0))


def _params(n_axes):
    return pltpu.CompilerParams(dimension_semantics=("arbitrary",) * n_axes,
                                vmem_limit_bytes=VMEM_LIMIT)


def _mod_body(c_ref, w_ref, b_ref, o_ref):
    s = _silu(c_ref[...]).astype(BF16)
    o_ref[0] = _dot(s, w_ref[0].astype(BF16)) + b_ref[0]


def _modulation(cvec, w_ada, b_ada):
    rows = cvec.shape[0]
    nblk = (6 * D_MODEL) // D_MODEL
    return pl.pallas_call(
        _mod_body,
        grid=(DEPTH, nblk),
        in_specs=[pl.BlockSpec((rows, D_MODEL), lambda l, j: (0, 0)),
                  pl.BlockSpec((1, D_MODEL, D_MODEL), lambda l, j: (l, 0, j)),
                  pl.BlockSpec((1, 1, D_MODEL), lambda l, j: (l, 0, j))],
        out_specs=pl.BlockSpec((1, rows, D_MODEL), lambda l, j: (l, 0, j)),
        out_shape=jax.ShapeDtypeStruct((DEPTH, rows, 6 * D_MODEL), F32),
        compiler_params=_params(2),
        name="adaln_mod",
    )(cvec, w_ada, b_ada.reshape(DEPTH, 1, 6 * D_MODEL))


IN_DIM = 4 * W_RET + W_SSD + CONV_CH + H_SSD + 2 * W_LRU
REGROUP_ROWS = 256


def _regroup_body(w_ref, o_ref):
    rows = w_ref.shape[1]
    o_dt = COL_S + W_SSD + CONV_CH
    o_ref[0, :, 0:W_RET] = (w_ref[0, :, 0:W_RET] * (DK_RET ** -0.5)).astype(BF16)
    o_ref[0, :, W_RET:o_dt] = w_ref[0, :, W_RET:o_dt].astype(BF16)
    tail = w_ref[0, :, o_dt:IN_DIM]
    first = tail[:, 0:LANES]
    lane = lax.broadcasted_iota(jnp.int32, (rows, LANES), 1)
    dt_fwd = jnp.where(lane < H_SSD, first, 0.0)
    dt_bwd = jnp.where((lane >= DT_BWD_LANE) & (lane < DT_BWD_LANE + H_SSD), pltpu.roll(first, DT_BWD_LANE, 1), 0.0)
    o_ref[0, :, o_dt:o_dt + LANES] = (dt_fwd + dt_bwd).astype(BF16)
    o_ref[0, :, o_dt + LANES:COL_L] = jnp.zeros((rows, COL_L - o_dt - LANES), BF16)
    o_ref[0, :, COL_L:W_IN_PAD] = tail[:, H_SSD:H_SSD + W_GRP_L].astype(BF16)


def _regroup_w_in(w_in):
    rb = REGROUP_ROWS
    return pl.pallas_call(
        _regroup_body,
        grid=(DEPTH, D_MODEL // rb),
        in_specs=[pl.BlockSpec((1, rb, IN_DIM), lambda l, i: (l, i, 0))],
        out_specs=pl.BlockSpec((1, rb, W_IN_PAD), lambda l, i: (l, i, 0)),
        out_shape=jax.ShapeDtypeStruct((DEPTH, D_MODEL, W_IN_PAD), BF16),
        compiler_params=_params(2),
        name="w_in_regroup",
    )(w_in)


def _project(x_ref, mod_ref, g_ref, w_ref, p_ref, rows=slice(None)):
    gain = g_ref[0] * (1.0 + mod_ref[0, 0, 1:2, :])
    h = (_rms_scale(x_ref[rows, :]) * gain + mod_ref[0, 0, 0:1, :]).astype(BF16)
    width = p_ref.shape[1]
    for j in range(0, width, PROJ_COL_BLOCK):
        jb = min(PROJ_COL_BLOCK, width - j)
        p_ref[rows, j:j + jb] = _dot(h, w_ref[0, :, j:j + jb])


def _shift_rows(x, k, pos, seq_len):
    n = x.shape[0]
    if k == 0:
        return x
    rolled = pltpu.roll(x, (-k) % n, 0)
    ok = (pos >= -k) if k < 0 else (pos <= seq_len - 1 - k)
    return jnp.where(ok, rolled, 0.0)


def _dwconv4(x, w_ref, b_ref, col0, pos, seq_len):
    width = x.shape[1]
    cols = slice(col0, col0 + width)
    acc = x * w_ref[0, 1:2, cols] + b_ref[0, 0:1, cols]
    acc = acc + _shift_rows(x, -1, pos, seq_len) * w_ref[0, 0:1, cols]
    acc = acc + _shift_rows(x, 1, pos, seq_len) * w_ref[0, 2:3, cols]
    acc = acc + _shift_rows(x, 2, pos, seq_len) * w_ref[0, 3:4, cols]
    return acc


def _zero_later_layers(st_ref):
    st_ref[:, 1:] = jnp.zeros((st_ref.shape[0], DEPTH - 1) + tuple(st_ref.shape[2:]), F32)


def _state_output(nseq_total, nseq_tile, tail, l, prev, n_inputs):
    zeros = (0,) * len(tail)
    shape = jax.ShapeDtypeStruct((nseq_total, DEPTH) + tail, F32)
    if l == 0:
        return pl.BlockSpec((nseq_tile, DEPTH) + tail, lambda i: (i, 0) + zeros), shape, [], [], {}
    spec = pl.BlockSpec((nseq_tile, 1) + tail, lambda i: (i, l) + zeros)
    return spec, shape, [pl.BlockSpec(memory_space=pl.ANY)], [prev], {n_inputs: 1}


def _head_lane_expand(x, lane0):
    rows = x.shape[0]
    low = lax.broadcasted_iota(jnp.int32, (rows, HEAD_PAIR), 1) < DK_RET
    blocks = []
    for p in range(H_RET // 2):
        a = jnp.broadcast_to(x[:, lane0 + 2 * p:lane0 + 2 * p + 1], (rows, HEAD_PAIR))
        b = jnp.broadcast_to(x[:, lane0 + 2 * p + 1:lane0 + 2 * p + 2], (rows, HEAD_PAIR))
        blocks.append(jnp.where(low, a, b))
    return jnp.concatenate(blocks, axis=1)


def _seq_call(body, name, x2d, mods, row0, rows_per_mod, gain, w_all, col0, width, l, seq_len, params,
              extra_in, y_width, state_tail, prev_states, scratch):
    n = x2d.shape[0]
    tm = SEQ_TILE
    nseq_tile = tm // seq_len
    emit_state = state_tail is not None
    in_specs = [pl.BlockSpec((tm, D_MODEL), lambda i: (i, 0)), _mod_spec(row0, rows_per_mod, tm, l),
                _layer_spec(gain, l), _w_in_spec(l, col0, width)]
    in_specs += [_layer_spec(a, l) for a in params]
    in_specs += [spec for _, spec in extra_in]
    args = [x2d, mods, gain, w_all] + list(params) + [a for a, _ in extra_in]
    out_specs = [pl.BlockSpec((tm, y_width), lambda i: (i, 0))]
    out_shape = [jax.ShapeDtypeStruct((n, y_width), BF16)]
    aliases = {}
    if emit_state:
        spec, shape, extra_specs, extra_args, aliases = _state_output(
            n // seq_len, nseq_tile, state_tail, l, prev_states, len(args))
        out_specs.append(spec)
        out_shape.append(shape)
        in_specs += extra_specs
        args += extra_args
    return pl.pallas_call(
        functools.partial(body, seq_len=seq_len, emit_state=emit_state, first_layer=l == 0),
        grid=(n // tm,), in_specs=in_specs, out_specs=out_specs, out_shape=out_shape,
        input_output_aliases=aliases, scratch_shapes=scratch, compiler_params=_params(1), name=name,
    )(*args)


def _ret_body(*refs, seq_len, rope, has_s0, emit_state, first_layer):
    it = iter(refs)
    x_ref, mod_ref, g_ref, w_ref = next(it), next(it), next(it), next(it)
    dec_lane_ref = next(it)
    dec_wide_ref = next(it)
    gn_ref = next(it)
    cos_ref = sin_ref = s0_ref = st_ref = qk_ref = None
    if rope:
        cos_ref = next(it)
        sin_ref = next(it)
    if has_s0:
        s0_ref = next(it)
    if emit_state and not first_layer:
        next(it)
    y_ref = next(it)
    if emit_state:
        st_ref = next(it)
    p_ref = next(it)
    dmat_ref = next(it)
    wts_ref = next(it)
    if rope:
        qk_ref = next(it)

    C = SCAN_CHUNK
    nch = seq_len // C
    nseq = p_ref.shape[0] // seq_len
    npair = H_RET // 2
    assert not (has_s0 or rope) or nseq == 1
    if nseq == 1:
        _project(x_ref, mod_ref, g_ref, w_ref, p_ref)
    lgl = _log_sigmoid(dec_lane_ref[0])

    @pl.when(pl.program_id(0) == 0)
    def _():
        lgw = _log_sigmoid(dec_wide_ref[0])
        ii = lax.broadcasted_iota(jnp.int32, (C, C), 0)
        mm = lax.broadcasted_iota(jnp.int32, (C, C), 1)
        dif = (ii - mm).astype(F32)
        neg_inf = -jnp.inf
        for h in range(H_RET):
            df = jnp.exp(jnp.where(ii >= mm, dif * lgw[h:h + 1, :], neg_inf))
            db = jnp.exp(jnp.where(mm >= ii, (-dif) * lgw[H_RET + h:H_RET + h + 1, :], neg_inf))
            dmat_ref[h] = df + db
        ri = lax.broadcasted_iota(jnp.int32, (C, W_RET), 0).astype(F32)
        wts_ref[0] = jnp.exp((C - 1.0 - ri) * lgl[0:1, :])
        wts_ref[1] = jnp.exp(ri * lgl[1:2, :])
        wts_ref[2] = jnp.exp((ri + 1.0) * lgl[0:1, :])
        wts_ref[3] = jnp.exp((C - ri) * lgl[1:2, :])

    lane = lax.broadcasted_iota(jnp.int32, (C, HEAD_PAIR), 1)
    low = lane < DK_RET
    r128 = lax.broadcasted_iota(jnp.int32, (HEAD_PAIR, HEAD_PAIR), 0) >= DK_RET
    c128 = lax.broadcasted_iota(jnp.int32, (HEAD_PAIR, HEAD_PAIR), 1) >= DK_RET
    same_head = r128 == c128
    avg = jnp.where(same_head, 1.0 / DK_RET, 0.0).astype(BF16)

    need_states = emit_state or (has_s0 and nch > 1)
    if has_s0:
        dec_f = jnp.exp(float(C) * lgl[0:1, :])
        dec_b = jnp.exp(float(C) * lgl[1:2, :])

    if rope:
        swap_low = (lax.broadcasted_iota(jnp.int32, (seq_len, HEAD_PAIR), 1) % 32) < 16
        cos = cos_ref[...]
        sin = sin_ref[...]
        for blk in range(2 * npair):
            cols_b = slice(blk * HEAD_PAIR, (blk + 1) * HEAD_PAIR)
            x = p_ref[:, cols_b]
            swapped = jnp.where(swap_low, pltpu.roll(x, HEAD_PAIR - 16, 1), pltpu.roll(x, 16, 1))
            qk_ref[:, cols_b] = x * cos + swapped * sin

    def cols(p):
        return slice(p * HEAD_PAIR, (p + 1) * HEAD_PAIR)

    if emit_state and first_layer:
        _zero_later_layers(st_ref)

    def project_seq(s):
        _project(x_ref, mod_ref, g_ref, w_ref, p_ref, slice(s * seq_len, (s + 1) * seq_len))

    def rows(s, c):
        return slice(s * seq_len + c * C, s * seq_len + (c + 1) * C)

    def get_q(s, c, p):
        src = qk_ref if rope else p_ref
        return src[rows(s, c), p * HEAD_PAIR:(p + 1) * HEAD_PAIR]

    def get_k(s, c, p):
        src = qk_ref if rope else p_ref
        return src[rows(s, c), W_RET + p * HEAD_PAIR:W_RET + (p + 1) * HEAD_PAIR]

    def get_v(s, c, p):
        return p_ref[rows(s, c), 2 * W_RET + p * HEAD_PAIR:2 * W_RET + (p + 1) * HEAD_PAIR]

    entering = {}

    def state_stage(s):
        ds_f = [[None] * npair for _ in range(nch)]
        ds_b = [[None] * npair for _ in range(nch)]
        if need_states:
            for c in range(nch):
                for p in range(npair):
                    k = get_k(s, c, p)
                    vb = get_v(s, c, p).astype(BF16)
                    kf = jnp.transpose(k * wts_ref[0, :, cols(p)]).astype(BF16)
                    kb = jnp.transpose(k * wts_ref[1, :, cols(p)]).astype(BF16)
                    ds_f[c][p] = _dot(kf, vb)
                    ds_b[c][p] = _dot(kb, vb)

        if emit_state:
            for p in range(npair):
                for a in range(2):
                    blk = slice(a * DK_RET, (a + 1) * DK_RET)
                    st_ref[s, 0, 0, 2 * p + a] = ds_f[0][p][blk, blk]
                    st_ref[s, 0, 1, 2 * p + a] = ds_b[0][p][blk, blk]

        if has_s0:
            low_half = lax.broadcasted_iota(jnp.int32, (DK_RET, HEAD_PAIR), 1) < DK_RET

            def pair_state(d, p):
                u = s0_ref[0, 0, d, p]
                return jnp.concatenate([jnp.where(low_half, u, 0.0), jnp.where(low_half, 0.0, u)], axis=0)

            for p in range(npair):
                fwd = [None] * nch
                bwd = [None] * nch
                st = pair_state(0, p)
                for c in range(nch):
                    fwd[c] = st
                    if c + 1 < nch:
                        st = st * dec_f[:, cols(p)] + jnp.where(same_head, ds_f[c][p], 0.0)
                st = pair_state(1, p)
                for c in reversed(range(nch)):
                    bwd[c] = st
                    if c > 0:
                        st = st * dec_b[:, cols(p)] + jnp.where(same_head, ds_b[c][p], 0.0)
                for c in range(nch):
                    entering[(s, c, p)] = (fwd[c], bwd[c])

    def output_stage(chunks):
        units = [(s, c, p) for s, c in chunks for p in range(npair)]
        q = {u: get_q(*u) for u in units}
        kb = {u: get_k(*u).astype(BF16) for u in units}
        vb = {u: get_v(*u).astype(BF16) for u in units}
        scores = {}
        for u in units:
            scores[u] = (_dot_nt(jnp.where(low, q[u], 0.0).astype(BF16), kb[u]),
                         _dot_nt(jnp.where(low, 0.0, q[u]).astype(BF16), kb[u]))
        inter = {}
        if has_s0:
            for u in units:
                sf, sb = entering[u]
                inter[u] = (_dot((q[u] * wts_ref[2, :, cols(u[2])]).astype(BF16), sf.astype(BF16))
                            + _dot((q[u] * wts_ref[3, :, cols(u[2])]).astype(BF16), sb.astype(BF16)))
        o = {}
        for u in units:
            p = u[2]
            o0 = _dot((scores[u][0] * dmat_ref[2 * p]).astype(BF16), vb[u])
            o1 = _dot((scores[u][1] * dmat_ref[2 * p + 1]).astype(BF16), vb[u])
            op = jnp.where(low, o0, o1)
            o[u] = op + inter[u] if has_s0 else op
        mu = {u: _dot(o[u].astype(BF16), avg) for u in units}
        d = {u: o[u] - mu[u] for u in units}
        var = {u: _dot((d[u] * d[u]).astype(BF16), avg) for u in units}
        for u in units:
            s, c, p = u
            on = d[u] * lax.rsqrt(var[u] + EPS)
            g = p_ref[rows(s, c), 3 * W_RET + p * HEAD_PAIR:3 * W_RET + (p + 1) * HEAD_PAIR]
            y_ref[rows(s, c), cols(p)] = (on * gn_ref[0, 0:1, cols(p)] * _silu(g)).astype(BF16)

    group = 2
    chunks = [(s, c) for s in range(nseq) for c in range(nch)]
    groups = [chunks[i:i + group] for i in range(0, len(chunks), group)]
    projected = {0} if nseq == 1 else set()
    prepared = set()

    def prepare(chunk_group, with_states):
        for s in sorted({s for s, _ in chunk_group}):
            if s not in projected:
                project_seq(s)
                projected.add(s)
            if with_states and s not in prepared:
                state_stage(s)
                prepared.add(s)

    prepare(groups[0], False)
    for gi, grp in enumerate(groups):
        if gi + 1 < len(groups):
            prepare(groups[gi + 1], False)
        prepare(grp, True)
        output_stage(grp)


def _retention(x2d, mods, row0, rows_per_mod, gain, w_all, dec_lane, dec_wide, gn, seq_len, l,
               rope_tabs=None, s0=None, prev_states=None):
    rope = rope_tabs is not None
    has_s0 = s0 is not None
    npair = H_RET // 2
    extra = []
    if rope:
        extra += [(t, _const_spec(t.shape)) for t in rope_tabs]
    if has_s0:
        extra.append((s0, pl.BlockSpec((1, 1, 2, npair, DK_RET, HEAD_PAIR), lambda i: (i, l, 0, 0, 0, 0))))
    scratch = [pltpu.VMEM((SEQ_TILE, W_GRP_R), F32),
               pltpu.VMEM((H_RET, SCAN_CHUNK, SCAN_CHUNK), F32), pltpu.VMEM((4, SCAN_CHUNK, W_RET), F32)]
    if rope:
        scratch.append(pltpu.VMEM((seq_len, 2 * W_RET), F32))
    return _seq_call(functools.partial(_ret_body, rope=rope, has_s0=has_s0),
                     "retention_lat" if rope else "retention_ctx",
                     x2d, mods, row0, rows_per_mod, gain, w_all, COL_R, W_GRP_R, l, seq_len,
                     (dec_lane, dec_wide, gn), extra, W_RET,
                     None if has_s0 else (2, H_RET, DK_RET, DK_RET), prev_states, scratch)


def _ssd_body(*refs, seq_len, has_s0, emit_state, first_layer):
    it = iter(refs)
    x_ref, mod_ref, g_ref, w_ref = next(it), next(it), next(it), next(it)
    cw_ref = next(it)
    cb_ref = next(it)
    dtb_ref = next(it)
    alog_ref = next(it)
    dskip_ref = next(it)
    gn_ref = next(it)
    s0_ref = st_ref = None
    if has_s0:
        s0_ref = next(it)
    if emit_state and not first_layer:
        next(it)
    y_ref = next(it)
    if emit_state:
        st_ref = next(it)
    p_ref = next(it)
    xs_ref = next(it)
    bc_ref = next(it)

    T = seq_len
    C = SCAN_CHUNK
    nch = T // C
    n_rows = p_ref.shape[0]
    nseq = n_rows // T
    assert not has_s0 or nseq == 1
    assert T & (T - 1) == 0
    pos = lax.broadcasted_iota(jnp.int32, (T, LANES), 0)

    def project_and_conv(s):
        seq_rows = slice(s * T, (s + 1) * T)
        _project(x_ref, mod_ref, g_ref, w_ref, p_ref, seq_rows)
        for blk in range(CONV_CH // LANES):
            x = p_ref[seq_rows, W_SSD + blk * LANES:W_SSD + (blk + 1) * LANES]
            a = _silu(_dwconv4(x, cw_ref, cb_ref, blk * LANES, pos, T))
            if blk < W_SSD // LANES:
                xs_ref[seq_rows, blk * LANES:(blk + 1) * LANES] = a
            else:
                o = blk * LANES - W_SSD
                bc_ref[seq_rows, o:o + LANES] = a

    a_neg = -jnp.exp(alog_ref[0])
    ii = lax.broadcasted_iota(jnp.int32, (C, C), 0)
    mm = lax.broadcasted_iota(jnp.int32, (C, C), 1)
    lower = ii >= mm
    upper = mm >= ii
    tri_l = jnp.where(lower, 1.0, 0.0).astype(BF16)
    tri_u = jnp.where(upper, 1.0, 0.0).astype(BF16)
    neg_inf = -jnp.inf
    grp0 = lax.broadcasted_iota(jnp.int32, (C, W_SSD), 1) < (W_SSD // G_SSD)
    grp0_s = lax.broadcasted_iota(jnp.int32, (N_SSD, W_SSD), 1) < (W_SSD // G_SSD)
    grp0_t = lax.broadcasted_iota(jnp.int32, (W_SSD, N_SSD), 0) < (W_SSD // G_SSD)
    low = lax.broadcasted_iota(jnp.int32, (C, HEAD_PAIR), 1) < P_SSD

    if emit_state and first_layer:
        _zero_later_layers(st_ref)

    scalars = {}

    def chunk_scalars(s, c):
        if (s, c) not in scalars:
            rows = slice(s * T + c * C, s * T + (c + 1) * C)
            dt = _softplus(p_ref[rows, W_SSD + CONV_CH:W_SSD + CONV_CH + LANES] + dtb_ref[0])
            la = dt * a_neg
            a1 = la.astype(BF16)
            r1 = la - a1.astype(F32)
            a2 = r1.astype(BF16)
            a3 = (r1 - a2.astype(F32)).astype(BF16)
            pre = _dot(tri_l, a1) + _dot(tri_l, a2) + _dot(tri_l, a3)
            suf = _dot(tri_u, a1) + _dot(tri_u, a2) + _dot(tri_u, a3)
            scalars[(s, c)] = (dt, pre, suf)
        return scalars[(s, c)]

    def light_stage(s):
        def rows(c):
            return slice(s * T + c * C, s * T + (c + 1) * C)

        def state_weights(c):
            dt, pre, suf = chunk_scalars(s, c)
            wf = jnp.exp(pre[C - 1:C, :] - pre) * dt
            wb = jnp.exp(suf[0:1, :] - suf) * dt
            xs = xs_ref[rows(c), :]
            return (xs * _head_lane_expand(wf, 0), xs * _head_lane_expand(wb, DT_BWD_LANE), pre, suf)

        if emit_state:
            xf, xb, _, _ = state_weights(0)
            bm = [bc_ref[rows(0), g * N_SSD:(g + 1) * N_SSD].astype(BF16) for g in range(G_SSD)]
            for d, xw in enumerate((xf, xb)):
                xt = jnp.transpose(xw).astype(BF16)
                st = jnp.where(grp0_t, _dot(xt, bm[0]), _dot(xt, bm[1]))
                for h in range(H_SSD):
                    st_ref[s, 0, d, h] = st[h * P_SSD:(h + 1) * P_SSD, :]

        ds_f = [None] * nch
        ds_b = [None] * nch
        dec_f = [None] * nch
        dec_b = [None] * nch
        if has_s0 and nch > 1:
            for c in range(nch):
                xf, xb, pre, suf = state_weights(c)
                bt0 = jnp.transpose(bc_ref[rows(c), 0:N_SSD]).astype(BF16)
                bt1 = jnp.transpose(bc_ref[rows(c), N_SSD:2 * N_SSD]).astype(BF16)
                ds_f[c] = jnp.where(grp0_s, _dot(bt0, xf.astype(BF16)), _dot(bt1, xf.astype(BF16)))
                ds_b[c] = jnp.where(grp0_s, _dot(bt0, xb.astype(BF16)), _dot(bt1, xb.astype(BF16)))
                dec_f[c] = _head_lane_expand(jnp.exp(pre[C - 1:C, :]), 0)
                dec_b[c] = _head_lane_expand(jnp.exp(suf[0:1, :]), DT_BWD_LANE)

        sf_in = [None] * nch
        sb_in = [None] * nch
        if has_s0:
            st = s0_ref[0, 0, 0]
            for c in range(nch):
                sf_in[c] = st
                if c + 1 < nch:
                    st = st * dec_f[c] + ds_f[c]
            st = s0_ref[0, 0, 1]
            for c in reversed(range(nch)):
                sb_in[c] = st
                if c > 0:
                    st = st * dec_b[c] + ds_b[c]

        staged = []
        for c in range(nch):
            dt, pre, suf = chunk_scalars(s, c)
            cm = [bc_ref[rows(c), 2 * N_SSD + g * N_SSD:2 * N_SSD + (g + 1) * N_SSD].astype(BF16)
                  for g in range(G_SSD)]
            bm = [bc_ref[rows(c), g * N_SSD:(g + 1) * N_SSD].astype(BF16) for g in range(G_SSD)]
            gram = [_dot_nt(cm[g], bm[g]) for g in range(G_SSD)]
            inter = None
            if has_s0:
                sf = sf_in[c].astype(BF16)
                sb = sb_in[c].astype(BF16)
                inter = (jnp.where(grp0, _dot(cm[0], sf), _dot(cm[1], sf)),
                         jnp.where(grp0, _dot(cm[0], sb), _dot(cm[1], sb)))
            log_dt = jnp.log(dt)
            staged.append((pre, suf, jnp.transpose(pre - log_dt), jnp.transpose(suf - log_dt), gram, inter))
        return staged

    def heavy_stage(s, staged):
        for c in range(nch):
            rows_c = slice(s * T + c * C, s * T + (c + 1) * C)
            pre, suf, pre_t, suf_t, gram, inter = staged[c]
            xs = xs_ref[rows_c, :]
            xs_b = xs.astype(BF16)
            heads = []
            for h in range(H_SSD):
                g = h // (H_SSD // G_SSD)
                hb = DT_BWD_LANE + h
                df = jnp.exp(jnp.where(lower, pre[:, h:h + 1] - pre_t[h:h + 1, :], neg_inf))
                db = jnp.exp(jnp.where(upper, suf[:, hb:hb + 1] - suf_t[hb:hb + 1, :], neg_inf))
                w = gram[g] * (df + db)
                p = h // 2
                heads.append(_dot(w.astype(BF16), xs_b[:, p * HEAD_PAIR:(p + 1) * HEAD_PAIR]))
            o = jnp.concatenate([jnp.where(low, heads[2 * p], heads[2 * p + 1]) for p in range(H_SSD // 2)],
                                axis=1)
            if has_s0:
                o = o + _head_lane_expand(jnp.exp(pre), 0) * inter[0]
                o = o + _head_lane_expand(jnp.exp(suf), DT_BWD_LANE) * inter[1]
            y = o + dskip_ref[0] * xs
            yz = y * _silu(p_ref[rows_c, 0:W_SSD])
            y_ref[rows_c, :] = (_rms_scale(yz) * gn_ref[0]).astype(BF16)

    project_and_conv(0)
    staged = light_stage(0)
    for s in range(nseq):
        staged_next = None
        if s + 1 < nseq:
            project_and_conv(s + 1)
            staged_next = light_stage(s + 1)
        heavy_stage(s, staged)
        staged = staged_next


def _ssd(x2d, mods, row0, rows_per_mod, gain, w_all, cw, cb, dtb, alog, dskip, gn, seq_len, l,
         s0=None, prev_states=None):
    has_s0 = s0 is not None
    extra = []
    if has_s0:
        extra.append((s0, pl.BlockSpec((1, 1, 2, N_SSD, W_SSD), lambda i: (i, l, 0, 0, 0))))
    scratch = [pltpu.VMEM((SEQ_TILE, W_GRP_S), F32), pltpu.VMEM((SEQ_TILE, W_SSD), F32),
               pltpu.VMEM((SEQ_TILE, 2 * G_SSD * N_SSD), F32)]
    return _seq_call(functools.partial(_ssd_body, has_s0=has_s0), "ssd_lat" if has_s0 else "ssd_ctx",
                     x2d, mods, row0, rows_per_mod, gain, w_all, COL_S, W_SLOT_S, l, seq_len,
                     (cw, cb, dtb, alog, dskip, gn), extra, W_SSD,
                     None if has_s0 else (2, H_SSD, P_SSD, N_SSD), prev_states, scratch)


def _lru_body(*refs, seq_len, has_h0, emit_state, first_layer):
    it = iter(refs)
    x_ref, mod_ref, g_ref, w_ref = next(it), next(it), next(it), next(it)
    cw_ref = next(it)
    cb_ref = next(it)
    wg_ref = next(it)
    bg_ref = next(it)
    lam_ref = next(it)
    h0_ref = st_ref = None
    if has_h0:
        h0_ref = next(it)
    if emit_state and not first_layer:
        next(it)
    y_ref = next(it)
    if emit_state:
        st_ref = next(it)
    p_ref = next(it)

    T = seq_len
    n_rows = p_ref.shape[0]
    nseq = n_rows // T
    assert not has_h0 or nseq == 1
    assert T & (T - 1) == 0
    _project(x_ref, mod_ref, g_ref, w_ref, p_ref)
    row = lax.broadcasted_iota(jnp.int32, (n_rows, W_LRU), 0)
    pos = row & (T - 1)
    sub = row & (SUBLANES - 1)
    xc = _dwconv4(p_ref[:, 0:W_LRU], cw_ref, cb_ref, 0, pos, T)
    gates = _sigmoid(_dot(xc.astype(BF16), wg_ref[0]) + bg_ref[0])
    decay_rate = _softplus(-lam_ref[0])
    if emit_state and first_layer:
        _zero_later_layers(st_ref)
    nblk = T // SUBLANES
    total = None
    for d in range(2):
        r = gates[:, 2 * d * W_LRU:(2 * d + 1) * W_LRU]
        i = gates[:, (2 * d + 1) * W_LRU:(2 * d + 2) * W_LRU]
        log_a = (-LRU_C * r) * decay_rate[d:d + 1, :]
        a = jnp.exp(log_a)
        th = jnp.tanh(log_a)
        b = jnp.sqrt(-2.0 * th / (1.0 - th)) * (i * xc)
        step = 1
        while step < SUBLANES:
            if d == 0:
                ok = sub >= step
                shift = step
            else:
                ok = sub <= SUBLANES - 1 - step
                shift = n_rows - step
            a_prev = jnp.where(ok, pltpu.roll(a, shift, 0), 1.0)
            b_prev = jnp.where(ok, pltpu.roll(b, shift, 0), 0.0)
            b = a * b_prev + b
            a = a * a_prev
            step *= 2
        edge = SUBLANES - 1 if d == 0 else 0
        tiles = [None] * (nseq * nblk)
        for s in range(nseq):
            carry = h0_ref[0, 0, d:d + 1, :] if has_h0 else None
            for v in (range(nblk) if d == 0 else reversed(range(nblk))):
                t = s * nblk + v
                rows = slice(t * SUBLANES, (t + 1) * SUBLANES)
                hv = b[rows]
                if carry is not None:
                    hv = hv + a[rows] * carry
                tiles[t] = hv
                carry = hv[edge:edge + 1, :]
            if emit_state:
                st_ref[s, 0, d:d + 1, :] = carry
        hs = jnp.concatenate(tiles, axis=0)
        total = hs if total is None else total + hs
    y_ref[...] = (total * _gelu_tanh(p_ref[:, W_LRU:2 * W_LRU])).astype(BF16)


def _lru(x2d, mods, row0, rows_per_mod, gain, w_all, cw, cb, wg, bg, lam, seq_len, l, h0=None, prev_states=None):
    has_h0 = h0 is not None
    extra = []
    if has_h0:
        extra.append((h0, pl.BlockSpec((1, 1, 2, W_LRU), lambda i: (i, l, 0, 0))))
    scratch = [pltpu.VMEM((SEQ_TILE, W_GRP_L), F32)]
    return _seq_call(functools.partial(_lru_body, has_h0=has_h0), "lru_lat" if has_h0 else "lru_ctx",
                     x2d, mods, row0, rows_per_mod, gain, w_all, COL_L, W_GRP_L, l, seq_len,
                     (cw, cb, wg, bg, lam), extra, W_LRU,
                     None if has_h0 else (2, W_LRU), prev_states, scratch)


def _outproj_body(x_ref, yr_ref, ys_ref, yl_ref, mod_ref, g_ref, wo_ref, x1_ref, h2_ref):
    mix = _dot(yr_ref[...], wo_ref[0, 0:W_RET, :])
    mix = mix + _dot(ys_ref[...], wo_ref[0, W_RET:W_RET + W_SSD, :])
    mix = mix + _dot(yl_ref[...], wo_ref[0, W_RET + W_SSD:D_MODEL, :])
    x1 = x_ref[...] + mod_ref[0, 0, 2:3, :] * mix
    x1_ref[...] = x1
    y = _rms_scale(x1) * g_ref[0]
    h2_ref[...] = (y * (1.0 + mod_ref[0, 0, 4:5, :]) + mod_ref[0, 0, 3:4, :]).astype(BF16)


def _out_proj(x2d, y_ret, y_ssd, y_lru, mods, row0, rows_per_mod, gain, wo, l):
    n = x2d.shape[0]
    tm = ROW_TILE
    row_spec = lambda w: pl.BlockSpec((tm, w), lambda i: (i, 0))
    return pl.pallas_call(
        _outproj_body,
        grid=(n // tm,),
        in_specs=[row_spec(D_MODEL), row_spec(W_RET), row_spec(W_SSD), row_spec(W_LRU),
                  _mod_spec(row0, rows_per_mod, tm, l), _layer_spec(gain, l), _layer_spec(wo, l)],
        out_specs=[row_spec(D_MODEL), row_spec(D_MODEL)],
        out_shape=[jax.ShapeDtypeStruct((n, D_MODEL), F32), jax.ShapeDtypeStruct((n, D_MODEL), BF16)],
        compiler_params=_params(1),
        name="out_proj",
    )(x2d, y_ret, y_ssd, y_lru, mods, gain, wo)


def _ffn_core(h, wup_ref, cw_ref, cb_ref, wd_ref, act_ref, seq_len):
    tm = h.shape[0]
    nseq = tm // seq_len
    row8 = lax.broadcasted_iota(jnp.int32, (SUBLANES, FF_BLOCK), 0)
    first_row = row8 == 0
    last_row = row8 == SUBLANES - 1

    def conv3(u, col0):
        cols = slice(col0, col0 + FF_BLOCK)
        w_prev, w_mid, w_next = (cw_ref[0, t:t + 1, cols] for t in range(3))
        bias = cb_ref[0, 0:1, cols]
        outs = []
        for s in range(nseq):
            us = u[s * seq_len:(s + 1) * seq_len]
            prev = pltpu.roll(us, 1, 0)
            nxt = pltpu.roll(us, seq_len - 1, 0)
            prev = jnp.concatenate([jnp.where(first_row, 0.0, prev[:SUBLANES]), prev[SUBLANES:]], axis=0)
            nxt = jnp.concatenate([nxt[:seq_len - SUBLANES], jnp.where(last_row, 0.0, nxt[seq_len - SUBLANES:])],
                                  axis=0)
            outs.append(us * w_mid + bias + prev * w_prev + nxt * w_next)
        return outs[0] if nseq == 1 else jnp.concatenate(outs, axis=0)

    def up(j):
        return (_dot(h, wup_ref[0, :, j * FF_BLOCK:(j + 1) * FF_BLOCK]),
                _dot(h, wup_ref[0, :, D_FF + j * FF_BLOCK:D_FF + (j + 1) * FF_BLOCK]))

    for j in range(N_FF_BLOCKS):
        uv, ug = up(j)
        val = conv3(uv, j * FF_BLOCK)
        gate = conv3(ug, D_FF + j * FF_BLOCK)
        act_ref[:, j * FF_BLOCK:(j + 1) * FF_BLOCK] = (_silu(gate) * val).astype(BF16)
    return _dot(act_ref[...], wd_ref[0])


def _ffn_body(h_ref, x1_ref, mod_ref, wup_ref, cw_ref, cb_ref, wd_ref, fg_ref, o_ref, act_ref, *, seq_len,
              final_norm):
    down = _ffn_core(h_ref[...], wup_ref, cw_ref, cb_ref, wd_ref, act_ref, seq_len)
    out = x1_ref[...] + mod_ref[0, 0, 5:6, :] * down
    if final_norm:
        out = _rms_scale(out) * fg_ref[...]
    o_ref[...] = out


def _mix_ffn_body(x_ref, yr_ref, ys_ref, yl_ref, mod_ref, g_ref, wo_ref, wup_ref, cw_ref, cb_ref, wd_ref, fg_ref,
                  o_ref, act_ref, h_ref, *, seq_len, final_norm):
    mix = _dot(yr_ref[...], wo_ref[0, 0:W_RET, :])
    mix = mix + _dot(ys_ref[...], wo_ref[0, W_RET:W_RET + W_SSD, :])
    mix = mix + _dot(yl_ref[...], wo_ref[0, W_RET + W_SSD:D_MODEL, :])
    x1 = x_ref[...] + mod_ref[0, 0, 2:3, :] * mix
    o_ref[...] = x1
    y = _rms_scale(x1) * g_ref[0]
    h_ref[...] = (y * (1.0 + mod_ref[0, 0, 4:5, :]) + mod_ref[0, 0, 3:4, :]).astype(BF16)
    down = _ffn_core(h_ref[...], wup_ref, cw_ref, cb_ref, wd_ref, act_ref, seq_len)
    out = o_ref[...] + mod_ref[0, 0, 5:6, :] * down
    if final_norm:
        out = _rms_scale(out) * fg_ref[...]
    o_ref[...] = out


def _mix_ffn(x2d, y_ret, y_ssd, y_lru, mods, row0, rows_per_mod, gain, wo, wup, cw, cb, wd, fgain, seq_len,
             final_norm, l):
    n = x2d.shape[0]
    tm = ROW_TILE if ROW_TILE % seq_len == 0 else seq_len
    assert tm % seq_len == 0
    row_spec = lambda w: pl.BlockSpec((tm, w), lambda i: (i, 0))
    return pl.pallas_call(
        functools.partial(_mix_ffn_body, seq_len=seq_len, final_norm=final_norm),
        grid=(n // tm,),
        in_specs=[row_spec(D_MODEL), row_spec(W_RET), row_spec(W_SSD), row_spec(W_LRU),
                  _mod_spec(row0, rows_per_mod, tm, l), _layer_spec(gain, l), _layer_spec(wo, l),
                  _layer_spec(wup, l), _layer_spec(cw, l), _layer_spec(cb, l), _layer_spec(wd, l),
                  _const_spec(fgain.shape)],
        out_specs=row_spec(D_MODEL),
        out_shape=jax.ShapeDtypeStruct((n, D_MODEL), F32),
        scratch_shapes=[pltpu.VMEM((tm, D_FF), BF16), pltpu.VMEM((tm, D_MODEL), BF16)],
        compiler_params=_params(1),
        name="mix_ffn",
    )(x2d, y_ret, y_ssd, y_lru, mods, gain, wo, wup, cw, cb, wd, fgain)


def _ffn(h2, x1, mods, row0, rows_per_mod, wup, cw, cb, wd, fgain, seq_len, final_norm, l):
    n = h2.shape[0]
    tm = SEQ_TILE
    row_spec = pl.BlockSpec((tm, D_MODEL), lambda i: (i, 0))
    return pl.pallas_call(
        functools.partial(_ffn_body, seq_len=seq_len, final_norm=final_norm),
        grid=(n // tm,),
        in_specs=[row_spec, row_spec, _mod_spec(row0, rows_per_mod, tm, l),
                  _layer_spec(wup, l), _layer_spec(cw, l), _layer_spec(cb, l), _layer_spec(wd, l),
                  _const_spec(fgain.shape)],
        out_specs=row_spec,
        out_shape=jax.ShapeDtypeStruct((n, D_MODEL), F32),
        scratch_shapes=[pltpu.VMEM((tm, D_FF), BF16)],
        compiler_params=_params(1),
        name="ffn",
    )(h2, x1, mods, wup, cw, cb, wd, fgain)


def _rope_tables(t_len):
    tok = jnp.arange(t_len)
    row_pos = (tok // GRID_W).astype(F32)
    col_pos = (tok % GRID_W).astype(F32)
    lane = jnp.arange(HEAD_PAIR)
    d = lane % DK_RET
    use_row = d < (DK_RET // 2)
    half = DK_RET // 4
    freqs = ROPE_BASE ** (-(d % half).astype(F32) / half)
    ang = jnp.where(use_row[None, :], row_pos[:, None], col_pos[:, None]) * freqs[None, :]
    first = (d % (2 * half)) < half
    return jnp.cos(ang), jnp.where(first[None, :], -jnp.sin(ang), jnp.sin(ang))


def _prep_params(w_in, ret_decay, ret_norm_g, ssd_conv_b, ssd_dt_bias, ssd_a_log, ssd_d, ssd_norm_g,
                 lru_conv_b, lru_w_a, lru_b_a, lru_w_x, lru_b_x, w_out, ffn_w_up, ffn_conv_b, ffn_w_down):
    w_all = _regroup_w_in(w_in)

    def lane_place(pair):
        z = jnp.zeros((DEPTH, LANES), F32)
        return jnp.concatenate([pair[:, 0], z[:, :DT_BWD_LANE - H_SSD], pair[:, 1],
                                z[:, :LANES - DT_BWD_LANE - H_SSD]], axis=-1)[:, None, :]

    eye = jnp.eye(LRU_BLOCKS, dtype=F32)
    gates = jnp.stack([lru_w_a, lru_w_x], axis=2)
    wgate = jnp.einsum("lxgncd,nm->lncxgmd", gates, eye).reshape(DEPTH, W_LRU, 4 * W_LRU).astype(BF16)
    bgate = jnp.stack([lru_b_a, lru_b_x], axis=2).reshape(DEPTH, 1, 4 * W_LRU)
    return dict(
        w_all=w_all,
        dec_lane=jnp.repeat(ret_decay, DK_RET, axis=-1),
        dec_wide=jnp.broadcast_to(ret_decay.reshape(DEPTH, 2 * H_RET, 1), (DEPTH, 2 * H_RET, SCAN_CHUNK)),
        ret_gn=ret_norm_g[:, None, :],
        ssd_cb=ssd_conv_b[:, None, :],
        ssd_dtb=lane_place(ssd_dt_bias), ssd_alog=lane_place(ssd_a_log),
        ssd_dskip=jnp.repeat(ssd_d, P_SSD, axis=-1)[:, None, :], ssd_gn=ssd_norm_g[:, None, :],
        lru_cb=lru_conv_b[:, None, :], lru_wg=wgate, lru_bg=bgate,
        wo=w_out.astype(BF16), wup=ffn_w_up.astype(BF16), ffn_cb=ffn_conv_b[:, None, :],
        wd=ffn_w_down.astype(BF16),
    )


def kernel(x_prompt, x_sample, c, c_ctx, state_ret, state_ssd, state_lru, w_ada, b_ada, norm1_g, norm2_g, w_in, ret_decay, ret_norm_g, ssd_conv_w, ssd_conv_b, ssd_dt_bias, ssd_a_log, ssd_d, ssd_norm_g, lru_conv_w, lru_conv_b, lru_w_a, lru_b_a, lru_w_x, lru_b_x, lru_lambda, w_out, ffn_w_up, ffn_conv_w, ffn_conv_b, ffn_w_down, final_norm_g):
    bp, tp, _ = x_prompt.shape
    bs, ts, _ = x_sample.shape
    assert tp == SCAN_CHUNK and ts == SEQ_TILE and ts % GRID_W == 0 and (bp * tp) % SEQ_TILE == 0

    n_req = 8
    cvec = jnp.concatenate([c_ctx[None, :], c, jnp.zeros((n_req - 1 - bs, D_MODEL), F32)], axis=0)
    mods = _modulation(cvec, w_ada, b_ada).reshape(DEPTH, n_req, 6, D_MODEL)

    pp = _prep_params(w_in, ret_decay, ret_norm_g, ssd_conv_b, ssd_dt_bias, ssd_a_log, ssd_d, ssd_norm_g,
                      lru_conv_b, lru_w_a, lru_b_a, lru_w_x, lru_b_x, w_out, ffn_w_up, ffn_conv_b, ffn_w_down)
    n1 = norm1_g[:, None, :]
    n2 = norm2_g[:, None, :]
    rope_tabs = _rope_tables(ts)
    fgain = final_norm_g[None, :]

    npair = H_RET // 2
    s0_ret = state_ret.reshape(bs, DEPTH, 2, npair, 2, DK_RET, DK_RET).transpose(0, 1, 2, 3, 5, 4, 6).reshape(
        bs, DEPTH, 2, npair, DK_RET, HEAD_PAIR)
    s0_ssd = state_ssd.transpose(0, 1, 2, 4, 3, 5).reshape(bs, DEPTH, 2, N_SSD, W_SSD)

    def trunk_layer(x2d, l, seq_len, row0, rows_per_mod, latent, prev=(None, None, None)):
        common = (x2d, mods, row0, rows_per_mod, n1, pp["w_all"])
        ret = _retention(*common, pp["dec_lane"], pp["dec_wide"], pp["ret_gn"], seq_len, l,
                         rope_tabs if latent else None, s0_ret if latent else None, prev[0])
        ssd = _ssd(*common, ssd_conv_w, pp["ssd_cb"], pp["ssd_dtb"], pp["ssd_alog"], pp["ssd_dskip"],
                   pp["ssd_gn"], seq_len, l, s0_ssd if latent else None, prev[1])
        lru = _lru(*common, lru_conv_w, pp["lru_cb"], pp["lru_wg"], pp["lru_bg"], lru_lambda, seq_len, l,
                   state_lru if latent else None, prev[2])
        ffn_args = (pp["wup"], ffn_conv_w, pp["ffn_cb"], pp["wd"], fgain, seq_len, l == DEPTH - 1, l)
        out = _mix_ffn(x2d, ret[0], ssd[0], lru[0], mods, row0, rows_per_mod, n2, pp["wo"], *ffn_args)
        states = None if latent else (ret[1], ssd[1], lru[1])
        return out, states

    xp = x_prompt.reshape(bp * tp, D_MODEL)
    xs = x_sample.reshape(bs * ts, D_MODEL)
    states = (None, None, None)
    for l in range(DEPTH):
        xp, states = trunk_layer(xp, l, tp, 0, bp * tp, False, states)
        xs, _ = trunk_layer(xs, l, ts, 1, ts, True)

    return (xp.reshape(bp, tp, D_MODEL), xs.reshape(bs, ts, D_MODEL), states[0],
            jnp.swapaxes(states[1], -1, -2), states[2])
```
